```python
import math
import jax, jax.numpy as jnp
from jax import lax
import numpy as np

D_MODEL = 1024
BATCH = 8
SEQ = 4096
DEPTH = 4

PLE_DIM = 256
N_BRANCH = 4
BRANCH_W = D_MODEL // N_BRANCH
CONV_W = 31
MLA_HEADS = 4
MLA_NOPE = 64
MLA_ROPE = 32
MLA_V = 64
MLA_Q_RANK = D_MODEL // 4
MLA_KV_RANK = D_MODEL // 8
ROPE_THETA = 10000.0
ATTN_BLOCK = 128
SSM_GROUP = 16
SSM_GROUPS = BRANCH_W // SSM_GROUP
SSM_STATE = 64
DT_MIN = 1e-3
DT_MAX = 1e-1
SWA_HEADS = 4
SWA_KV_HEADS = 2
SWA_HEAD_DIM = 64
WINDOW = 128
DEEPNORM_ALPHA = (2.0 * DEPTH) ** 0.25
DEEPNORM_BETA = (8.0 * DEPTH) ** -0.25
LN_EPS = 1e-5
RMS_EPS = 1e-6

IN_SPLITS = (
    BRANCH_W, BRANCH_W, BRANCH_W,
    MLA_Q_RANK, MLA_KV_RANK, MLA_ROPE, BRANCH_W,
    BRANCH_W, BRANCH_W,
    SWA_HEADS * SWA_HEAD_DIM, SWA_KV_HEADS * SWA_HEAD_DIM,
    SWA_KV_HEADS * SWA_HEAD_DIM, BRANCH_W,
)
IN_WIDTH = sum(IN_SPLITS)
IN_OFFSETS = tuple(sum(IN_SPLITS[:i + 1]) for i in range(len(IN_SPLITS) - 1))

kernel_name = 'hybrid_gated_parallel_mixers'


def layer_norm(x, g, b):
    xf = x.astype(jnp.float32)
    mu = jnp.mean(xf, axis=-1, keepdims=True)
    var = jnp.mean(jnp.square(xf - mu), axis=-1, keepdims=True)
    return ((xf - mu) * lax.rsqrt(var + LN_EPS) * g.astype(jnp.float32) + b.astype(jnp.float32)).astype(x.dtype)


def rms_norm(x, g):
    xf = x.astype(jnp.float32)
    ms = jnp.mean(jnp.square(xf), axis=-1, keepdims=True)
    return (xf * lax.rsqrt(ms + RMS_EPS) * g.astype(jnp.float32)).astype(x.dtype)


def rope_tables(seq):
    pos = jnp.arange(seq, dtype=jnp.float32)
    inv_freq = ROPE_THETA ** (-jnp.arange(0, MLA_ROPE, 2, dtype=jnp.float32) / MLA_ROPE)
    ang = pos[:, None] * inv_freq[None, :]
    return jnp.cos(ang), jnp.sin(ang)


def rope(x, cos, sin):
    half = x.shape[-1] // 2
    xf = x.astype(jnp.float32)
    x1, x2 = xf[..., :half], xf[..., half:]
    return jnp.concatenate([x1 * cos - x2 * sin, x2 * cos + x1 * sin], axis=-1).astype(x.dtype)


def conv_module(a_val, a_gate, conv_w, conv_b, norm_g, norm_b, w_pw2):
    h = a_val * jax.nn.sigmoid(a_gate)
    h = jnp.pad(h, ((0, 0), (CONV_W - 1, 0), (0, 0)))
    h = lax.conv_general_dilated(h, conv_w[:, None, :].astype(h.dtype), window_strides=(1,),
                                 padding='VALID', dimension_numbers=('NWC', 'WIO', 'NWC'),
                                 feature_group_count=BRANCH_W) + conv_b
    h = jax.nn.silu(layer_norm(h, norm_g, norm_b))
    return h @ w_pw2


def mla(c_q, c_kv, k_r, q_norm_g, kv_norm_g, w_uq, w_ukv, cos, sin):
    B, S, _ = c_q.shape
    q = (rms_norm(c_q, q_norm_g) @ w_uq).reshape(B, S, MLA_HEADS, MLA_NOPE + MLA_ROPE)
    q_nope = q[..., :MLA_NOPE]
    q_rope = rope(q[..., MLA_NOPE:], cos[:, None, :], sin[:, None, :])
    kv = (rms_norm(c_kv, kv_norm_g) @ w_ukv).reshape(B, S, MLA_HEADS, MLA_NOPE + MLA_V)
    k_nope, v = kv[..., :MLA_NOPE], kv[..., MLA_NOPE:]
    k_rope = rope(k_r, cos, sin)
    scale = (MLA_NOPE + MLA_ROPE) ** -0.5
    nblk = S // ATTN_BLOCK
    qn = q_nope.reshape(B, nblk, ATTN_BLOCK, MLA_HEADS, MLA_NOPE).transpose(1, 0, 2, 3, 4)
    qr = q_rope.reshape(B, nblk, ATTN_BLOCK, MLA_HEADS, MLA_ROPE).transpose(1, 0, 2, 3, 4)
    k_pos = jnp.arange(S)

    def block(args):
        qn_b, qr_b, i = args
        s = (jnp.einsum('bqhd,bkhd->bhqk', qn_b, k_nope)
             + jnp.einsum('bqhr,bkr->bhqk', qr_b, k_rope)).astype(jnp.float32) * scale
        q_pos = i * ATTN_BLOCK + jnp.arange(ATTN_BLOCK)
        s = jnp.where(k_pos[None, :] <= q_pos[:, None], s, -jnp.inf)
        prob = jax.nn.softmax(s, axis=-1).astype(v.dtype)
        return jnp.einsum('bhqk,bkhd->bqhd', prob, v)

    o = lax.map(block, (qn, qr, jnp.arange(nblk)))
    return o.transpose(1, 0, 2, 3, 4).reshape(B, S, MLA_HEADS * MLA_V)


def s5_layer(u, a_re, a_im, log_dt, b_re, b_im, c_re, c_im, d, w_glu):
    B, S, W = u.shape
    f32 = jnp.float32
    uf = u.astype(f32)
    ug = uf.reshape(B, S, SSM_GROUPS, SSM_GROUP)
    dt = jnp.exp(log_dt.astype(f32))[:, None]
    lr, li = a_re.astype(f32), a_im.astype(f32)
    mag = jnp.exp(lr * dt)
    lb_re, lb_im = mag * jnp.cos(li * dt), mag * jnp.sin(li * dt)
    den = lr * lr + li * li
    nr, ni = lb_re - 1.0, lb_im
    f_re = ((nr * lr + ni * li) / den)[..., None]
    f_im = ((ni * lr - nr * li) / den)[..., None]
    br, bi = b_re.astype(f32), b_im.astype(f32)
    bb_re = f_re * br - f_im * bi
    bb_im = f_re * bi + f_im * br
    bu_re = jnp.einsum('bsgh,gph->bsgp', ug, bb_re)
    bu_im = jnp.einsum('bsgh,gph->bsgp', ug, bb_im)
    a_r = jnp.broadcast_to(lb_re[None, None], (1, S, SSM_GROUPS, SSM_STATE))
    a_i = jnp.broadcast_to(lb_im[None, None], (1, S, SSM_GROUPS, SSM_STATE))

    def combine(e1, e2):
        a1r, a1i, x1r, x1i = e1
        a2r, a2i, x2r, x2i = e2
        return (a2r * a1r - a2i * a1i, a2r * a1i + a2i * a1r,
                a2r * x1r - a2i * x1i + x2r, a2r * x1i + a2i * x1r + x2i)

    _, _, h_re, h_im = lax.associative_scan(combine, (a_r, a_i, bu_re, bu_im), axis=1)
    y = (jnp.einsum('bsgp,ghp->bsgh', h_re, c_re.astype(f32))
         - jnp.einsum('bsgp,ghp->bsgh', h_im, c_im.astype(f32))).reshape(B, S, W)
    y = jax.nn.gelu(y + d.astype(f32) * uf).astype(u.dtype)
    g = y @ w_glu
    return g[..., :W] * jax.nn.sigmoid(g[..., W:])


def swa(q, k, v, sinks):
    B, S, _ = q.shape
    nb = S // WINDOW
    G = SWA_HEADS // SWA_KV_HEADS
    q = q.reshape(B, nb, WINDOW, SWA_KV_HEADS, G, SWA_HEAD_DIM)
    k = k.reshape(B, nb, WINDOW, SWA_KV_HEADS, SWA_HEAD_DIM)
    v = v.reshape(B, nb, WINDOW, SWA_KV_HEADS, SWA_HEAD_DIM)
    prev = lambda t: jnp.concatenate([jnp.zeros_like(t[:, :1]), t[:, :-1]], axis=1)
    k2 = jnp.concatenate([prev(k), k], axis=2)
    v2 = jnp.concatenate([prev(v), v], axis=2)
    s = jnp.einsum('bnqhgd,bnkhd->bnhgqk', q, k2).astype(jnp.float32) * (SWA_HEAD_DIM ** -0.5)
    qi = jnp.arange(WINDOW)[:, None] + WINDOW
    kj = jnp.arange(2 * WINDOW)[None, :]
    rel = qi - kj
    band = (rel >= 0) & (rel < WINDOW)
    blk = jnp.arange(nb)[:, None, None]
    valid = band[None] & ((blk > 0) | (kj >= WINDOW)[None])
    s = jnp.where(valid[None, :, None, None], s, -jnp.inf)
    sink = sinks.astype(jnp.float32).reshape(SWA_KV_HEADS, G)
    sink_col = jnp.broadcast_to(sink[None, None, :, :, None, None], s.shape[:-1] + (1,))
    prob = jax.nn.softmax(jnp.concatenate([s, sink_col], axis=-1), axis=-1)[..., :-1]
    o = jnp.einsum('bnhgqk,bnkhd->bnqhgd', prob.astype(v.dtype), v2)
    return o.reshape(B, S, SWA_HEADS * SWA_HEAD_DIM)


def hybrid_layer(x, p_i, cos, sin, w_in, w_merge, b_merge, conv_w, conv_b, conv_norm_g, conv_norm_b,
                 w_pw2, mla_q_norm_g, mla_kv_norm_g, w_uq, w_ukv, ssm_a_re, ssm_a_im, ssm_log_dt,
                 ssm_b_re, ssm_b_im, ssm_c_re, ssm_c_im, ssm_d, w_glu, attn_sinks, w_branch, w_out,
                 ln_g, ln_b, w_ple, w_ple_gate, ple_norm_g):
    B, S, D = x.shape
    h = x @ w_in
    (a_val, a_gate, a_z, c_q, c_kv, k_r, b_z, u, c_z, q, k, v, d_z) = jnp.split(h, IN_OFFSETS, axis=-1)
    y_a = conv_module(a_val, a_gate, conv_w, conv_b, conv_norm_g, conv_norm_b, w_pw2) * jax.nn.silu(a_z)
    y_b = mla(c_q, c_kv, k_r, mla_q_norm_g, mla_kv_norm_g, w_uq, w_ukv, cos, sin) * jax.nn.silu(b_z)
    y_c = s5_layer(u, ssm_a_re, ssm_a_im, ssm_log_dt, ssm_b_re, ssm_b_im, ssm_c_re, ssm_c_im,
                   ssm_d, w_glu) * jax.nn.silu(c_z)
    y_d = swa(q, k, v, attn_sinks) * jax.nn.silu(d_z)
    ys = jnp.stack([y_a, y_b, y_c, y_d], axis=2)
    branch = jnp.einsum('bsnw,nwd->bsnd', ys, w_branch)
    gates = jax.nn.sigmoid(x @ w_merge + b_merge).reshape(B, S, N_BRANCH, D)
    merged = jnp.einsum('bsnd,bsnd->bsd', gates, branch)
    x = layer_norm(DEEPNORM_ALPHA * x + merged @ w_out, ln_g, ln_b)
    e = (p_i @ w_ple) * jax.nn.sigmoid(x @ w_ple_gate)
    return x + rms_norm(e, ple_norm_g)


def setup_inputs(seed: int = 0) -> dict:
    key = jax.random.key(seed)
    ks = jax.random.split(key, 31)
    L, D, W = DEPTH, D_MODEL, BRANCH_W
    G, P, H = SSM_GROUPS, SSM_STATE, SSM_GROUP
    nrm = lambda k, shape, scale: jax.random.normal(k, shape, jnp.float32) * scale
    n_idx = jnp.arange(P, dtype=jnp.float32)
    return {
        'x': nrm(ks[0], (BATCH, SEQ, D), 1.0),
        'p': nrm(ks[1], (DEPTH, BATCH, SEQ, PLE_DIM), 1.0),
        'w_in': nrm(ks[2], (L, D, IN_WIDTH), D ** -0.5),
        'w_merge': nrm(ks[3], (L, D, N_BRANCH * D), D ** -0.5),
        'b_merge': nrm(ks[4], (L, N_BRANCH * D), 0.01),
        'conv_w': nrm(ks[5], (L, CONV_W, W), CONV_W ** -0.5),
        'conv_b': nrm(ks[6], (L, W), 0.01),
        'conv_norm_g': 1.0 + nrm(ks[7], (L, W), 0.01),
        'conv_norm_b': nrm(ks[8], (L, W), 0.01),
        'w_pw2': nrm(ks[9], (L, W, W), W ** -0.5),
        'mla_q_norm_g': 1.0 + nrm(ks[10], (L, MLA_Q_RANK), 0.01),
        'mla_kv_norm_g': 1.0 + nrm(ks[11], (L, MLA_KV_RANK), 0.01),
        'w_uq': nrm(ks[12], (L, MLA_Q_RANK, MLA_HEADS * (MLA_NOPE + MLA_ROPE)), MLA_Q_RANK ** -0.5),
        'w_ukv': nrm(ks[13], (L, MLA_KV_RANK, MLA_HEADS * (MLA_NOPE + MLA_V)), MLA_KV_RANK ** -0.5),
        'ssm_a_re': -0.5 + nrm(ks[14], (L, G, P), 0.01),
        'ssm_a_im': math.pi * n_idx + nrm(ks[15], (L, G, P), 0.01),
        'ssm_log_dt': jax.random.uniform(ks[16], (L, G), jnp.float32, math.log(DT_MIN), math.log(DT_MAX)),
        'ssm_b_re': nrm(ks[17], (L, G, P, H), (2 * H) ** -0.5),
        'ssm_b_im': nrm(ks[18], (L, G, P, H), (2 * H) ** -0.5),
        'ssm_c_re': nrm(ks[19], (L, G, H, P), P ** -0.5),
        'ssm_c_im': nrm(ks[20], (L, G, H, P), P ** -0.5),
        'ssm_d': nrm(ks[21], (L, W), 1.0),
        'w_glu': nrm(ks[22], (L, W, 2 * W), W ** -0.5),
        'attn_sinks': nrm(ks[23], (L, SWA_HEADS), 0.5),
        'w_branch': nrm(ks[24], (L, N_BRANCH, W, D), DEEPNORM_BETA * W ** -0.5),
        'w_out': nrm(ks[25], (L, D, D), DEEPNORM_BETA * D ** -0.5),
        'ln_g': 1.0 + nrm(ks[26], (L, D), 0.01),
        'ln_b': nrm(ks[27], (L, D), 0.01),
        'w_ple': nrm(ks[28], (L, PLE_DIM, D), PLE_DIM ** -0.5),
        'w_ple_gate': nrm(ks[29], (L, D, D), D ** -0.5),
        'ple_norm_g': 1.0 + nrm(ks[30], (L, D), 0.01),
    }


def reference(x, p, w_in, w_merge, b_merge, conv_w, conv_b, conv_norm_g, conv_norm_b, w_pw2,
              mla_q_norm_g, mla_kv_norm_g, w_uq, w_ukv, ssm_a_re, ssm_a_im, ssm_log_dt,
              ssm_b_re, ssm_b_im, ssm_c_re, ssm_c_im, ssm_d, w_glu, attn_sinks, w_branch, w_out,
              ln_g, ln_b, w_ple, w_ple_gate, ple_norm_g):
    cos, sin = rope_tables(x.shape[1])
    for i in range(DEPTH):
        x = hybrid_layer(x, p[i], cos, sin, w_in[i], w_merge[i], b_merge[i], conv_w[i], conv_b[i],
                         conv_norm_g[i], conv_norm_b[i], w_pw2[i], mla_q_norm_g[i], mla_kv_norm_g[i],
                         w_uq[i], w_ukv[i], ssm_a_re[i], ssm_a_im[i], ssm_log_dt[i], ssm_b_re[i],
                         ssm_b_im[i], ssm_c_re[i], ssm_c_im[i], ssm_d[i], w_glu[i], attn_sinks[i],
                         w_branch[i], w_out[i], ln_g[i], ln_b[i], w_ple[i], w_ple_gate[i], ple_norm_g[i])
    return x
```

```python
import functools
import math

import jax
import jax.numpy as jnp
from jax import lax
from jax.experimental import pallas as pl
from jax.experimental.pallas import tpu as pltpu

D_MODEL = 1024
DEPTH = 4
PLE_DIM = 256
N_BRANCH = 4
BRANCH_W = 256
CONV_W = 31
MLA_HEADS = 4
MLA_NOPE = 64
MLA_ROPE = 32
MLA_V = 64
MLA_Q_RANK = 256
MLA_KV_RANK = 128
ROPE_THETA = 10000.0
SSM_GROUP = 16
SSM_GROUPS = 16
SSM_STATE = 64
SWA_HEADS = 4
SWA_KV_HEADS = 2
SWA_HEAD_DIM = 64
WINDOW = 128
DEEPNORM_ALPHA = (2.0 * DEPTH) ** 0.25
LN_EPS = 1e-5
RMS_EPS = 1e-6

LANES = 128
VMEM_LIMIT_BYTES = 56 * 1024 * 1024

IN_TM = 512
SEQ_BLOCK = 512
SSM_CHUNK = 128
ATTN_BQ = 256
ATTN_BK = 256
MERGE_TM = 256
CONV_HALO = 32
CONV_ROWS = 64

MLA_SLOT = LANES
NSTATE = SSM_GROUPS * SSM_STATE
NEG_BIG = -1e30

F32 = jnp.float32
BF16 = jnp.bfloat16


def _params(*sem):
    return pltpu.CompilerParams(dimension_semantics=sem, vmem_limit_bytes=VMEM_LIMIT_BYTES)


def _resident(shape):
    nd = len(shape)
    return pl.BlockSpec(shape, lambda *_: (0,) * nd)


def _dot(a, b):
    return jnp.dot(a, b, preferred_element_type=F32)


def _dot_nt(a, b):
    return lax.dot_general(a, b, (((1,), (1,)), ((), ())), preferred_element_type=F32)


def _sigmoid(x):
    return 1.0 / (1.0 + jnp.exp(-x))


def _silu(x):
    return x * _sigmoid(x)


IN_GROUPS = (
    ("conv", 3 * BRANCH_W, F32),
    ("mla", MLA_Q_RANK + MLA_KV_RANK + 2 * LANES, F32),
    ("bz", BRANCH_W, F32),
    ("ssm", 2 * BRANCH_W, F32),
    ("swa", 2 * BRANCH_W, BF16),
    ("dz", BRANCH_W, F32),
)
IN_EXT_WIDTH = sum(w for _, w, _ in IN_GROUPS)


def _in_proj_kernel(x_ref, w_ref, *out_refs):
    xb = x_ref[...].astype(BF16)
    start = 0
    for (_, width, dtype), o_ref in zip(IN_GROUPS, out_refs):
        o_ref[...] = _dot(xb, w_ref[:, start:start + width]).astype(dtype)
        start += width


def _in_proj(x2d, w_in_ext):
    t = x2d.shape[0]
    return pl.pallas_call(
        _in_proj_kernel,
        grid=(t // IN_TM,),
        in_specs=[pl.BlockSpec((IN_TM, D_MODEL), lambda i: (i, 0)),
                  _resident((D_MODEL, IN_EXT_WIDTH))],
        out_specs=[pl.BlockSpec((IN_TM, w), lambda i: (i, 0)) for _, w, _ in IN_GROUPS],
        out_shape=[jax.ShapeDtypeStruct((t, w), d) for _, w, d in IN_GROUPS],
        compiler_params=_params("parallel"),
        name="in_proj",
    )(x2d, w_in_ext)


def _conv_kernel(h_ref, cw_ref, cb_ref, ng_ref, nb_ref, pw2_ref, o_ref, hbuf, ybuf):
    i = pl.program_id(1)
    bs = SEQ_BLOCK
    w = BRANCH_W

    @pl.when(i == 0)
    def _():
        hbuf[0:CONV_HALO, :] = jnp.zeros((CONV_HALO, w), F32)

    @pl.when(i > 0)
    def _():
        hbuf[0:CONV_HALO, :] = hbuf[bs:bs + CONV_HALO, :]

    hbuf[CONV_HALO:, :] = h_ref[:, 0:w] * _sigmoid(h_ref[:, w:2 * w])

    base = CONV_HALO - (CONV_W - 1)
    for c in range(bs // CONV_ROWS):
        r0 = c * CONV_ROWS
        acc = jnp.broadcast_to(cb_ref[...], (CONV_ROWS, w))
        for j in range(CONV_W):
            acc = acc + cw_ref[j:j + 1, :] * hbuf[r0 + base + j:r0 + base + j + CONV_ROWS, :]
        mu = jnp.mean(acc, axis=-1, keepdims=True)
        d = acc - mu
        var = jnp.mean(d * d, axis=-1, keepdims=True)
        y = d * lax.rsqrt(var + LN_EPS) * ng_ref[...] + nb_ref[...]
        ybuf[r0:r0 + CONV_ROWS, :] = _silu(y).astype(BF16)
    o_ref[...] = (_dot(ybuf[...], pw2_ref[...]) * _silu(h_ref[:, 2 * w:3 * w])).astype(o_ref.dtype)


def _conv_branch(conv_in, conv_w, conv_b, norm_g, norm_b, w_pw2, batch, seq):
    bs = SEQ_BLOCK
    w = BRANCH_W
    return pl.pallas_call(
        _conv_kernel,
        grid=(batch, seq // bs),
        in_specs=[pl.BlockSpec((None, bs, 3 * w), lambda b, i: (b, i, 0)),
                  _resident((CONV_W, w)), _resident((1, w)), _resident((1, w)), _resident((1, w)),
                  _resident((w, w))],
        out_specs=pl.BlockSpec((None, bs, w), lambda b, i: (b, i, 0)),
        out_shape=jax.ShapeDtypeStruct((batch, seq, w), BF16),
        scratch_shapes=[pltpu.VMEM((bs + CONV_HALO, w), F32), pltpu.VMEM((bs, w), BF16)],
        compiler_params=_params("parallel", "arbitrary"),
        name="conv_branch",
    )(conv_in.reshape(batch, seq, 3 * w), conv_w, conv_b, norm_g, norm_b, w_pw2)


SSM_LEVELS = int(math.log2(SSM_CHUNK))


def _ssm_kernel(in_ref, bmat_ref, cmat_ref, pw_ref, d_ref, wglu_ref, o_ref, carry):
    i = pl.program_id(1)
    n = NSTATE
    w = BRANCH_W
    rows = SSM_CHUNK

    @pl.when(i == 0)
    def _():
        carry[...] = jnp.zeros_like(carry)

    u = in_ref[:, 0:w]
    bu = _dot(u.astype(BF16), bmat_ref[...])
    xr = bu[:, 0:n]
    xi = bu[:, n:2 * n]
    row = lax.broadcasted_iota(jnp.int32, (rows, n), 0)

    cr = carry[0:1, :]
    ci = carry[1:2, :]
    lr = pw_ref[0:1, 0:n]
    li = pw_ref[0:1, n:2 * n]
    first = row == 0
    xr = xr + jnp.where(first, lr * cr - li * ci, 0.0)
    xi = xi + jnp.where(first, lr * ci + li * cr, 0.0)

    for lvl in range(SSM_LEVELS):
        s = 1 << lvl
        ar = pw_ref[lvl:lvl + 1, 0:n]
        ai = pw_ref[lvl:lvl + 1, n:2 * n]
        keep = row >= s
        sr = jnp.where(keep, pltpu.roll(xr, s, 0), 0.0)
        si = jnp.where(keep, pltpu.roll(xi, s, 0), 0.0)
        xr, xi = xr + (ar * sr - ai * si), xi + (ar * si + ai * sr)

    carry[0:1, :] = xr[rows - 1:rows, :]
    carry[1:2, :] = xi[rows - 1:rows, :]

    hcat = jnp.concatenate([xr, xi], axis=1).astype(BF16)
    y = _dot(hcat, cmat_ref[...])
    y = jax.nn.gelu(y + d_ref[...] * u)
    g = _dot(y.astype(BF16), wglu_ref[...])
    out = g[:, 0:w] * _sigmoid(g[:, w:2 * w]) * _silu(in_ref[:, w:2 * w])
    o_ref[...] = out.astype(o_ref.dtype)


def _ssm_branch(ssm_in, bmat, cmat, powers, d_skip, w_glu, batch, seq):
    rows = SSM_CHUNK
    w = BRANCH_W
    return pl.pallas_call(
        _ssm_kernel,
        grid=(batch, seq // rows),
        in_specs=[pl.BlockSpec((None, rows, 2 * w), lambda b, i: (b, i, 0)),
                  _resident((w, 2 * NSTATE)), _resident((2 * NSTATE, w)),
                  _resident((8, 2 * NSTATE)), _resident((1, w)), _resident((w, 2 * w))],
        out_specs=pl.BlockSpec((None, rows, w), lambda b, i: (b, i, 0)),
        out_shape=jax.ShapeDtypeStruct((batch, seq, w), BF16),
        scratch_shapes=[pltpu.VMEM((8, NSTATE), F32)],
        compiler_params=_params("parallel", "arbitrary"),
        name="ssm_branch",
    )(ssm_in.reshape(batch, seq, 2 * w), bmat, cmat, powers, d_skip, w_glu)


SWA_GROUP = SWA_HEADS // SWA_KV_HEADS
SWA_BLOCKS = SEQ_BLOCK // WINDOW


def _swa_kernel(sink_ref, cur_ref, prev_ref, dz_ref, o_ref, kvbuf, ybuf):
    i = pl.program_id(1)
    hd = SWA_HEAD_DIM
    kv_w = SWA_KV_HEADS * hd
    q_w = SWA_HEADS * hd
    win = WINDOW
    scale = hd ** -0.5

    kvbuf[0:win, :] = prev_ref[:, q_w:q_w + 2 * kv_w]
    kvbuf[win:, :] = cur_ref[:, q_w:q_w + 2 * kv_w]

    rows = SWA_GROUP * win
    qrow = lax.broadcasted_iota(jnp.int32, (rows, 2 * win), 0) % win + win
    kcol = lax.broadcasted_iota(jnp.int32, (rows, 2 * win), 1)
    rel = qrow - kcol
    band = (rel >= 0) & (rel < win)
    head_row = lax.broadcasted_iota(jnp.int32, (rows, 1), 0) // win

    for j in range(SWA_BLOCKS):
        first_block = (i * SWA_BLOCKS + j) == 0
        valid = band & (jnp.logical_not(first_block) | (kcol >= win))
        for kvh in range(SWA_KV_HEADS):
            q2 = jnp.concatenate(
                [cur_ref[j * win:(j + 1) * win, (kvh * SWA_GROUP + g) * hd:(kvh * SWA_GROUP + g + 1) * hd]
                 for g in range(SWA_GROUP)], axis=0)
            k2 = kvbuf[j * win:(j + 2) * win, kvh * hd:(kvh + 1) * hd]
            v2 = kvbuf[j * win:(j + 2) * win, kv_w + kvh * hd:kv_w + (kvh + 1) * hd]
            s = jnp.where(valid, _dot_nt(q2, k2) * scale, NEG_BIG)
            sink = jnp.where(head_row == 0, sink_ref[kvh * SWA_GROUP], sink_ref[kvh * SWA_GROUP + 1])
            m = jnp.maximum(jnp.max(s, axis=-1, keepdims=True), sink)
            p = jnp.exp(s - m)
            denom = jnp.sum(p, axis=-1, keepdims=True) + jnp.exp(sink - m)
            o = _dot(p.astype(BF16), v2) / denom
            for g in range(SWA_GROUP):
                h = kvh * SWA_GROUP + g
                ybuf[j * win:(j + 1) * win, h * hd:(h + 1) * hd] = o[g * win:(g + 1) * win, :]
    o_ref[...] = (ybuf[...] * _silu(dz_ref[...])).astype(o_ref.dtype)


def _swa_branch(swa_in, dz, sinks, batch, seq):
    bs = SEQ_BLOCK
    w = BRANCH_W
    per = SWA_BLOCKS
    return pl.pallas_call(
        _swa_kernel,
        grid=(batch, seq // bs),
        in_specs=[pl.BlockSpec(memory_space=pltpu.SMEM),
                  pl.BlockSpec((None, bs, 2 * w), lambda b, i: (b, i, 0)),
                  pl.BlockSpec((None, WINDOW, 2 * w), lambda b, i: (b, jnp.maximum(i * per - 1, 0), 0)),
                  pl.BlockSpec((None, bs, w), lambda b, i: (b, i, 0))],
        out_specs=pl.BlockSpec((None, bs, w), lambda b, i: (b, i, 0)),
        out_shape=jax.ShapeDtypeStruct((batch, seq, w), BF16),
        scratch_shapes=[pltpu.VMEM((bs + WINDOW, 2 * SWA_KV_HEADS * SWA_HEAD_DIM), BF16),
                        pltpu.VMEM((bs, w), F32)],
        compiler_params=_params("parallel", "arbitrary"),
        name="swa_branch",
    )(sinks, swa_in.reshape(batch, seq, 2 * w), swa_in.reshape(batch, seq, 2 * w), dz.reshape(batch, seq, w))


MLA_QK = MLA_HEADS * MLA_SLOT


def _rms(x, g):
    ms = jnp.mean(x * x, axis=-1, keepdims=True)
    return x * lax.rsqrt(ms + RMS_EPS) * g


def _mla_prep_kernel(in_ref, cos_ref, sin_ref, qg_ref, kvg_ref, wuq_ref, wukv_ref, q_ref, k_ref, v_ref):
    scale = (MLA_NOPE + MLA_ROPE) ** -0.5
    cq = _rms(in_ref[:, 0:MLA_Q_RANK], qg_ref[...]).astype(BF16)
    ckv = _rms(in_ref[:, MLA_Q_RANK:MLA_Q_RANK + MLA_KV_RANK], kvg_ref[...]).astype(BF16)
    off = MLA_Q_RANK + MLA_KV_RANK
    cos = cos_ref[...]
    sin = sin_ref[...]
    k_rope = in_ref[:, off:off + LANES] * cos + in_ref[:, off + LANES:off + 2 * LANES] * sin
    kv = _dot(ckv, wukv_ref[...])
    for h in range(MLA_HEADS):
        lo = h * MLA_SLOT
        qm = _dot(cq, wuq_ref[:, lo:lo + MLA_SLOT])
        qr = _dot(cq, wuq_ref[:, MLA_QK + lo:MLA_QK + lo + MLA_SLOT])
        q_ref[:, lo:lo + MLA_SLOT] = ((qm * cos + qr * sin) * scale).astype(BF16)
        k_ref[:, lo:lo + MLA_SLOT] = (kv[:, lo:lo + MLA_SLOT] + k_rope).astype(BF16)
    v_ref[...] = kv[:, MLA_QK:].astype(BF16)


def _mla_prep(mla_in, cos_t, sin_t, q_norm_g, kv_norm_g, w_uq_ext, w_ukv_ext, seq):
    t = mla_in.shape[0]
    tm = IN_TM
    nsb = seq // tm
    vw = MLA_HEADS * MLA_V
    return pl.pallas_call(
        _mla_prep_kernel,
        grid=(t // tm,),
        in_specs=[pl.BlockSpec((tm, mla_in.shape[1]), lambda i: (i, 0)),
                  pl.BlockSpec((tm, LANES), lambda i: (i % nsb, 0)),
                  pl.BlockSpec((tm, LANES), lambda i: (i % nsb, 0)),
                  _resident((1, MLA_Q_RANK)), _resident((1, MLA_KV_RANK)),
                  _resident((MLA_Q_RANK, 2 * MLA_QK)), _resident((MLA_KV_RANK, MLA_QK + vw))],
        out_specs=[pl.BlockSpec((tm, MLA_QK), lambda i: (i, 0)),
                   pl.BlockSpec((tm, MLA_QK), lambda i: (i, 0)),
                   pl.BlockSpec((tm, vw), lambda i: (i, 0))],
        out_shape=[jax.ShapeDtypeStruct((t, MLA_QK), BF16), jax.ShapeDtypeStruct((t, MLA_QK), BF16),
                   jax.ShapeDtypeStruct((t, vw), BF16)],
        compiler_params=_params("parallel"),
        name="mla_prep",
    )(mla_in, cos_t, sin_t, q_norm_g, kv_norm_g, w_uq_ext, w_ukv_ext)


def _attn_kernel(q_ref, k_ref, v_ref, bz_ref, o_ref, m_sc, l_sc, acc_sc):
    i = pl.program_id(1)
    bq, bk = ATTN_BQ, ATTN_BK
    row = lax.broadcasted_iota(jnp.int32, (bq, bk), 0)
    col = lax.broadcasted_iota(jnp.int32, (bq, bk), 1)
    causal = col <= row
    outs = []
    for h in range(MLA_HEADS):
        qh = q_ref[:, h * MLA_SLOT:(h + 1) * MLA_SLOT]
        vlo = (h // 2) * LANES
        m_sc[...] = jnp.full((bq, 1), NEG_BIG, F32)
        l_sc[...] = jnp.zeros((bq, 1), F32)
        acc_sc[...] = jnp.zeros((bq, LANES), F32)

        def step(j, masked):
            start = pl.multiple_of(j * bk, bk)
            kb = k_ref[pl.ds(start, bk), h * MLA_SLOT:(h + 1) * MLA_SLOT]
            vb = v_ref[pl.ds(start, bk), vlo:vlo + LANES]
            s = _dot_nt(qh, kb)
            if masked:
                s = jnp.where(causal, s, NEG_BIG)
            m_prev = m_sc[...]
            m_new = jnp.maximum(m_prev, jnp.max(s, axis=-1, keepdims=True))
            alpha = jnp.exp(m_prev - m_new)
            p = jnp.exp(s - m_new)
            l_sc[...] = alpha * l_sc[...] + jnp.sum(p, axis=-1, keepdims=True)
            acc_sc[...] = alpha * acc_sc[...] + _dot(p.astype(BF16), vb)
            m_sc[...] = m_new

        def body(j, c):
            step(j, False)
            return c

        lax.fori_loop(0, i, body, 0)
        step(i, True)
        outs.append(acc_sc[...] / l_sc[...])
    lane = lax.broadcasted_iota(jnp.int32, (bq, LANES), 1)
    low = lane < MLA_V
    y = jnp.concatenate([jnp.where(low, outs[0], outs[1]), jnp.where(low, outs[2], outs[3])], axis=1)
    o_ref[...] = (y * _silu(bz_ref[...])).astype(o_ref.dtype)


def _mla_attention(q, k, v, bz, batch, seq):
    assert ATTN_BQ == ATTN_BK
    bq = ATTN_BQ
    vw = MLA_HEADS * MLA_V
    return pl.pallas_call(
        _attn_kernel,
        grid=(batch, seq // bq),
        in_specs=[pl.BlockSpec((None, bq, MLA_QK), lambda b, i: (b, i, 0)),
                  pl.BlockSpec((None, seq, MLA_QK), lambda b, i: (b, 0, 0)),
                  pl.BlockSpec((None, seq, vw), lambda b, i: (b, 0, 0)),
                  pl.BlockSpec((None, bq, BRANCH_W), lambda b, i: (b, i, 0))],
        out_specs=pl.BlockSpec((None, bq, BRANCH_W), lambda b, i: (b, i, 0)),
        out_shape=jax.ShapeDtypeStruct((batch, seq, BRANCH_W), BF16),
        scratch_shapes=[pltpu.VMEM((bq, 1), F32), pltpu.VMEM((bq, 1), F32), pltpu.VMEM((bq, LANES), F32)],
        compiler_params=_params("parallel", "arbitrary"),
        name="mla_attention",
    )(q.reshape(batch, seq, MLA_QK), k.reshape(batch, seq, MLA_QK), v.reshape(batch, seq, vw),
      bz.reshape(batch, seq, BRANCH_W))


def _merge_kernel(x_ref, ya_ref, yb_ref, yc_ref, yd_ref, p_ref, wm_ref, bm_ref, wb_ref, wo_ref,
                  lng_ref, lnb_ref, wp_ref, wpg_ref, pg_ref, o_ref):
    d = D_MODEL
    x = x_ref[...]
    xb = x.astype(BF16)
    merged = None
    for n, y_ref in enumerate((ya_ref, yb_ref, yc_ref, yd_ref)):
        gate = _sigmoid(_dot(xb, wm_ref[:, n * d:(n + 1) * d]) + bm_ref[:, n * d:(n + 1) * d])
        term = gate * _dot(y_ref[...], wb_ref[n])
        merged = term if merged is None else merged + term
    z = DEEPNORM_ALPHA * x + _dot(merged.astype(BF16), wo_ref[...])
    mu = jnp.mean(z, axis=-1, keepdims=True)
    zc = z - mu
    var = jnp.mean(zc * zc, axis=-1, keepdims=True)
    xn = zc * lax.rsqrt(var + LN_EPS) * lng_ref[...] + lnb_ref[...]
    e = _dot(p_ref[...].astype(BF16), wp_ref[...]) * _sigmoid(_dot(xn.astype(BF16), wpg_ref[...]))
    o_ref[...] = xn + _rms(e, pg_ref[...])


def _merge(x2d, ys, p2d, w_merge, b_merge, w_branch, w_out, ln_g, ln_b, w_ple, w_ple_gate, ple_norm_g):
    t = x2d.shape[0]
    tm = MERGE_TM
    d = D_MODEL
    row_spec = lambda width: pl.BlockSpec((tm, width), lambda i: (i, 0))
    return pl.pallas_call(
        _merge_kernel,
        grid=(t // tm,),
        in_specs=[row_spec(d)] + [row_spec(BRANCH_W)] * N_BRANCH + [row_spec(PLE_DIM),
                  _resident((d, N_BRANCH * d)), _resident((1, N_BRANCH * d)),
                  _resident((N_BRANCH, BRANCH_W, d)), _resident((d, d)),
                  _resident((1, d)), _resident((1, d)), _resident((PLE_DIM, d)), _resident((d, d)),
                  _resident((1, d))],
        out_specs=row_spec(d),
        out_shape=jax.ShapeDtypeStruct((t, d), F32),
        compiler_params=_params("parallel"),
        name="merge",
    )(x2d, *ys, p2d, w_merge, b_merge, w_branch, w_out, ln_g, ln_b, w_ple, w_ple_gate, ple_norm_g)


def _rot_cols(w):
    half = w.shape[1] // 2
    return jnp.concatenate([-w[:, half:], w[:, :half]], axis=1)


def _rope_slot(w):
    z = jnp.zeros((w.shape[0], MLA_NOPE), w.dtype)
    z2 = jnp.zeros((w.shape[0], MLA_SLOT - MLA_NOPE - MLA_ROPE), w.dtype)
    return jnp.concatenate([z, w, z2], axis=1)


def _in_weights(w_in):
    o = [0]
    for s in (256, 256, 256, 256, 128, 32, 256, 256, 256, 256, 128, 128, 256):
        o.append(o[-1] + s)
    col = lambda k: w_in[:, o[k]:o[k + 1]]
    a_val, a_gate, a_z, c_q, c_kv, k_r, b_z, u, c_z, q, k, v, d_z = (col(k) for k in range(13))
    groups = [a_val, a_gate, a_z, c_q, c_kv, _rope_slot(k_r), _rope_slot(_rot_cols(k_r)), b_z, u, c_z, q, k, v, d_z]
    return jnp.concatenate(groups, axis=1).astype(BF16)


def _uq_weights(w_uq):
    hd = MLA_NOPE + MLA_ROPE
    main, rot = [], []
    for h in range(MLA_HEADS):
        nope = w_uq[:, h * hd:h * hd + MLA_NOPE]
        rope_w = w_uq[:, h * hd + MLA_NOPE:(h + 1) * hd]
        pad = jnp.zeros((w_uq.shape[0], MLA_SLOT - hd), w_uq.dtype)
        main.append(jnp.concatenate([nope, rope_w, pad], axis=1))
        rot.append(_rope_slot(_rot_cols(rope_w)))
    return jnp.concatenate(main + rot, axis=1).astype(BF16)


def _ukv_weights(w_ukv):
    hd = MLA_NOPE + MLA_V
    ks, vs = [], []
    for h in range(MLA_HEADS):
        k_nope = w_ukv[:, h * hd:h * hd + MLA_NOPE]
        ks.append(jnp.concatenate([k_nope, jnp.zeros((w_ukv.shape[0], MLA_SLOT - MLA_NOPE), w_ukv.dtype)], axis=1))
        vs.append(w_ukv[:, h * hd + MLA_NOPE:(h + 1) * hd])
    return jnp.concatenate(ks + vs, axis=1).astype(BF16)


def _rope_tables(seq):
    pos = jnp.arange(seq, dtype=F32)
    inv_freq = ROPE_THETA ** (-jnp.arange(0, MLA_ROPE, 2, dtype=F32) / MLA_ROPE)
    ang = pos[:, None] * inv_freq[None, :]
    cos, sin = jnp.cos(ang), jnp.sin(ang)
    ones = jnp.ones((seq, MLA_NOPE), F32)
    zeros = jnp.zeros((seq, MLA_NOPE), F32)
    pad = jnp.zeros((seq, MLA_SLOT - MLA_NOPE - MLA_ROPE), F32)
    cos_t = jnp.concatenate([ones, cos, cos, pad], axis=1)
    sin_t = jnp.concatenate([zeros, sin, sin, pad], axis=1)
    return cos_t, sin_t


def _ssm_weights(a_re, a_im, log_dt, b_re, b_im, c_re, c_im):
    g, p, h = SSM_GROUPS, SSM_STATE, SSM_GROUP
    dt = jnp.exp(log_dt.astype(F32))[:, None]
    lr, li = a_re.astype(F32), a_im.astype(F32)
    mag = jnp.exp(lr * dt)
    lb_re, lb_im = mag * jnp.cos(li * dt), mag * jnp.sin(li * dt)
    den = lr * lr + li * li
    nr, ni = lb_re - 1.0, lb_im
    f_re = ((nr * lr + ni * li) / den)[..., None]
    f_im = ((ni * lr - nr * li) / den)[..., None]
    bb_re = f_re * b_re - f_im * b_im
    bb_im = f_re * b_im + f_im * b_re
    eye = jnp.eye(g, dtype=F32)
    bd = lambda m: jnp.einsum('gph,gk->ghkp', m, eye).reshape(g * h, g * p)
    bmat = jnp.concatenate([bd(bb_re), bd(bb_im)], axis=1).astype(BF16)
    cd = lambda m: jnp.einsum('ghp,gk->gpkh', m, eye).reshape(g * p, g * h)
    cmat = jnp.concatenate([cd(c_re.astype(F32)), -cd(c_im.astype(F32))], axis=0).astype(BF16)
    pr, pi = lb_re.reshape(1, g * p), lb_im.reshape(1, g * p)
    rows = []
    for _ in range(8):
        rows.append(jnp.concatenate([pr, pi], axis=1))
        pr, pi = pr * pr - pi * pi, 2.0 * pr * pi
    return bmat, cmat, jnp.concatenate(rows, axis=0)


def kernel(x, p, w_in, w_merge, b_merge, conv_w, conv_b, conv_norm_g, conv_norm_b, w_pw2, mla_q_norm_g,
           mla_kv_norm_g, w_uq, w_ukv, ssm_a_re, ssm_a_im, ssm_log_dt, ssm_b_re, ssm_b_im, ssm_c_re,
           ssm_c_im, ssm_d, w_glu, attn_sinks, w_branch, w_out, ln_g, ln_b, w_ple, w_ple_gate, ple_norm_g):
    batch, seq, d = x.shape
    t = batch * seq
    cos_t, sin_t = _rope_tables(seq)
    x2d = x.reshape(t, d)
    row = lambda a: a.reshape(1, -1).astype(F32)
    for l in range(DEPTH):
        conv_in, mla_in, bz, ssm_in, swa_in, dz = _in_proj(x2d, _in_weights(w_in[l]))
        y_a = _conv_branch(conv_in, conv_w[l], row(conv_b[l]), row(conv_norm_g[l]), row(conv_norm_b[l]),
                           w_pw2[l].astype(BF16), batch, seq)
        q, k, v = _mla_prep(mla_in, cos_t, sin_t, row(mla_q_norm_g[l]), row(mla_kv_norm_g[l]),
                            _uq_weights(w_uq[l]), _ukv_weights(w_ukv[l]), seq)
        y_b = _mla_attention(q, k, v, bz, batch, seq)
        bmat, cmat, powers = _ssm_weights(ssm_a_re[l], ssm_a_im[l], ssm_log_dt[l], ssm_b_re[l], ssm_b_im[l],
                                          ssm_c_re[l], ssm_c_im[l])
        y_c = _ssm_branch(ssm_in, bmat, cmat, powers, row(ssm_d[l]), w_glu[l].astype(BF16), batch, seq)
        y_d = _swa_branch(swa_in, dz, attn_sinks[l].astype(F32), batch, seq)
        ys = [y.reshape(t, BRANCH_W) for y in (y_a, y_b, y_c, y_d)]
        x2d = _merge(x2d, ys, p[l].reshape(t, PLE_DIM), w_merge[l].astype(BF16), row(b_merge[l]),
                     w_branch[l].astype(BF16), w_out[l].astype(BF16), row(ln_g[l]), row(ln_b[l]),
                     w_ple[l].astype(BF16), w_ple_gate[l].astype(BF16), row(ple_norm_g[l]))
    return x2d.reshape(batch, seq, d)
```

```python
import functools
import math

import jax
import jax.numpy as jnp
from jax import lax
from jax.experimental import pallas as pl
from jax.experimental.pallas import tpu as pltpu

D_MODEL = 1024
DEPTH = 4
PLE_DIM = 256
N_BRANCH = 4
BRANCH_W = 256
CONV_W = 31
MLA_HEADS = 4
MLA_NOPE = 64
MLA_ROPE = 32
MLA_V = 64
MLA_Q_RANK = 256
MLA_KV_RANK = 128
ROPE_THETA = 10000.0
SSM_GROUP = 16
SSM_GROUPS = 16
SSM_STATE = 64
SWA_HEADS = 4
SWA_KV_HEADS = 2
SWA_HEAD_DIM = 64
WINDOW = 128
DEEPNORM_ALPHA = (2.0 * DEPTH) ** 0.25
LN_EPS = 1e-5
RMS_EPS = 1e-6

LANES = 128
VMEM_LIMIT_BYTES = 56 * 1024 * 1024

IN_TM = 512
SEQ_BLOCK = 512
SSM_CHUNK = 128
ATTN_BLK = 512
MERGE_TM = 256
CONV_HALO = 32
CONV_ROWS = 64

MLA_SLOT = LANES
MLA_VT_ROWS = MLA_V + 16
LOG2E = math.log2(math.e)
NSTATE = SSM_GROUPS * SSM_STATE
NEG_BIG = -1e30

F32 = jnp.float32
BF16 = jnp.bfloat16


def _params(*sem):
    return pltpu.CompilerParams(dimension_semantics=sem, vmem_limit_bytes=VMEM_LIMIT_BYTES)


def _resident(shape):
    nd = len(shape)
    return pl.BlockSpec(shape, lambda *_: (0,) * nd)


def _dot(a, b):
    return jnp.dot(a, b, preferred_element_type=F32)


def _dot_nt(a, b):
    return lax.dot_general(a, b, (((1,), (1,)), ((), ())), preferred_element_type=F32)


def _sigmoid(x):
    return 1.0 / (1.0 + jnp.exp(-x))


def _silu(x):
    return x * _sigmoid(x)


IN_GROUPS = (
    ("conv", 3 * BRANCH_W, F32),
    ("mla", MLA_Q_RANK + MLA_KV_RANK + 2 * LANES, F32),
    ("bz", BRANCH_W, F32),
    ("ssm", 2 * BRANCH_W, F32),
    ("swa", 2 * BRANCH_W, BF16),
    ("dz", BRANCH_W, F32),
)
IN_EXT_WIDTH = sum(w for _, w, _ in IN_GROUPS)


def _in_proj_kernel(x_ref, w_ref, *out_refs):
    xb = x_ref[...].astype(BF16)
    start = 0
    for (_, width, dtype), o_ref in zip(IN_GROUPS, out_refs):
        o_ref[...] = _dot(xb, w_ref[:, start:start + width]).astype(dtype)
        start += width


def _in_proj(x2d, w_in_ext):
    t = x2d.shape[0]
    return pl.pallas_call(
        _in_proj_kernel,
        grid=(t // IN_TM,),
        in_specs=[pl.BlockSpec((IN_TM, D_MODEL), lambda i: (i, 0)),
                  _resident((D_MODEL, IN_EXT_WIDTH))],
        out_specs=[pl.BlockSpec((IN_TM, w), lambda i: (i, 0)) for _, w, _ in IN_GROUPS],
        out_shape=[jax.ShapeDtypeStruct((t, w), d) for _, w, d in IN_GROUPS],
        compiler_params=_params("parallel"),
        name="in_proj",
    )(x2d, w_in_ext)


def _conv_kernel(h_ref, cw_ref, cb_ref, ng_ref, nb_ref, pw2_ref, o_ref, hbuf, ybuf):
    i = pl.program_id(1)
    bs = SEQ_BLOCK
    w = BRANCH_W

    @pl.when(i == 0)
    def _():
        hbuf[0:CONV_HALO, :] = jnp.zeros((CONV_HALO, w), F32)

    @pl.when(i > 0)
    def _():
        hbuf[0:CONV_HALO, :] = hbuf[bs:bs + CONV_HALO, :]

    hbuf[CONV_HALO:, :] = h_ref[:, 0:w] * _sigmoid(h_ref[:, w:2 * w])

    base = CONV_HALO - (CONV_W - 1)
    for c in range(bs // CONV_ROWS):
        r0 = c * CONV_ROWS
        acc = jnp.broadcast_to(cb_ref[...], (CONV_ROWS, w))
        for j in range(CONV_W):
            acc = acc + cw_ref[j:j + 1, :] * hbuf[r0 + base + j:r0 + base + j + CONV_ROWS, :]
        mu = jnp.mean(acc, axis=-1, keepdims=True)
        d = acc - mu
        var = jnp.mean(d * d, axis=-1, keepdims=True)
        y = d * lax.rsqrt(var + LN_EPS) * ng_ref[...] + nb_ref[...]
        ybuf[r0:r0 + CONV_ROWS, :] = _silu(y).astype(BF16)
    o_ref[...] = (_dot(ybuf[...], pw2_ref[...]) * _silu(h_ref[:, 2 * w:3 * w])).astype(o_ref.dtype)


def _conv_branch(conv_in, conv_w, conv_b, norm_g, norm_b, w_pw2, batch, seq):
    bs = SEQ_BLOCK
    w = BRANCH_W
    return pl.pallas_call(
        _conv_kernel,
        grid=(batch, seq // bs),
        in_specs=[pl.BlockSpec((None, bs, 3 * w), lambda b, i: (b, i, 0)),
                  _resident((CONV_W, w)), _resident((1, w)), _resident((1, w)), _resident((1, w)),
                  _resident((w, w))],
        out_specs=pl.BlockSpec((None, bs, w), lambda b, i: (b, i, 0)),
        out_shape=jax.ShapeDtypeStruct((batch, seq, w), BF16),
        scratch_shapes=[pltpu.VMEM((bs + CONV_HALO, w), F32), pltpu.VMEM((bs, w), BF16)],
        compiler_params=_params("parallel", "arbitrary"),
        name="conv_branch",
    )(conv_in.reshape(batch, seq, 3 * w), conv_w, conv_b, norm_g, norm_b, w_pw2)


SSM_LEVELS = int(math.log2(SSM_CHUNK))


def _ssm_kernel(in_ref, bmat_ref, cmat_ref, pw_ref, d_ref, wglu_ref, o_ref, carry):
    i = pl.program_id(1)
    n = NSTATE
    w = BRANCH_W
    rows = SSM_CHUNK

    @pl.when(i == 0)
    def _():
        carry[...] = jnp.zeros_like(carry)

    u = in_ref[:, 0:w]
    bu = _dot(u.astype(BF16), bmat_ref[...])
    xr = bu[:, 0:n]
    xi = bu[:, n:2 * n]
    row = lax.broadcasted_iota(jnp.int32, (rows, n), 0)

    cr = carry[0:1, :]
    ci = carry[1:2, :]
    lr = pw_ref[0:1, 0:n]
    li = pw_ref[0:1, n:2 * n]
    first = row == 0
    xr = xr + jnp.where(first, lr * cr - li * ci, 0.0)
    xi = xi + jnp.where(first, lr * ci + li * cr, 0.0)

    for lvl in range(SSM_LEVELS):
        s = 1 << lvl
        ar = pw_ref[lvl:lvl + 1, 0:n]
        ai = pw_ref[lvl:lvl + 1, n:2 * n]
        keep = row >= s
        sr = jnp.where(keep, pltpu.roll(xr, s, 0), 0.0)
        si = jnp.where(keep, pltpu.roll(xi, s, 0), 0.0)
        xr, xi = xr + (ar * sr - ai * si), xi + (ar * si + ai * sr)

    carry[0:1, :] = xr[rows - 1:rows, :]
    carry[1:2, :] = xi[rows - 1:rows, :]

    hcat = jnp.concatenate([xr, xi], axis=1).astype(BF16)
    y = _dot(hcat, cmat_ref[...])
    y = jax.nn.gelu(y + d_ref[...] * u)
    g = _dot(y.astype(BF16), wglu_ref[...])
    out = g[:, 0:w] * _sigmoid(g[:, w:2 * w]) * _silu(in_ref[:, w:2 * w])
    o_ref[...] = out.astype(o_ref.dtype)


def _ssm_branch(ssm_in, bmat, cmat, powers, d_skip, w_glu, batch, seq):
    rows = SSM_CHUNK
    w = BRANCH_W
    return pl.pallas_call(
        _ssm_kernel,
        grid=(batch, seq // rows),
        in_specs=[pl.BlockSpec((None, rows, 2 * w), lambda b, i: (b, i, 0)),
                  _resident((w, 2 * NSTATE)), _resident((2 * NSTATE, w)),
                  _resident((8, 2 * NSTATE)), _resident((1, w)), _resident((w, 2 * w))],
        out_specs=pl.BlockSpec((None, rows, w), lambda b, i: (b, i, 0)),
        out_shape=jax.ShapeDtypeStruct((batch, seq, w), BF16),
        scratch_shapes=[pltpu.VMEM((8, NSTATE), F32)],
        compiler_params=_params("parallel", "arbitrary"),
        name="ssm_branch",
    )(ssm_in.reshape(batch, seq, 2 * w), bmat, cmat, powers, d_skip, w_glu)


SWA_GROUP = SWA_HEADS // SWA_KV_HEADS
SWA_BLOCKS = SEQ_BLOCK // WINDOW


def _swa_kernel(sink_ref, cur_ref, prev_ref, dz_ref, o_ref, kvbuf, ybuf):
    i = pl.program_id(1)
    hd = SWA_HEAD_DIM
    kv_w = SWA_KV_HEADS * hd
    q_w = SWA_HEADS * hd
    win = WINDOW
    scale = hd ** -0.5

    kvbuf[0:win, :] = prev_ref[:, q_w:q_w + 2 * kv_w]
    kvbuf[win:, :] = cur_ref[:, q_w:q_w + 2 * kv_w]

    rows = SWA_GROUP * win
    qrow = lax.broadcasted_iota(jnp.int32, (rows, 2 * win), 0) % win + win
    kcol = lax.broadcasted_iota(jnp.int32, (rows, 2 * win), 1)
    rel = qrow - kcol
    band = (rel >= 0) & (rel < win)
    head_row = lax.broadcasted_iota(jnp.int32, (rows, 1), 0) // win

    for j in range(SWA_BLOCKS):
        first_block = (i * SWA_BLOCKS + j) == 0
        valid = band & (jnp.logical_not(first_block) | (kcol >= win))
        for kvh in range(SWA_KV_HEADS):
            q2 = jnp.concatenate(
                [cur_ref[j * win:(j + 1) * win, (kvh * SWA_GROUP + g) * hd:(kvh * SWA_GROUP + g + 1) * hd]
                 for g in range(SWA_GROUP)], axis=0)
            k2 = kvbuf[j * win:(j + 2) * win, kvh * hd:(kvh + 1) * hd]
            v2 = kvbuf[j * win:(j + 2) * win, kv_w + kvh * hd:kv_w + (kvh + 1) * hd]
            s = jnp.where(valid, _dot_nt(q2, k2) * scale, NEG_BIG)
            sink = jnp.where(head_row == 0, sink_ref[kvh * SWA_GROUP], sink_ref[kvh * SWA_GROUP + 1])
            m = jnp.maximum(jnp.max(s, axis=-1, keepdims=True), sink)
            p = jnp.exp(s - m)
            denom = jnp.sum(p, axis=-1, keepdims=True) + jnp.exp(sink - m)
            o = _dot(p.astype(BF16), v2) / denom
            for g in range(SWA_GROUP):
                h = kvh * SWA_GROUP + g
                ybuf[j * win:(j + 1) * win, h * hd:(h + 1) * hd] = o[g * win:(g + 1) * win, :]
    o_ref[...] = (ybuf[...] * _silu(dz_ref[...])).astype(o_ref.dtype)


def _swa_branch(swa_in, dz, sinks, batch, seq):
    bs = SEQ_BLOCK
    w = BRANCH_W
    per = SWA_BLOCKS
    return pl.pallas_call(
        _swa_kernel,
        grid=(batch, seq // bs),
        in_specs=[pl.BlockSpec(memory_space=pltpu.SMEM),
                  pl.BlockSpec((None, bs, 2 * w), lambda b, i: (b, i, 0)),
                  pl.BlockSpec((None, WINDOW, 2 * w), lambda b, i: (b, jnp.maximum(i * per - 1, 0), 0)),
                  pl.BlockSpec((None, bs, w), lambda b, i: (b, i, 0))],
        out_specs=pl.BlockSpec((None, bs, w), lambda b, i: (b, i, 0)),
        out_shape=jax.ShapeDtypeStruct((batch, seq, w), BF16),
        scratch_shapes=[pltpu.VMEM((bs + WINDOW, 2 * SWA_KV_HEADS * SWA_HEAD_DIM), BF16),
                        pltpu.VMEM((bs, w), F32)],
        compiler_params=_params("parallel", "arbitrary"),
        name="swa_branch",
    )(sinks, swa_in.reshape(batch, seq, 2 * w), swa_in.reshape(batch, seq, 2 * w), dz.reshape(batch, seq, w))


MLA_QK = MLA_HEADS * MLA_SLOT


def _rms(x, g):
    ms = jnp.mean(x * x, axis=-1, keepdims=True)
    return x * lax.rsqrt(ms + RMS_EPS) * g


def _mla_prep_kernel(in_ref, cos_ref, sin_ref, qg_ref, kvg_ref, wuq_ref, wukv_ref, qt_ref, k_ref, vt_ref):
    scale = (MLA_NOPE + MLA_ROPE) ** -0.5 * LOG2E
    cq = _rms(in_ref[:, 0:MLA_Q_RANK], qg_ref[...]).astype(BF16)
    ckv = _rms(in_ref[:, MLA_Q_RANK:MLA_Q_RANK + MLA_KV_RANK], kvg_ref[...]).astype(BF16)
    off = MLA_Q_RANK + MLA_KV_RANK
    cos = cos_ref[...]
    sin = sin_ref[...]
    k_rope = in_ref[:, off:off + LANES] * cos + in_ref[:, off + LANES:off + 2 * LANES] * sin
    kv = _dot(ckv, wukv_ref[...])
    for h in range(MLA_HEADS):
        lo = h * MLA_SLOT
        qm = _dot(cq, wuq_ref[:, lo:lo + MLA_SLOT])
        qr = _dot(cq, wuq_ref[:, MLA_QK + lo:MLA_QK + lo + MLA_SLOT])
        qt_ref[lo:lo + MLA_SLOT, :] = ((qm * cos + qr * sin) * scale).T.astype(BF16)
        k_ref[:, lo:lo + MLA_SLOT] = (kv[:, lo:lo + MLA_SLOT] + k_rope).astype(BF16)
    vt = kv[:, MLA_QK:].T.astype(BF16)
    ones = jnp.ones((MLA_VT_ROWS - MLA_V, vt.shape[1]), BF16)
    for h in range(MLA_HEADS):
        vt_ref[h * MLA_VT_ROWS:h * MLA_VT_ROWS + MLA_V, :] = vt[h * MLA_V:(h + 1) * MLA_V, :]
        vt_ref[h * MLA_VT_ROWS + MLA_V:(h + 1) * MLA_VT_ROWS, :] = ones


def _mla_prep(mla_in, cos_t, sin_t, q_norm_g, kv_norm_g, w_uq_ext, w_ukv_ext, seq):
    t = mla_in.shape[0]
    tm = ATTN_BLK
    nsb = seq // tm
    vw = MLA_HEADS * MLA_V
    vt_rows = MLA_HEADS * MLA_VT_ROWS
    return pl.pallas_call(
        _mla_prep_kernel,
        grid=(t // tm,),
        in_specs=[pl.BlockSpec((tm, mla_in.shape[1]), lambda i: (i, 0)),
                  pl.BlockSpec((tm, LANES), lambda i: (i % nsb, 0)),
                  pl.BlockSpec((tm, LANES), lambda i: (i % nsb, 0)),
                  _resident((1, MLA_Q_RANK)), _resident((1, MLA_KV_RANK)),
                  _resident((MLA_Q_RANK, 2 * MLA_QK)), _resident((MLA_KV_RANK, MLA_QK + vw))],
        out_specs=[pl.BlockSpec((None, MLA_QK, tm), lambda i: (i, 0, 0)),
                   pl.BlockSpec((tm, MLA_QK), lambda i: (i, 0)),
                   pl.BlockSpec((None, vt_rows, tm), lambda i: (i, 0, 0))],
        out_shape=[jax.ShapeDtypeStruct((t // tm, MLA_QK, tm), BF16), jax.ShapeDtypeStruct((t, MLA_QK), BF16),
                   jax.ShapeDtypeStruct((t // tm, vt_rows, tm), BF16)],
        compiler_params=_params("parallel"),
        name="mla_prep",
    )(mla_in, cos_t, sin_t, q_norm_g, kv_norm_g, w_uq_ext, w_ukv_ext)


def _attn_kernel(qt_ref, k_ref, vt_ref, bz_ref, o_ref, m_sc, acc_sc):
    i = pl.program_id(1)
    blk = ATTN_BLK
    krow = lax.broadcasted_iota(jnp.int32, (blk, blk), 0)
    qcol = lax.broadcasted_iota(jnp.int32, (blk, blk), 1)
    causal = krow <= qcol
    m_sc[...] = jnp.full(m_sc.shape, NEG_BIG, F32)
    acc_sc[...] = jnp.zeros(acc_sc.shape, F32)

    def step(j, masked):
        start = pl.multiple_of(j * blk, blk)
        scores = []
        for h in range(MLA_HEADS):
            kb = k_ref[pl.ds(start, blk), h * MLA_SLOT:(h + 1) * MLA_SLOT]
            scores.append(_dot(kb, qt_ref[h * MLA_SLOT:(h + 1) * MLA_SLOT, :]))
        for h in range(MLA_HEADS):
            s = scores[h]
            if masked:
                s = jnp.where(causal, s, NEG_BIG)
            m_prev = m_sc[h]
            m_new = jnp.maximum(m_prev, jnp.max(s, axis=0, keepdims=True))
            alpha = jnp.exp2(m_prev - m_new)
            p = jnp.exp2(s - m_new).astype(BF16)
            vb = vt_ref[j, h * MLA_VT_ROWS:(h + 1) * MLA_VT_ROWS, :]
            acc_sc[h] = alpha * acc_sc[h] + _dot(vb, p)
            m_sc[h] = m_new

    def body(j, c):
        step(j, False)
        return c

    lax.fori_loop(0, i, body, 0)
    step(i, True)
    outs = [acc_sc[h, 0:MLA_V, :] * (1.0 / acc_sc[h, MLA_V:MLA_V + 1, :]) for h in range(MLA_HEADS)]
    y = jnp.concatenate(outs, axis=0).T
    o_ref[...] = (y * _silu(bz_ref[...])).astype(o_ref.dtype)


def _mla_attention(qt, k, vt, bz, batch, seq):
    blk = ATTN_BLK
    nq = seq // blk
    vt_rows = MLA_HEADS * MLA_VT_ROWS
    return pl.pallas_call(
        _attn_kernel,
        grid=(batch, nq),
        in_specs=[pl.BlockSpec((None, MLA_QK, blk), lambda b, i: (b * nq + i, 0, 0)),
                  pl.BlockSpec((None, seq, MLA_QK), lambda b, i: (b, 0, 0)),
                  pl.BlockSpec((None, nq, vt_rows, blk), lambda b, i: (b, 0, 0, 0)),
                  pl.BlockSpec((None, blk, BRANCH_W), lambda b, i: (b, i, 0))],
        out_specs=pl.BlockSpec((None, blk, BRANCH_W), lambda b, i: (b, i, 0)),
        out_shape=jax.ShapeDtypeStruct((batch, seq, BRANCH_W), BF16),
        scratch_shapes=[pltpu.VMEM((MLA_HEADS, 1, blk), F32), pltpu.VMEM((MLA_HEADS, MLA_VT_ROWS, blk), F32)],
        compiler_params=_params("parallel", "arbitrary"),
        name="mla_attention",
    )(qt, k.reshape(batch, seq, MLA_QK), vt.reshape(batch, nq, vt_rows, blk), bz.reshape(batch, seq, BRANCH_W))


def _merge_kernel(x_ref, ya_ref, yb_ref, yc_ref, yd_ref, p_ref, wm_ref, bm_ref, wb_ref, wo_ref,
                  lng_ref, lnb_ref, wp_ref, wpg_ref, pg_ref, o_ref):
    d = D_MODEL
    x = x_ref[...]
    xb = x.astype(BF16)
    merged = None
    for n, y_ref in enumerate((ya_ref, yb_ref, yc_ref, yd_ref)):
        gate = _sigmoid(_dot(xb, wm_ref[:, n * d:(n + 1) * d]) + bm_ref[:, n * d:(n + 1) * d])
        term = gate * _dot(y_ref[...], wb_ref[n])
        merged = term if merged is None else merged + term
    z = DEEPNORM_ALPHA * x + _dot(merged.astype(BF16), wo_ref[...])
    mu = jnp.mean(z, axis=-1, keepdims=True)
    zc = z - mu
    var = jnp.mean(zc * zc, axis=-1, keepdims=True)
    xn = zc * lax.rsqrt(var + LN_EPS) * lng_ref[...] + lnb_ref[...]
    e = _dot(p_ref[...].astype(BF16), wp_ref[...]) * _sigmoid(_dot(xn.astype(BF16), wpg_ref[...]))
    o_ref[...] = xn + _rms(e, pg_ref[...])


def _merge(x2d, ys, p2d, w_merge, b_merge, w_branch, w_out, ln_g, ln_b, w_ple, w_ple_gate, ple_norm_g):
    t = x2d.shape[0]
    tm = MERGE_TM
    d = D_MODEL
    row_spec = lambda width: pl.BlockSpec((tm, width), lambda i: (i, 0))
    return pl.pallas_call(
        _merge_kernel,
        grid=(t // tm,),
        in_specs=[row_spec(d)] + [row_spec(BRANCH_W)] * N_BRANCH + [row_spec(PLE_DIM),
                  _resident((d, N_BRANCH * d)), _resident((1, N_BRANCH * d)),
                  _resident((N_BRANCH, BRANCH_W, d)), _resident((d, d)),
                  _resident((1, d)), _resident((1, d)), _resident((PLE_DIM, d)), _resident((d, d)),
                  _resident((1, d))],
        out_specs=row_spec(d),
        out_shape=jax.ShapeDtypeStruct((t, d), F32),
        compiler_params=_params("parallel"),
        name="merge",
    )(x2d, *ys, p2d, w_merge, b_merge, w_branch, w_out, ln_g, ln_b, w_ple, w_ple_gate, ple_norm_g)


def _rot_cols(w):
    half = w.shape[1] // 2
    return jnp.concatenate([-w[:, half:], w[:, :half]], axis=1)


def _rope_slot(w):
    z = jnp.zeros((w.shape[0], MLA_NOPE), w.dtype)
    z2 = jnp.zeros((w.shape[0], MLA_SLOT - MLA_NOPE - MLA_ROPE), w.dtype)
    return jnp.concatenate([z, w, z2], axis=1)


def _in_weights(w_in):
    o = [0]
    for s in (256, 256, 256, 256, 128, 32, 256, 256, 256, 256, 128, 128, 256):
        o.append(o[-1] + s)
    col = lambda k: w_in[:, o[k]:o[k + 1]]
    a_val, a_gate, a_z, c_q, c_kv, k_r, b_z, u, c_z, q, k, v, d_z = (col(k) for k in range(13))
    groups = [a_val, a_gate, a_z, c_q, c_kv, _rope_slot(k_r), _rope_slot(_rot_cols(k_r)), b_z, u, c_z, q, k, v, d_z]
    return jnp.concatenate(groups, axis=1).astype(BF16)


def _uq_weights(w_uq):
    hd = MLA_NOPE + MLA_ROPE
    main, rot = [], []
    for h in range(MLA_HEADS):
        nope = w_uq[:, h * hd:h * hd + MLA_NOPE]
        rope_w = w_uq[:, h * hd + MLA_NOPE:(h + 1) * hd]
        pad = jnp.zeros((w_uq.shape[0], MLA_SLOT - hd), w_uq.dtype)
        main.append(jnp.concatenate([nope, rope_w, pad], axis=1))
        rot.append(_rope_slot(_rot_cols(rope_w)))
    return jnp.concatenate(main + rot, axis=1).astype(BF16)


def _ukv_weights(w_ukv):
    hd = MLA_NOPE + MLA_V
    ks, vs = [], []
    for h in range(MLA_HEADS):
        k_nope = w_ukv[:, h * hd:h * hd + MLA_NOPE]
        ks.append(jnp.concatenate([k_nope, jnp.zeros((w_ukv.shape[0], MLA_SLOT - MLA_NOPE), w_ukv.dtype)], axis=1))
        vs.append(w_ukv[:, h * hd + MLA_NOPE:(h + 1) * hd])
    return jnp.concatenate(ks + vs, axis=1).astype(BF16)


def _rope_tables(seq):
    pos = jnp.arange(seq, dtype=F32)
    inv_freq = ROPE_THETA ** (-jnp.arange(0, MLA_ROPE, 2, dtype=F32) / MLA_ROPE)
    ang = pos[:, None] * inv_freq[None, :]
    cos, sin = jnp.cos(ang), jnp.sin(ang)
    ones = jnp.ones((seq, MLA_NOPE), F32)
    zeros = jnp.zeros((seq, MLA_NOPE), F32)
    pad = jnp.zeros((seq, MLA_SLOT - MLA_NOPE - MLA_ROPE), F32)
    cos_t = jnp.concatenate([ones, cos, cos, pad], axis=1)
    sin_t = jnp.concatenate([zeros, sin, sin, pad], axis=1)
    return cos_t, sin_t


def _ssm_weights(a_re, a_im, log_dt, b_re, b_im, c_re, c_im):
    g, p, h = SSM_GROUPS, SSM_STATE, SSM_GROUP
    dt = jnp.exp(log_dt.astype(F32))[:, None]
    lr, li = a_re.astype(F32), a_im.astype(F32)
    mag = jnp.exp(lr * dt)
    lb_re, lb_im = mag * jnp.cos(li * dt), mag * jnp.sin(li * dt)
    den = lr * lr + li * li
    nr, ni = lb_re - 1.0, lb_im
    f_re = ((nr * lr + ni * li) / den)[..., None]
    f_im = ((ni * lr - nr * li) / den)[..., None]
    bb_re = f_re * b_re - f_im * b_im
    bb_im = f_re * b_im + f_im * b_re
    eye = jnp.eye(g, dtype=F32)
    bd = lambda m: jnp.einsum('gph,gk->ghkp', m, eye).reshape(g * h, g * p)
    bmat = jnp.concatenate([bd(bb_re), bd(bb_im)], axis=1).astype(BF16)
    cd = lambda m: jnp.einsum('ghp,gk->gpkh', m, eye).reshape(g * p, g * h)
    cmat = jnp.concatenate([cd(c_re.astype(F32)), -cd(c_im.astype(F32))], axis=0).astype(BF16)
    pr, pi = lb_re.reshape(1, g * p), lb_im.reshape(1, g * p)
    rows = []
    for _ in range(8):
        rows.append(jnp.concatenate([pr, pi], axis=1))
        pr, pi = pr * pr - pi * pi, 2.0 * pr * pi
    return bmat, cmat, jnp.concatenate(rows, axis=0)


def kernel(x, p, w_in, w_merge, b_merge, conv_w, conv_b, conv_norm_g, conv_norm_b, w_pw2, mla_q_norm_g,
           mla_kv_norm_g, w_uq, w_ukv, ssm_a_re, ssm_a_im, ssm_log_dt, ssm_b_re, ssm_b_im, ssm_c_re,
           ssm_c_im, ssm_d, w_glu, attn_sinks, w_branch, w_out, ln_g, ln_b, w_ple, w_ple_gate, ple_norm_g):
    batch, seq, d = x.shape
    t = batch * seq
    cos_t, sin_t = _rope_tables(seq)
    x2d = x.reshape(t, d)
    row = lambda a: a.reshape(1, -1).astype(F32)
    for l in range(DEPTH):
        conv_in, mla_in, bz, ssm_in, swa_in, dz = _in_proj(x2d, _in_weights(w_in[l]))
        y_a = _conv_branch(conv_in, conv_w[l], row(conv_b[l]), row(conv_norm_g[l]), row(conv_norm_b[l]),
                           w_pw2[l].astype(BF16), batch, seq)
        qt, k, vt = _mla_prep(mla_in, cos_t, sin_t, row(mla_q_norm_g[l]), row(mla_kv_norm_g[l]),
                              _uq_weights(w_uq[l]), _ukv_weights(w_ukv[l]), seq)
        y_b = _mla_attention(qt, k, vt, bz, batch, seq)
        bmat, cmat, powers = _ssm_weights(ssm_a_re[l], ssm_a_im[l], ssm_log_dt[l], ssm_b_re[l], ssm_b_im[l],
                                          ssm_c_re[l], ssm_c_im[l])
        y_c = _ssm_branch(ssm_in, bmat, cmat, powers, row(ssm_d[l]), w_glu[l].astype(BF16), batch, seq)
        y_d = _swa_branch(swa_in, dz, attn_sinks[l].astype(F32), batch, seq)
        ys = [y.reshape(t, BRANCH_W) for y in (y_a, y_b, y_c, y_d)]
        x2d = _merge(x2d, ys, p[l].reshape(t, PLE_DIM), w_merge[l].astype(BF16), row(b_merge[l]),
                     w_branch[l].astype(BF16), w_out[l].astype(BF16), row(ln_g[l]), row(ln_b[l]),
                     w_ple[l].astype(BF16), w_ple_gate[l].astype(BF16), row(ple_norm_g[l]))
    return x2d.reshape(batch, seq, d)
```

```python
import functools
import math

import jax
import jax.numpy as jnp
from jax import lax
from jax.experimental import pallas as pl
from jax.experimental.pallas import tpu as pltpu

D_MODEL = 1024
DEPTH = 4
PLE_DIM = 256
N_BRANCH = 4
BRANCH_W = 256
CONV_W = 31
MLA_HEADS = 4
MLA_NOPE = 64
MLA_ROPE = 32
MLA_V = 64
MLA_Q_RANK = 256
MLA_KV_RANK = 128
ROPE_THETA = 10000.0
SSM_GROUP = 16
SSM_GROUPS = 16
SSM_STATE = 64
SWA_HEADS = 4
SWA_KV_HEADS = 2
SWA_HEAD_DIM = 64
WINDOW = 128
DEEPNORM_ALPHA = (2.0 * DEPTH) ** 0.25
LN_EPS = 1e-5
RMS_EPS = 1e-6

LANES = 128
VMEM_LIMIT_BYTES = 56 * 1024 * 1024

IN_TM = 512
SEQ_BLOCK = 512
SSM_CHUNK = 2048
SSM_SUB = 8
SSM_ROWS = 512
ATTN_BLK = 512
MERGE_TM = 256
CONV_HALO = 32
CONV_ROWS = 64

MLA_SLOT = LANES
MLA_VT_ROWS = MLA_V + 16
LOG2E = math.log2(math.e)
NSTATE = SSM_GROUPS * SSM_STATE
NEG_BIG = -1e30

F32 = jnp.float32
BF16 = jnp.bfloat16


def _params(*sem):
    return pltpu.CompilerParams(dimension_semantics=sem, vmem_limit_bytes=VMEM_LIMIT_BYTES)


def _resident(shape):
    nd = len(shape)
    return pl.BlockSpec(shape, lambda *_: (0,) * nd, pipeline_mode=pl.Buffered(1))


def _dot(a, b):
    return jnp.dot(a, b, preferred_element_type=F32)


def _dot_nt(a, b):
    return lax.dot_general(a, b, (((1,), (1,)), ((), ())), preferred_element_type=F32)


def _sigmoid(x):
    return 1.0 / (1.0 + jnp.exp(-x))


def _silu(x):
    return x * _sigmoid(x)


IN_GROUPS = (
    ("conv", 3 * BRANCH_W, F32),
    ("mla", MLA_Q_RANK + MLA_KV_RANK + 2 * LANES, F32),
    ("bz", BRANCH_W, F32),
    ("ssm", 2 * BRANCH_W, F32),
    ("swa", 2 * BRANCH_W, BF16),
    ("dz", BRANCH_W, F32),
)
IN_EXT_WIDTH = sum(w for _, w, _ in IN_GROUPS)


def _in_proj_kernel(x_ref, w_ref, *out_refs):
    xb = x_ref[...].astype(BF16)
    start = 0
    for (_, width, dtype), o_ref in zip(IN_GROUPS, out_refs):
        o_ref[...] = _dot(xb, w_ref[:, start:start + width]).astype(dtype)
        start += width


def _in_proj(x2d, w_in_ext):
    t = x2d.shape[0]
    return pl.pallas_call(
        _in_proj_kernel,
        grid=(t // IN_TM,),
        in_specs=[pl.BlockSpec((IN_TM, D_MODEL), lambda i: (i, 0)),
                  _resident((D_MODEL, IN_EXT_WIDTH))],
        out_specs=[pl.BlockSpec((IN_TM, w), lambda i: (i, 0)) for _, w, _ in IN_GROUPS],
        out_shape=[jax.ShapeDtypeStruct((t, w), d) for _, w, d in IN_GROUPS],
        compiler_params=_params("parallel"),
        name="in_proj",
    )(x2d, w_in_ext)


def _conv_kernel(h_ref, cw_ref, cb_ref, ng_ref, nb_ref, pw2_ref, o_ref, hbuf, ybuf):
    i = pl.program_id(1)
    bs = SEQ_BLOCK
    w = BRANCH_W

    @pl.when(i == 0)
    def _():
        hbuf[0:CONV_HALO, :] = jnp.zeros((CONV_HALO, w), F32)

    @pl.when(i > 0)
    def _():
        hbuf[0:CONV_HALO, :] = hbuf[bs:bs + CONV_HALO, :]

    hbuf[CONV_HALO:, :] = h_ref[:, 0:w] * _sigmoid(h_ref[:, w:2 * w])

    base = CONV_HALO - (CONV_W - 1)
    tile = 8
    for c in range(bs // CONV_ROWS):
        r0 = c * CONV_ROWS
        acc = jnp.broadcast_to(cb_ref[...], (CONV_ROWS, w))
        for r in range(tile):
            span = CONV_ROWS + (tile if r else 0)
            part = None
            for j in range((r - base) % tile, CONV_W, tile):
                lo = r0 + base + j - r
                term = cw_ref[j:j + 1, :] * hbuf[lo:lo + span, :]
                part = term if part is None else part + term
            acc = acc + part[r:r + CONV_ROWS, :]
        mu = jnp.mean(acc, axis=-1, keepdims=True)
        d = acc - mu
        var = jnp.mean(d * d, axis=-1, keepdims=True)
        y = d * lax.rsqrt(var + LN_EPS) * ng_ref[...] + nb_ref[...]
        ybuf[r0:r0 + CONV_ROWS, :] = _silu(y).astype(BF16)
    o_ref[...] = (_dot(ybuf[...], pw2_ref[...]) * _silu(h_ref[:, 2 * w:3 * w])).astype(o_ref.dtype)


def _conv_branch(conv_in, conv_w, conv_b, norm_g, norm_b, w_pw2, batch, seq):
    bs = SEQ_BLOCK
    w = BRANCH_W
    return pl.pallas_call(
        _conv_kernel,
        grid=(batch, seq // bs),
        in_specs=[pl.BlockSpec((None, bs, 3 * w), lambda b, i: (b, i, 0)),
                  _resident((CONV_W, w)), _resident((1, w)), _resident((1, w)), _resident((1, w)),
                  _resident((w, w))],
        out_specs=pl.BlockSpec((None, bs, w), lambda b, i: (b, i, 0)),
        out_shape=jax.ShapeDtypeStruct((batch, seq, w), BF16),
        scratch_shapes=[pltpu.VMEM((bs + CONV_HALO, w), F32), pltpu.VMEM((bs, w), BF16)],
        compiler_params=_params("parallel", "arbitrary"),
        name="conv_branch",
    )(conv_in.reshape(batch, seq, 3 * w), conv_w, conv_b, norm_g, norm_b, w_pw2)


SSM_BOUND = SSM_CHUNK // SSM_SUB
assert SSM_BOUND % 8 == 0


def _ssm_kernel(in_ref, bp_ref, cp_ref, g_ref, sc_ref, d_ref, wglu_ref, o_ref, carry, ubuf, ybuf, hbuf):
    i = pl.program_id(1)
    n = NSTATE
    w = BRANCH_W
    m = SSM_BOUND

    @pl.when(i == 0)
    def _():
        carry[...] = jnp.zeros_like(carry)

    nt = w // LANES
    for k in range(nt):
        ubuf[k] = in_ref[:, k * LANES:(k + 1) * LANES]

    x = None
    for p in range(SSM_SUB):
        up = jnp.concatenate([ubuf[k, pl.ds(p, m, stride=SSM_SUB), :] for k in range(nt)], axis=1).astype(BF16)
        t = _dot(up, bp_ref[p])
        x = t if x is None else x + t
    tile = 8
    first = lax.broadcasted_iota(jnp.int32, (tile, n), 0) == 0
    cr = carry[0:1, :]
    ci = carry[1:2, :]
    for k in range(m // tile):
        tr = x[k * tile:(k + 1) * tile, 0:n]
        ti = x[k * tile:(k + 1) * tile, n:2 * n]
        for lvl in range(3):
            s = 1 << lvl
            ar = sc_ref[lvl, :, 0:n]
            ai = sc_ref[lvl, :, n:2 * n]
            sr = pltpu.roll(tr, s, 0)
            si = pltpu.roll(ti, s, 0)
            tr, ti = tr + (ar * sr - ai * si), ti + (ar * si + ai * sr)
        pr = sc_ref[3, :, 0:n]
        pi = sc_ref[3, :, n:2 * n]
        tr, ti = tr + (pr * cr - pi * ci), ti + (pr * ci + pi * cr)
        hbuf[k * tile:(k + 1) * tile, 0:n] = jnp.where(first, cr, pltpu.roll(tr, 1, 0))
        hbuf[k * tile:(k + 1) * tile, n:2 * n] = jnp.where(first, ci, pltpu.roll(ti, 1, 0))
        cr = tr[tile - 1:tile, :]
        ci = ti[tile - 1:tile, :]
    carry[0:1, :] = cr
    carry[1:2, :] = ci
    hcat = hbuf[...].astype(BF16)
    for p in range(SSM_SUB):
        yp = _dot(hcat, cp_ref[p])
        for k in range(nt):
            ybuf[k, pl.ds(p, m, stride=SSM_SUB), :] = yp[:, k * LANES:(k + 1) * LANES]

    rb = SSM_ROWS
    pos = lax.broadcasted_iota(jnp.int32, (rb, w), 0) % SSM_SUB
    for r0 in range(0, SSM_CHUNK, rb):
        u = in_ref[r0:r0 + rb, 0:w]
        yb = jnp.concatenate([ybuf[k, r0:r0 + rb, :] for k in range(nt)], axis=1)
        y = yb + _dot(u.astype(BF16), g_ref[0])
        for d in range(1, SSM_SUB):
            ud = jnp.where(pos >= d, pltpu.roll(u, d, 0), 0.0)
            y = y + _dot(ud.astype(BF16), g_ref[d])
        y = jax.nn.gelu(y + d_ref[...] * u)
        g = _dot(y.astype(BF16), wglu_ref[...])
        out = g[:, 0:w] * _sigmoid(g[:, w:2 * w]) * _silu(in_ref[r0:r0 + rb, w:2 * w])
        o_ref[r0:r0 + rb, :] = out.astype(o_ref.dtype)


def _ssm_branch(ssm_in, bp, cp, gd, powers, d_skip, w_glu, batch, seq):
    rows = SSM_CHUNK
    w = BRANCH_W
    return pl.pallas_call(
        _ssm_kernel,
        grid=(batch, seq // rows),
        in_specs=[pl.BlockSpec((None, rows, 2 * w), lambda b, i: (b, i, 0)),
                  _resident((SSM_SUB, w, 2 * NSTATE)), _resident((SSM_SUB, 2 * NSTATE, w)),
                  _resident((SSM_SUB, w, w)),
                  _resident((4, 8, 2 * NSTATE)), _resident((1, w)), _resident((w, 2 * w))],
        out_specs=pl.BlockSpec((None, rows, w), lambda b, i: (b, i, 0)),
        out_shape=jax.ShapeDtypeStruct((batch, seq, w), BF16),
        scratch_shapes=[pltpu.VMEM((8, NSTATE), F32), pltpu.VMEM((w // LANES, rows, LANES), F32),
                        pltpu.VMEM((w // LANES, rows, LANES), F32), pltpu.VMEM((SSM_BOUND, 2 * NSTATE), F32)],
        compiler_params=_params("parallel", "arbitrary"),
        name="ssm_branch",
    )(ssm_in.reshape(batch, seq, 2 * w), bp, cp, gd, powers, d_skip, w_glu)


SWA_GROUP = SWA_HEADS // SWA_KV_HEADS
SWA_BLOCKS = SEQ_BLOCK // WINDOW


def _swa_kernel(sink_ref, cur_ref, prev_ref, dz_ref, o_ref, kvbuf, ybuf):
    i = pl.program_id(1)
    hd = SWA_HEAD_DIM
    kv_w = SWA_KV_HEADS * hd
    q_w = SWA_HEADS * hd
    win = WINDOW
    scale = hd ** -0.5

    kvbuf[0:win, :] = prev_ref[:, q_w:q_w + 2 * kv_w]
    kvbuf[win:, :] = cur_ref[:, q_w:q_w + 2 * kv_w]

    rows = SWA_GROUP * win
    qrow = lax.broadcasted_iota(jnp.int32, (rows, 2 * win), 0) % win + win
    kcol = lax.broadcasted_iota(jnp.int32, (rows, 2 * win), 1)
    rel = qrow - kcol
    band = (rel >= 0) & (rel < win)
    head_row = lax.broadcasted_iota(jnp.int32, (rows, 1), 0) // win

    for j in range(SWA_BLOCKS):
        first_block = (i * SWA_BLOCKS + j) == 0
        valid = band & (jnp.logical_not(first_block) | (kcol >= win))
        for kvh in range(SWA_KV_HEADS):
            q2 = jnp.concatenate(
                [cur_ref[j * win:(j + 1) * win, (kvh * SWA_GROUP + g) * hd:(kvh * SWA_GROUP + g + 1) * hd]
                 for g in range(SWA_GROUP)], axis=0)
            k2 = kvbuf[j * win:(j + 2) * win, kvh * hd:(kvh + 1) * hd]
            v2 = kvbuf[j * win:(j + 2) * win, kv_w + kvh * hd:kv_w + (kvh + 1) * hd]
            s = jnp.where(valid, _dot_nt(q2, k2) * scale, NEG_BIG)
            sink = jnp.where(head_row == 0, sink_ref[kvh * SWA_GROUP], sink_ref[kvh * SWA_GROUP + 1])
            m = jnp.maximum(jnp.max(s, axis=-1, keepdims=True), sink)
            p = jnp.exp(s - m)
            denom = jnp.sum(p, axis=-1, keepdims=True) + jnp.exp(sink - m)
            o = _dot(p.astype(BF16), v2) / denom
            for g in range(SWA_GROUP):
                h = kvh * SWA_GROUP + g
                ybuf[j * win:(j + 1) * win, h * hd:(h + 1) * hd] = o[g * win:(g + 1) * win, :]
    o_ref[...] = (ybuf[...] * _silu(dz_ref[...])).astype(o_ref.dtype)


def _swa_branch(swa_in, dz, sinks, batch, seq):
    bs = SEQ_BLOCK
    w = BRANCH_W
    per = SWA_BLOCKS
    return pl.pallas_call(
        _swa_kernel,
        grid=(batch, seq // bs),
        in_specs=[pl.BlockSpec(memory_space=pltpu.SMEM),
                  pl.BlockSpec((None, bs, 2 * w), lambda b, i: (b, i, 0)),
                  pl.BlockSpec((None, WINDOW, 2 * w), lambda b, i: (b, jnp.maximum(i * per - 1, 0), 0)),
                  pl.BlockSpec((None, bs, w), lambda b, i: (b, i, 0))],
        out_specs=pl.BlockSpec((None, bs, w), lambda b, i: (b, i, 0)),
        out_shape=jax.ShapeDtypeStruct((batch, seq, w), BF16),
        scratch_shapes=[pltpu.VMEM((bs + WINDOW, 2 * SWA_KV_HEADS * SWA_HEAD_DIM), BF16),
                        pltpu.VMEM((bs, w), F32)],
        compiler_params=_params("parallel", "arbitrary"),
        name="swa_branch",
    )(sinks, swa_in.reshape(batch, seq, 2 * w), swa_in.reshape(batch, seq, 2 * w), dz.reshape(batch, seq, w))


MLA_QK = MLA_HEADS * MLA_SLOT


def _rms(x, g):
    ms = jnp.mean(x * x, axis=-1, keepdims=True)
    return x * lax.rsqrt(ms + RMS_EPS) * g


def _mla_prep_kernel(in_ref, cos_ref, sin_ref, qg_ref, kvg_ref, wuq_ref, wukv_ref, qt_ref, k_ref, vt_ref):
    scale = (MLA_NOPE + MLA_ROPE) ** -0.5 * LOG2E
    cq = _rms(in_ref[:, 0:MLA_Q_RANK], qg_ref[...]).astype(BF16)
    ckv = _rms(in_ref[:, MLA_Q_RANK:MLA_Q_RANK + MLA_KV_RANK], kvg_ref[...]).astype(BF16)
    off = MLA_Q_RANK + MLA_KV_RANK
    cos = cos_ref[...]
    sin = sin_ref[...]
    k_rope = in_ref[:, off:off + LANES] * cos + in_ref[:, off + LANES:off + 2 * LANES] * sin
    kv = _dot(ckv, wukv_ref[...])
    for h in range(MLA_HEADS):
        lo = h * MLA_SLOT
        qm = _dot(cq, wuq_ref[:, lo:lo + MLA_SLOT])
        qr = _dot(cq, wuq_ref[:, MLA_QK + lo:MLA_QK + lo + MLA_SLOT])
        qt_ref[lo:lo + MLA_SLOT, :] = ((qm * cos + qr * sin) * scale).T.astype(BF16)
        k_ref[:, lo:lo + MLA_SLOT] = (kv[:, lo:lo + MLA_SLOT] + k_rope).astype(BF16)
    vt = kv[:, MLA_QK:].T.astype(BF16)
    ones = jnp.ones((MLA_VT_ROWS - MLA_V, vt.shape[1]), BF16)
    for h in range(MLA_HEADS):
        vt_ref[h * MLA_VT_ROWS:h * MLA_VT_ROWS + MLA_V, :] = vt[h * MLA_V:(h + 1) * MLA_V, :]
        vt_ref[h * MLA_VT_ROWS + MLA_V:(h + 1) * MLA_VT_ROWS, :] = ones


def _mla_prep(mla_in, cos_t, sin_t, q_norm_g, kv_norm_g, w_uq_ext, w_ukv_ext, seq):
    t = mla_in.shape[0]
    tm = ATTN_BLK
    nsb = seq // tm
    vw = MLA_HEADS * MLA_V
    vt_rows = MLA_HEADS * MLA_VT_ROWS
    return pl.pallas_call(
        _mla_prep_kernel,
        grid=(t // tm,),
        in_specs=[pl.BlockSpec((tm, mla_in.shape[1]), lambda i: (i, 0)),
                  pl.BlockSpec((tm, LANES), lambda i: (i % nsb, 0)),
                  pl.BlockSpec((tm, LANES), lambda i: (i % nsb, 0)),
                  _resident((1, MLA_Q_RANK)), _resident((1, MLA_KV_RANK)),
                  _resident((MLA_Q_RANK, 2 * MLA_QK)), _resident((MLA_KV_RANK, MLA_QK + vw))],
        out_specs=[pl.BlockSpec((None, MLA_QK, tm), lambda i: (i, 0, 0)),
                   pl.BlockSpec((tm, MLA_QK), lambda i: (i, 0)),
                   pl.BlockSpec((None, vt_rows, tm), lambda i: (i, 0, 0))],
        out_shape=[jax.ShapeDtypeStruct((t // tm, MLA_QK, tm), BF16), jax.ShapeDtypeStruct((t, MLA_QK), BF16),
                   jax.ShapeDtypeStruct((t // tm, vt_rows, tm), BF16)],
        compiler_params=_params("parallel"),
        name="mla_prep",
    )(mla_in, cos_t, sin_t, q_norm_g, kv_norm_g, w_uq_ext, w_ukv_ext)


def _attn_kernel(qt_ref, k_ref, vt_ref, bz_ref, o_ref, m_sc, acc_sc):
    i = pl.program_id(1)
    blk = ATTN_BLK
    krow = lax.broadcasted_iota(jnp.int32, (blk, blk), 0)
    qcol = lax.broadcasted_iota(jnp.int32, (blk, blk), 1)
    causal = krow <= qcol
    m_sc[...] = jnp.full(m_sc.shape, NEG_BIG, F32)
    acc_sc[...] = jnp.zeros(acc_sc.shape, F32)

    def step(j, masked):
        start = pl.multiple_of(j * blk, blk)
        scores = []
        for h in range(MLA_HEADS):
            kb = k_ref[pl.ds(start, blk), h * MLA_SLOT:(h + 1) * MLA_SLOT]
            scores.append(_dot(kb, qt_ref[h * MLA_SLOT:(h + 1) * MLA_SLOT, :]))
        for h in range(MLA_HEADS):
            s = scores[h]
            if masked:
                s = jnp.where(causal, s, NEG_BIG)
            m_prev = m_sc[h]
            m_new = jnp.maximum(m_prev, jnp.max(s, axis=0, keepdims=True))
            alpha = jnp.exp2(m_prev - m_new)
            p = jnp.exp2(s - m_new).astype(BF16)
            vb = vt_ref[j, h * MLA_VT_ROWS:(h + 1) * MLA_VT_ROWS, :]
            acc_sc[h] = alpha * acc_sc[h] + _dot(vb, p)
            m_sc[h] = m_new

    def body(j, c):
        step(j, False)
        return c

    lax.fori_loop(0, i, body, 0)
    step(i, True)
    outs = [acc_sc[h, 0:MLA_V, :] * (1.0 / acc_sc[h, MLA_V:MLA_V + 1, :]) for h in range(MLA_HEADS)]
    y = jnp.concatenate(outs, axis=0).T
    o_ref[...] = (y * _silu(bz_ref[...])).astype(o_ref.dtype)


def _mla_attention(qt, k, vt, bz, batch, seq):
    blk = ATTN_BLK
    nq = seq // blk
    vt_rows = MLA_HEADS * MLA_VT_ROWS
    return pl.pallas_call(
        _attn_kernel,
        grid=(batch, nq),
        in_specs=[pl.BlockSpec((None, MLA_QK, blk), lambda b, i: (b * nq + i, 0, 0)),
                  pl.BlockSpec((None, seq, MLA_QK), lambda b, i: (b, 0, 0)),
                  pl.BlockSpec((None, nq, vt_rows, blk), lambda b, i: (b, 0, 0, 0)),
                  pl.BlockSpec((None, blk, BRANCH_W), lambda b, i: (b, i, 0))],
        out_specs=pl.BlockSpec((None, blk, BRANCH_W), lambda b, i: (b, i, 0)),
        out_shape=jax.ShapeDtypeStruct((batch, seq, BRANCH_W), BF16),
        scratch_shapes=[pltpu.VMEM((MLA_HEADS, 1, blk), F32), pltpu.VMEM((MLA_HEADS, MLA_VT_ROWS, blk), F32)],
        compiler_params=_params("parallel", "arbitrary"),
        name="mla_attention",
    )(qt, k.reshape(batch, seq, MLA_QK), vt.reshape(batch, nq, vt_rows, blk), bz.reshape(batch, seq, BRANCH_W))


def _merge_kernel(x_ref, ya_ref, yb_ref, yc_ref, yd_ref, p_ref, wm_ref, bm_ref, wb_ref, wo_ref,
                  lng_ref, lnb_ref, wp_ref, wpg_ref, pg_ref, o_ref):
    d = D_MODEL
    x = x_ref[...]
    xb = x.astype(BF16)
    merged = None
    for n, y_ref in enumerate((ya_ref, yb_ref, yc_ref, yd_ref)):
        gate = _sigmoid(_dot(xb, wm_ref[:, n * d:(n + 1) * d]) + bm_ref[:, n * d:(n + 1) * d])
        term = gate * _dot(y_ref[...], wb_ref[n])
        merged = term if merged is None else merged + term
    z = DEEPNORM_ALPHA * x + _dot(merged.astype(BF16), wo_ref[...])
    mu = jnp.mean(z, axis=-1, keepdims=True)
    zc = z - mu
    var = jnp.mean(zc * zc, axis=-1, keepdims=True)
    xn = zc * lax.rsqrt(var + LN_EPS) * lng_ref[...] + lnb_ref[...]
    e = _dot(p_ref[...].astype(BF16), wp_ref[...]) * _sigmoid(_dot(xn.astype(BF16), wpg_ref[...]))
    o_ref[...] = xn + _rms(e, pg_ref[...])


def _merge(x2d, ys, p2d, w_merge, b_merge, w_branch, w_out, ln_g, ln_b, w_ple, w_ple_gate, ple_norm_g):
    t = x2d.shape[0]
    tm = MERGE_TM
    d = D_MODEL
    row_spec = lambda width: pl.BlockSpec((tm, width), lambda i: (i, 0))
    return pl.pallas_call(
        _merge_kernel,
        grid=(t // tm,),
        in_specs=[row_spec(d)] + [row_spec(BRANCH_W)] * N_BRANCH + [row_spec(PLE_DIM),
                  _resident((d, N_BRANCH * d)), _resident((1, N_BRANCH * d)),
                  _resident((N_BRANCH, BRANCH_W, d)), _resident((d, d)),
                  _resident((1, d)), _resident((1, d)), _resident((PLE_DIM, d)), _resident((d, d)),
                  _resident((1, d))],
        out_specs=row_spec(d),
        out_shape=jax.ShapeDtypeStruct((t, d), F32),
        compiler_params=_params("parallel"),
        name="merge",
    )(x2d, *ys, p2d, w_merge, b_merge, w_branch, w_out, ln_g, ln_b, w_ple, w_ple_gate, ple_norm_g)


def _rot_cols(w):
    half = w.shape[1] // 2
    return jnp.concatenate([-w[:, half:], w[:, :half]], axis=1)


def _rope_slot(w):
    z = jnp.zeros((w.shape[0], MLA_NOPE), w.dtype)
    z2 = jnp.zeros((w.shape[0], MLA_SLOT - MLA_NOPE - MLA_ROPE), w.dtype)
    return jnp.concatenate([z, w, z2], axis=1)


def _in_weights(w_in):
    o = [0]
    for s in (256, 256, 256, 256, 128, 32, 256, 256, 256, 256, 128, 128, 256):
        o.append(o[-1] + s)
    col = lambda k: w_in[:, o[k]:o[k + 1]]
    a_val, a_gate, a_z, c_q, c_kv, k_r, b_z, u, c_z, q, k, v, d_z = (col(k) for k in range(13))
    groups = [a_val, a_gate, a_z, c_q, c_kv, _rope_slot(k_r), _rope_slot(_rot_cols(k_r)), b_z, u, c_z, q, k, v, d_z]
    return jnp.concatenate(groups, axis=1).astype(BF16)


def _uq_weights(w_uq):
    hd = MLA_NOPE + MLA_ROPE
    main, rot = [], []
    for h in range(MLA_HEADS):
        nope = w_uq[:, h * hd:h * hd + MLA_NOPE]
        rope_w = w_uq[:, h * hd + MLA_NOPE:(h + 1) * hd]
        pad = jnp.zeros((w_uq.shape[0], MLA_SLOT - hd), w_uq.dtype)
        main.append(jnp.concatenate([nope, rope_w, pad], axis=1))
        rot.append(_rope_slot(_rot_cols(rope_w)))
    return jnp.concatenate(main + rot, axis=1).astype(BF16)


def _ukv_weights(w_ukv):
    hd = MLA_NOPE + MLA_V
    ks, vs = [], []
    for h in range(MLA_HEADS):
        k_nope = w_ukv[:, h * hd:h * hd + MLA_NOPE]
        ks.append(jnp.concatenate([k_nope, jnp.zeros((w_ukv.shape[0], MLA_SLOT - MLA_NOPE), w_ukv.dtype)], axis=1))
        vs.append(w_ukv[:, h * hd + MLA_NOPE:(h + 1) * hd])
    return jnp.concatenate(ks + vs, axis=1).astype(BF16)


def _rope_tables(seq):
    pos = jnp.arange(seq, dtype=F32)
    inv_freq = ROPE_THETA ** (-jnp.arange(0, MLA_ROPE, 2, dtype=F32) / MLA_ROPE)
    ang = pos[:, None] * inv_freq[None, :]
    cos, sin = jnp.cos(ang), jnp.sin(ang)
    ones = jnp.ones((seq, MLA_NOPE), F32)
    zeros = jnp.zeros((seq, MLA_NOPE), F32)
    pad = jnp.zeros((seq, MLA_SLOT - MLA_NOPE - MLA_ROPE), F32)
    cos_t = jnp.concatenate([ones, cos, cos, pad], axis=1)
    sin_t = jnp.concatenate([zeros, sin, sin, pad], axis=1)
    return cos_t, sin_t


def _ssm_weights(a_re, a_im, log_dt, b_re, b_im, c_re, c_im):
    g, p, h = SSM_GROUPS, SSM_STATE, SSM_GROUP
    dt = jnp.exp(log_dt.astype(F32))[:, None]
    lr, li = a_re.astype(F32), a_im.astype(F32)
    mag = jnp.exp(lr * dt)
    lb_re, lb_im = mag * jnp.cos(li * dt), mag * jnp.sin(li * dt)
    den = lr * lr + li * li
    nr, ni = lb_re - 1.0, lb_im
    f_re = ((nr * lr + ni * li) / den)[..., None]
    f_im = ((ni * lr - nr * li) / den)[..., None]
    bb_re = f_re * b_re - f_im * b_im
    bb_im = f_re * b_im + f_im * b_re
    c_re, c_im = c_re.astype(F32), c_im.astype(F32)
    eye = jnp.eye(g, dtype=F32)[:, None, :, None]
    bd = lambda m: (jnp.swapaxes(m, 1, 2)[:, :, None, :] * eye).reshape(g * h, g * p)
    cd = lambda m: (jnp.swapaxes(m, 1, 2)[:, :, None, :] * eye).reshape(g * p, g * h)
    gd = lambda m: (m[:, :, None, :] * eye).reshape(g * h, g * h)
    pows = [(jnp.ones_like(lb_re), jnp.zeros_like(lb_im))]
    for _ in range(SSM_SUB):
        qr, qi = pows[-1]
        pows.append((qr * lb_re - qi * lb_im, qr * lb_im + qi * lb_re))
    bp, cp, gl = [], [], []
    for k in range(SSM_SUB):
        qr, qi = (a[..., None] for a in pows[SSM_SUB - 1 - k])
        bp.append(jnp.concatenate([bd(qr * bb_re - qi * bb_im), bd(qr * bb_im + qi * bb_re)], axis=1))
        qr, qi = (a[:, None, :] for a in pows[k + 1])
        cp.append(jnp.concatenate([cd(c_re * qr - c_im * qi), -cd(c_re * qi + c_im * qr)], axis=0))
        qr, qi = (a[..., None] for a in pows[k])
        tr = jnp.swapaxes(qr * bb_re - qi * bb_im, 1, 2)[:, :, None, :]
        ti = jnp.swapaxes(qr * bb_im + qi * bb_re, 1, 2)[:, :, None, :]
        gl.append(gd(jnp.sum(tr * c_re[:, None, :, :] - ti * c_im[:, None, :, :], axis=-1)))
    mr, mi = (a.reshape(1, g * p) for a in pows[SSM_SUB])
    mu = [(mr, mi)]
    for _ in range(7):
        qr, qi = mu[-1]
        mu.append((qr * mr - qi * mi, qr * mi + qi * mr))
    cat = lambda q: jnp.concatenate(q, axis=1)
    rowid = jnp.arange(8)[:, None]
    tables = [jnp.where(rowid >= (1 << lvl), cat(mu[(1 << lvl) - 1]), 0.0) for lvl in range(3)]
    tables.append(jnp.concatenate([cat(q) for q in mu], axis=0))
    return (jnp.stack(bp).astype(BF16), jnp.stack(cp).astype(BF16), jnp.stack(gl).astype(BF16),
            jnp.stack(tables))


def kernel(x, p, w_in, w_merge, b_merge, conv_w, conv_b, conv_norm_g, conv_norm_b, w_pw2, mla_q_norm_g,
           mla_kv_norm_g, w_uq, w_ukv, ssm_a_re, ssm_a_im, ssm_log_dt, ssm_b_re, ssm_b_im, ssm_c_re,
           ssm_c_im, ssm_d, w_glu, attn_sinks, w_branch, w_out, ln_g, ln_b, w_ple, w_ple_gate, ple_norm_g):
    batch, seq, d = x.shape
    t = batch * seq
    cos_t, sin_t = _rope_tables(seq)
    x2d = x.reshape(t, d)
    row = lambda a: a.reshape(1, -1).astype(F32)
    for l in range(DEPTH):
        conv_in, mla_in, bz, ssm_in, swa_in, dz = _in_proj(x2d, _in_weights(w_in[l]))
        y_a = _conv_branch(conv_in, conv_w[l], row(conv_b[l]), row(conv_norm_g[l]), row(conv_norm_b[l]),
                           w_pw2[l].astype(BF16), batch, seq)
        qt, k, vt = _mla_prep(mla_in, cos_t, sin_t, row(mla_q_norm_g[l]), row(mla_kv_norm_g[l]),
                              _uq_weights(w_uq[l]), _ukv_weights(w_ukv[l]), seq)
        y_b = _mla_attention(qt, k, vt, bz, batch, seq)
        bp, cp, gd, powers = _ssm_weights(ssm_a_re[l], ssm_a_im[l], ssm_log_dt[l], ssm_b_re[l], ssm_b_im[l],
                                          ssm_c_re[l], ssm_c_im[l])
        y_c = _ssm_branch(ssm_in, bp, cp, gd, powers, row(ssm_d[l]), w_glu[l].astype(BF16), batch, seq)
        y_d = _swa_branch(swa_in, dz, attn_sinks[l].astype(F32), batch, seq)
        ys = [y.reshape(t, BRANCH_W) for y in (y_a, y_b, y_c, y_d)]
        x2d = _merge(x2d, ys, p[l].reshape(t, PLE_DIM), w_merge[l].astype(BF16), row(b_merge[l]),
                     w_branch[l].astype(BF16), w_out[l].astype(BF16), row(ln_g[l]), row(ln_b[l]),
                     w_ple[l].astype(BF16), w_ple_gate[l].astype(BF16), row(ple_norm_g[l]))
    return x2d.reshape(batch, seq, d)
```

```python
import functools
import math

import jax
import jax.numpy as jnp
from jax import lax
from jax.experimental import pallas as pl
from jax.experimental.pallas import tpu as pltpu

D_MODEL = 1024
DEPTH = 4
PLE_DIM = 256
N_BRANCH = 4
BRANCH_W = 256
CONV_W = 31
MLA_HEADS = 4
MLA_NOPE = 64
MLA_ROPE = 32
MLA_V = 64
MLA_Q_RANK = 256
MLA_KV_RANK = 128
ROPE_THETA = 10000.0
SSM_GROUP = 16
SSM_GROUPS = 16
SSM_STATE = 64
SWA_HEADS = 4
SWA_KV_HEADS = 2
SWA_HEAD_DIM = 64
WINDOW = 128
DEEPNORM_ALPHA = (2.0 * DEPTH) ** 0.25
LN_EPS = 1e-5
RMS_EPS = 1e-6

LANES = 128
VMEM_LIMIT_BYTES = 56 * 1024 * 1024

IN_TM = 512
SEQ_BLOCK = 512
SSM_CHUNK = 2048
SSM_SUB = 8
SSM_ROWS = 512
ATTN_BLK = 512
MERGE_TM = 512
CONV_HALO = 32
CONV_ROWS = 64

MLA_SLOT = LANES
MLA_VT_ROWS = MLA_V + 16
LOG2E = math.log2(math.e)
NSTATE = SSM_GROUPS * SSM_STATE
NEG_BIG = -1e30

F32 = jnp.float32
BF16 = jnp.bfloat16


def _params(*sem):
    return pltpu.CompilerParams(dimension_semantics=sem, vmem_limit_bytes=VMEM_LIMIT_BYTES)


def _resident(l, shape):
    nd = len(shape)
    return pl.BlockSpec((None,) + tuple(shape), lambda *_: (l,) + (0,) * nd, pipeline_mode=pl.Buffered(1))


def _dot(a, b):
    return jnp.dot(a, b, preferred_element_type=F32)


def _dot_nt(a, b):
    return lax.dot_general(a, b, (((1,), (1,)), ((), ())), preferred_element_type=F32)


def _sigmoid(x):
    return 1.0 / (1.0 + jnp.exp(-x))


def _silu(x):
    return x * _sigmoid(x)


IN_GROUPS = (
    ("conv", 3 * BRANCH_W, F32),
    ("mla", MLA_Q_RANK + MLA_KV_RANK + 2 * LANES, F32),
    ("bz", BRANCH_W, F32),
    ("ssm", 2 * BRANCH_W, F32),
    ("swa", 2 * BRANCH_W, BF16),
    ("dz", BRANCH_W, F32),
)
IN_EXT_WIDTH = sum(w for _, w, _ in IN_GROUPS)


def _in_proj_kernel(x_ref, w_ref, *out_refs):
    xb = x_ref[...].astype(BF16)
    start = 0
    for (_, width, dtype), o_ref in zip(IN_GROUPS, out_refs):
        o_ref[...] = _dot(xb, w_ref[:, start:start + width]).astype(dtype)
        start += width


def _in_proj(l, x2d, w_in_ext):
    t = x2d.shape[0]
    return pl.pallas_call(
        _in_proj_kernel,
        grid=(t // IN_TM,),
        in_specs=[pl.BlockSpec((IN_TM, D_MODEL), lambda i: (i, 0)),
                  _resident(l, (D_MODEL, IN_EXT_WIDTH))],
        out_specs=[pl.BlockSpec((IN_TM, w), lambda i: (i, 0)) for _, w, _ in IN_GROUPS],
        out_shape=[jax.ShapeDtypeStruct((t, w), d) for _, w, d in IN_GROUPS],
        compiler_params=_params("parallel"),
        name="in_proj",
    )(x2d, w_in_ext)


def _conv_kernel(h_ref, cw_ref, cb_ref, ng_ref, nb_ref, pw2_ref, o_ref, hbuf, ybuf):
    i = pl.program_id(1)
    bs = SEQ_BLOCK
    w = BRANCH_W

    @pl.when(i == 0)
    def _():
        hbuf[0:CONV_HALO, :] = jnp.zeros((CONV_HALO, w), F32)

    @pl.when(i > 0)
    def _():
        hbuf[0:CONV_HALO, :] = hbuf[bs:bs + CONV_HALO, :]

    hbuf[CONV_HALO:, :] = h_ref[:, 0:w] * _sigmoid(h_ref[:, w:2 * w])

    base = CONV_HALO - (CONV_W - 1)
    tile = 8
    for c in range(bs // CONV_ROWS):
        r0 = c * CONV_ROWS
        acc = jnp.broadcast_to(cb_ref[...], (CONV_ROWS, w))
        for r in range(tile):
            span = CONV_ROWS + (tile if r else 0)
            part = None
            for j in range((r - base) % tile, CONV_W, tile):
                lo = r0 + base + j - r
                term = cw_ref[j:j + 1, :] * hbuf[lo:lo + span, :]
                part = term if part is None else part + term
            acc = acc + part[r:r + CONV_ROWS, :]
        mu = jnp.mean(acc, axis=-1, keepdims=True)
        d = acc - mu
        var = jnp.mean(d * d, axis=-1, keepdims=True)
        y = d * lax.rsqrt(var + LN_EPS) * ng_ref[...] + nb_ref[...]
        ybuf[r0:r0 + CONV_ROWS, :] = _silu(y).astype(BF16)
    o_ref[...] = (_dot(ybuf[...], pw2_ref[...]) * _silu(h_ref[:, 2 * w:3 * w])).astype(o_ref.dtype)


def _conv_branch(l, conv_in, conv_w, conv_b, norm_g, norm_b, w_pw2, batch, seq):
    bs = SEQ_BLOCK
    w = BRANCH_W
    return pl.pallas_call(
        _conv_kernel,
        grid=(batch, seq // bs),
        in_specs=[pl.BlockSpec((None, bs, 3 * w), lambda b, i: (b, i, 0)),
                  _resident(l, (CONV_W, w)), _resident(l, (1, w)), _resident(l, (1, w)), _resident(l, (1, w)),
                  _resident(l, (w, w))],
        out_specs=pl.BlockSpec((None, bs, w), lambda b, i: (b, i, 0)),
        out_shape=jax.ShapeDtypeStruct((batch, seq, w), BF16),
        scratch_shapes=[pltpu.VMEM((bs + CONV_HALO, w), F32), pltpu.VMEM((bs, w), BF16)],
        compiler_params=_params("parallel", "arbitrary"),
        name="conv_branch",
    )(conv_in.reshape(batch, seq, 3 * w), conv_w, conv_b, norm_g, norm_b, w_pw2)


SSM_BOUND = SSM_CHUNK // SSM_SUB
assert SSM_BOUND % 8 == 0


def _ssm_kernel(in_ref, bp_ref, cp_ref, g_ref, sc_ref, d_ref, wglu_ref, o_ref, carry, ubuf, ybuf, hbuf):
    i = pl.program_id(1)
    n = NSTATE
    w = BRANCH_W
    m = SSM_BOUND

    @pl.when(i == 0)
    def _():
        carry[...] = jnp.zeros_like(carry)

    nt = w // LANES
    for k in range(nt):
        ubuf[k] = in_ref[:, k * LANES:(k + 1) * LANES]

    x = None
    for p in range(SSM_SUB):
        up = jnp.concatenate([ubuf[k, pl.ds(p, m, stride=SSM_SUB), :] for k in range(nt)], axis=1).astype(BF16)
        t = _dot(up, bp_ref[p])
        x = t if x is None else x + t
    tile = 8
    first = lax.broadcasted_iota(jnp.int32, (tile, n), 0) == 0
    cr = carry[0:1, :]
    ci = carry[1:2, :]
    for k in range(m // tile):
        tr = x[k * tile:(k + 1) * tile, 0:n]
        ti = x[k * tile:(k + 1) * tile, n:2 * n]
        for lvl in range(3):
            s = 1 << lvl
            ar = sc_ref[lvl, :, 0:n]
            ai = sc_ref[lvl, :, n:2 * n]
            sr = pltpu.roll(tr, s, 0)
            si = pltpu.roll(ti, s, 0)
            tr, ti = tr + (ar * sr - ai * si), ti + (ar * si + ai * sr)
        pr = sc_ref[3, :, 0:n]
        pi = sc_ref[3, :, n:2 * n]
        tr, ti = tr + (pr * cr - pi * ci), ti + (pr * ci + pi * cr)
        hbuf[k * tile:(k + 1) * tile, 0:n] = jnp.where(first, cr, pltpu.roll(tr, 1, 0))
        hbuf[k * tile:(k + 1) * tile, n:2 * n] = jnp.where(first, ci, pltpu.roll(ti, 1, 0))
        cr = tr[tile - 1:tile, :]
        ci = ti[tile - 1:tile, :]
    carry[0:1, :] = cr
    carry[1:2, :] = ci
    hcat = hbuf[...].astype(BF16)
    for p in range(SSM_SUB):
        yp = _dot(hcat, cp_ref[p])
        for k in range(nt):
            ybuf[k, pl.ds(p, m, stride=SSM_SUB), :] = yp[:, k * LANES:(k + 1) * LANES]

    rb = SSM_ROWS
    pos = lax.broadcasted_iota(jnp.int32, (rb, w), 0) % SSM_SUB
    for r0 in range(0, SSM_CHUNK, rb):
        u = in_ref[r0:r0 + rb, 0:w]
        yb = jnp.concatenate([ybuf[k, r0:r0 + rb, :] for k in range(nt)], axis=1)
        y = yb + _dot(u.astype(BF16), g_ref[0])
        for d in range(1, SSM_SUB):
            ud = jnp.where(pos >= d, pltpu.roll(u, d, 0), 0.0)
            y = y + _dot(ud.astype(BF16), g_ref[d])
        y = jax.nn.gelu(y + d_ref[...] * u)
        g = _dot(y.astype(BF16), wglu_ref[...])
        out = g[:, 0:w] * _sigmoid(g[:, w:2 * w]) * _silu(in_ref[r0:r0 + rb, w:2 * w])
        o_ref[r0:r0 + rb, :] = out.astype(o_ref.dtype)


def _ssm_branch(l, ssm_in, bp, cp, gd, powers, d_skip, w_glu, batch, seq):
    rows = SSM_CHUNK
    w = BRANCH_W
    return pl.pallas_call(
        _ssm_kernel,
        grid=(batch, seq // rows),
        in_specs=[pl.BlockSpec((None, rows, 2 * w), lambda b, i: (b, i, 0)),
                  _resident(l, (SSM_SUB, w, 2 * NSTATE)), _resident(l, (SSM_SUB, 2 * NSTATE, w)),
                  _resident(l, (SSM_SUB, w, w)),
                  _resident(l, (4, 8, 2 * NSTATE)), _resident(l, (1, w)), _resident(l, (w, 2 * w))],
        out_specs=pl.BlockSpec((None, rows, w), lambda b, i: (b, i, 0)),
        out_shape=jax.ShapeDtypeStruct((batch, seq, w), BF16),
        scratch_shapes=[pltpu.VMEM((8, NSTATE), F32), pltpu.VMEM((w // LANES, rows, LANES), F32),
                        pltpu.VMEM((w // LANES, rows, LANES), F32), pltpu.VMEM((SSM_BOUND, 2 * NSTATE), F32)],
        compiler_params=_params("parallel", "arbitrary"),
        name="ssm_branch",
    )(ssm_in.reshape(batch, seq, 2 * w), bp, cp, gd, powers, d_skip, w_glu)


SWA_GROUP = SWA_HEADS // SWA_KV_HEADS
SWA_BLOCKS = SEQ_BLOCK // WINDOW


def _swa_kernel(l, sink_ref, cur_ref, prev_ref, dz_ref, o_ref, kvbuf, ybuf):
    i = pl.program_id(1)
    hd = SWA_HEAD_DIM
    kv_w = SWA_KV_HEADS * hd
    q_w = SWA_HEADS * hd
    win = WINDOW
    scale = hd ** -0.5

    kvbuf[0:win, :] = prev_ref[:, q_w:q_w + 2 * kv_w]
    kvbuf[win:, :] = cur_ref[:, q_w:q_w + 2 * kv_w]

    rows = SWA_GROUP * win
    qrow = lax.broadcasted_iota(jnp.int32, (rows, 2 * win), 0) % win + win
    kcol = lax.broadcasted_iota(jnp.int32, (rows, 2 * win), 1)
    rel = qrow - kcol
    band = (rel >= 0) & (rel < win)
    head_row = lax.broadcasted_iota(jnp.int32, (rows, 1), 0) // win

    for j in range(SWA_BLOCKS):
        first_block = (i * SWA_BLOCKS + j) == 0
        valid = band & (jnp.logical_not(first_block) | (kcol >= win))
        for kvh in range(SWA_KV_HEADS):
            q2 = jnp.concatenate(
                [cur_ref[j * win:(j + 1) * win, (kvh * SWA_GROUP + g) * hd:(kvh * SWA_GROUP + g + 1) * hd]
                 for g in range(SWA_GROUP)], axis=0)
            k2 = kvbuf[j * win:(j + 2) * win, kvh * hd:(kvh + 1) * hd]
            v2 = kvbuf[j * win:(j + 2) * win, kv_w + kvh * hd:kv_w + (kvh + 1) * hd]
            s = jnp.where(valid, _dot_nt(q2, k2) * scale, NEG_BIG)
            sink = jnp.where(head_row == 0, sink_ref[l, kvh * SWA_GROUP], sink_ref[l, kvh * SWA_GROUP + 1])
            m = jnp.maximum(jnp.max(s, axis=-1, keepdims=True), sink)
            p = jnp.exp(s - m)
            denom = jnp.sum(p, axis=-1, keepdims=True) + jnp.exp(sink - m)
            o = _dot(p.astype(BF16), v2) / denom
            for g in range(SWA_GROUP):
                h = kvh * SWA_GROUP + g
                ybuf[j * win:(j + 1) * win, h * hd:(h + 1) * hd] = o[g * win:(g + 1) * win, :]
    o_ref[...] = (ybuf[...] * _silu(dz_ref[...])).astype(o_ref.dtype)


def _swa_branch(l, swa_in, dz, sinks, batch, seq):
    bs = SEQ_BLOCK
    w = BRANCH_W
    per = SWA_BLOCKS
    return pl.pallas_call(
        functools.partial(_swa_kernel, l),
        grid=(batch, seq // bs),
        in_specs=[pl.BlockSpec(memory_space=pltpu.SMEM),
                  pl.BlockSpec((None, bs, 2 * w), lambda b, i: (b, i, 0)),
                  pl.BlockSpec((None, WINDOW, 2 * w), lambda b, i: (b, jnp.maximum(i * per - 1, 0), 0)),
                  pl.BlockSpec((None, bs, w), lambda b, i: (b, i, 0))],
        out_specs=pl.BlockSpec((None, bs, w), lambda b, i: (b, i, 0)),
        out_shape=jax.ShapeDtypeStruct((batch, seq, w), BF16),
        scratch_shapes=[pltpu.VMEM((bs + WINDOW, 2 * SWA_KV_HEADS * SWA_HEAD_DIM), BF16),
                        pltpu.VMEM((bs, w), F32)],
        compiler_params=_params("parallel", "arbitrary"),
        name="swa_branch",
    )(sinks, swa_in.reshape(batch, seq, 2 * w), swa_in.reshape(batch, seq, 2 * w), dz.reshape(batch, seq, w))


MLA_QK = MLA_HEADS * MLA_SLOT


def _rms(x, g):
    ms = jnp.mean(x * x, axis=-1, keepdims=True)
    return x * lax.rsqrt(ms + RMS_EPS) * g


def _mla_prep_kernel(in_ref, cos_ref, sin_ref, qg_ref, kvg_ref, wuq_ref, wukv_ref, qt_ref, k_ref, vt_ref):
    scale = (MLA_NOPE + MLA_ROPE) ** -0.5 * LOG2E
    cq = _rms(in_ref[:, 0:MLA_Q_RANK], qg_ref[...]).astype(BF16)
    ckv = _rms(in_ref[:, MLA_Q_RANK:MLA_Q_RANK + MLA_KV_RANK], kvg_ref[...]).astype(BF16)
    off = MLA_Q_RANK + MLA_KV_RANK
    cos = cos_ref[...]
    sin = sin_ref[...]
    k_rope = in_ref[:, off:off + LANES] * cos + in_ref[:, off + LANES:off + 2 * LANES] * sin
    kv = _dot(ckv, wukv_ref[...])
    for h in range(MLA_HEADS):
        lo = h * MLA_SLOT
        qm = _dot(cq, wuq_ref[:, lo:lo + MLA_SLOT])
        qr = _dot(cq, wuq_ref[:, MLA_QK + lo:MLA_QK + lo + MLA_SLOT])
        qt_ref[lo:lo + MLA_SLOT, :] = ((qm * cos + qr * sin) * scale).T.astype(BF16)
        k_ref[:, lo:lo + MLA_SLOT] = (kv[:, lo:lo + MLA_SLOT] + k_rope).astype(BF16)
    vt = kv[:, MLA_QK:].T.astype(BF16)
    ones = jnp.ones((MLA_VT_ROWS - MLA_V, vt.shape[1]), BF16)
    for h in range(MLA_HEADS):
        vt_ref[h * MLA_VT_ROWS:h * MLA_VT_ROWS + MLA_V, :] = vt[h * MLA_V:(h + 1) * MLA_V, :]
        vt_ref[h * MLA_VT_ROWS + MLA_V:(h + 1) * MLA_VT_ROWS, :] = ones


def _mla_prep(l, mla_in, cos_t, sin_t, q_norm_g, kv_norm_g, w_uq_ext, w_ukv_ext, seq):
    t = mla_in.shape[0]
    tm = ATTN_BLK
    nsb = seq // tm
    vw = MLA_HEADS * MLA_V
    vt_rows = MLA_HEADS * MLA_VT_ROWS
    return pl.pallas_call(
        _mla_prep_kernel,
        grid=(t // tm,),
        in_specs=[pl.BlockSpec((tm, mla_in.shape[1]), lambda i: (i, 0)),
                  pl.BlockSpec((tm, LANES), lambda i: (i % nsb, 0)),
                  pl.BlockSpec((tm, LANES), lambda i: (i % nsb, 0)),
                  _resident(l, (1, MLA_Q_RANK)), _resident(l, (1, MLA_KV_RANK)),
                  _resident(l, (MLA_Q_RANK, 2 * MLA_QK)), _resident(l, (MLA_KV_RANK, MLA_QK + vw))],
        out_specs=[pl.BlockSpec((None, MLA_QK, tm), lambda i: (i, 0, 0)),
                   pl.BlockSpec((tm, MLA_QK), lambda i: (i, 0)),
                   pl.BlockSpec((None, vt_rows, tm), lambda i: (i, 0, 0))],
        out_shape=[jax.ShapeDtypeStruct((t // tm, MLA_QK, tm), BF16), jax.ShapeDtypeStruct((t, MLA_QK), BF16),
                   jax.ShapeDtypeStruct((t // tm, vt_rows, tm), BF16)],
        compiler_params=_params("parallel"),
        name="mla_prep",
    )(mla_in, cos_t, sin_t, q_norm_g, kv_norm_g, w_uq_ext, w_ukv_ext)


def _attn_kernel(qt_ref, k_ref, vt_ref, bz_ref, o_ref, m_sc, acc_sc, s_sc):
    i = pl.program_id(1)
    blk = ATTN_BLK
    krow = lax.broadcasted_iota(jnp.int32, (blk, blk), 0)
    qcol = lax.broadcasted_iota(jnp.int32, (blk, blk), 1)
    causal = krow <= qcol
    m_sc[...] = jnp.full(m_sc.shape, NEG_BIG, F32)
    acc_sc[...] = jnp.zeros(acc_sc.shape, F32)

    def score(j, slot):
        start = j * blk if isinstance(j, int) else pl.multiple_of(j * blk, blk)
        for h in range(MLA_HEADS):
            kb = k_ref[pl.ds(start, blk), h * MLA_SLOT:(h + 1) * MLA_SLOT]
            s_sc[slot, h] = _dot(kb, qt_ref[h * MLA_SLOT:(h + 1) * MLA_SLOT, :])

    def absorb(j, slot, masked):
        for h in range(MLA_HEADS):
            s = s_sc[slot, h]
            if masked:
                s = jnp.where(causal, s, NEG_BIG)
            m_prev = m_sc[h]
            m_new = jnp.maximum(m_prev, jnp.max(s, axis=0, keepdims=True))
            alpha = jnp.exp2(m_prev - m_new)
            p = jnp.exp2(s - m_new).astype(BF16)
            vb = vt_ref[j, h * MLA_VT_ROWS:(h + 1) * MLA_VT_ROWS, :]
            acc_sc[h] = alpha * acc_sc[h] + _dot(vb, p)
            m_sc[h] = m_new

    score(0, 0)

    def body(t, c):
        j = 2 * t
        score(j + 1, 1)
        absorb(j, 0, False)
        score(j + 2, 0)
        absorb(j + 1, 1, False)
        return c

    lax.fori_loop(0, i // 2, body, 0)

    @pl.when(i % 2 == 0)
    def _():
        absorb(i, 0, True)

    @pl.when(i % 2 == 1)
    def _():
        score(i, 1)
        absorb(i - 1, 0, False)
        absorb(i, 1, True)
    outs = [acc_sc[h, 0:MLA_V, :] * (1.0 / acc_sc[h, MLA_V:MLA_V + 1, :]) for h in range(MLA_HEADS)]
    y = jnp.concatenate(outs, axis=0).T
    o_ref[...] = (y * _silu(bz_ref[...])).astype(o_ref.dtype)


def _mla_attention(qt, k, vt, bz, batch, seq):
    blk = ATTN_BLK
    nq = seq // blk
    vt_rows = MLA_HEADS * MLA_VT_ROWS
    return pl.pallas_call(
        _attn_kernel,
        grid=(batch, nq),
        in_specs=[pl.BlockSpec((None, MLA_QK, blk), lambda b, i: (b * nq + i, 0, 0)),
                  pl.BlockSpec((None, seq, MLA_QK), lambda b, i: (b, 0, 0)),
                  pl.BlockSpec((None, nq, vt_rows, blk), lambda b, i: (b, 0, 0, 0)),
                  pl.BlockSpec((None, blk, BRANCH_W), lambda b, i: (b, i, 0))],
        out_specs=pl.BlockSpec((None, blk, BRANCH_W), lambda b, i: (b, i, 0)),
        out_shape=jax.ShapeDtypeStruct((batch, seq, BRANCH_W), BF16),
        scratch_shapes=[pltpu.VMEM((MLA_HEADS, 1, blk), F32), pltpu.VMEM((MLA_HEADS, MLA_VT_ROWS, blk), F32),
                        pltpu.VMEM((2, MLA_HEADS, blk, blk), F32)],
        compiler_params=_params("parallel", "arbitrary"),
        name="mla_attention",
    )(qt, k.reshape(batch, seq, MLA_QK), vt.reshape(batch, nq, vt_rows, blk), bz.reshape(batch, seq, BRANCH_W))


def _merge_kernel(x_ref, ya_ref, yb_ref, yc_ref, yd_ref, p_ref, wm_ref, bm_ref, wb_ref, wo_ref,
                  lng_ref, lnb_ref, wp_ref, wpg_ref, pg_ref, o_ref):
    d = D_MODEL
    x = x_ref[...]
    xb = x.astype(BF16)
    merged = None
    for n, y_ref in enumerate((ya_ref, yb_ref, yc_ref, yd_ref)):
        gate = _sigmoid(_dot(xb, wm_ref[:, n * d:(n + 1) * d]) + bm_ref[:, n * d:(n + 1) * d])
        term = gate * _dot(y_ref[...], wb_ref[n])
        merged = term if merged is None else merged + term
    z = DEEPNORM_ALPHA * x + _dot(merged.astype(BF16), wo_ref[...])
    mu = jnp.mean(z, axis=-1, keepdims=True)
    zc = z - mu
    var = jnp.mean(zc * zc, axis=-1, keepdims=True)
    xn = zc * lax.rsqrt(var + LN_EPS) * lng_ref[...] + lnb_ref[...]
    e = _dot(p_ref[...].astype(BF16), wp_ref[...]) * _sigmoid(_dot(xn.astype(BF16), wpg_ref[...]))
    o_ref[...] = xn + _rms(e, pg_ref[...])


def _merge(l, x2d, ys, p3d, w_merge, b_merge, w_branch, w_out, ln_g, ln_b, w_ple, w_ple_gate, ple_norm_g):
    t = x2d.shape[0]
    tm = MERGE_TM
    d = D_MODEL
    row_spec = lambda width: pl.BlockSpec((tm, width), lambda i: (i, 0))
    return pl.pallas_call(
        _merge_kernel,
        grid=(t // tm,),
        in_specs=[row_spec(d)] + [row_spec(BRANCH_W)] * N_BRANCH + [
                  pl.BlockSpec((None, tm, PLE_DIM), lambda i: (l, i, 0)),
                  _resident(l, (d, N_BRANCH * d)), _resident(l, (1, N_BRANCH * d)),
                  _resident(l, (N_BRANCH, BRANCH_W, d)), _resident(l, (d, d)),
                  _resident(l, (1, d)), _resident(l, (1, d)), _resident(l, (PLE_DIM, d)), _resident(l, (d, d)),
                  _resident(l, (1, d))],
        out_specs=row_spec(d),
        out_shape=jax.ShapeDtypeStruct((t, d), F32),
        compiler_params=_params("parallel"),
        name="merge",
    )(x2d, *ys, p3d, w_merge, b_merge, w_branch, w_out, ln_g, ln_b, w_ple, w_ple_gate, ple_norm_g)


def _rot_cols(w):
    half = w.shape[-1] // 2
    return jnp.concatenate([-w[..., half:], w[..., :half]], axis=-1)


def _zero_cols(w, n):
    return jnp.zeros(w.shape[:-1] + (n,), w.dtype)


def _rope_slot(w):
    return jnp.concatenate([_zero_cols(w, MLA_NOPE), w, _zero_cols(w, MLA_SLOT - MLA_NOPE - MLA_ROPE)], axis=-1)


def _in_weights(w_in):
    o = [0]
    for s in (256, 256, 256, 256, 128, 32, 256, 256, 256, 256, 128, 128, 256):
        o.append(o[-1] + s)
    col = lambda k: w_in[..., o[k]:o[k + 1]]
    a_val, a_gate, a_z, c_q, c_kv, k_r, b_z, u, c_z, q, k, v, d_z = (col(k) for k in range(13))
    groups = [a_val, a_gate, a_z, c_q, c_kv, _rope_slot(k_r), _rope_slot(_rot_cols(k_r)), b_z, u, c_z, q, k, v, d_z]
    return jnp.concatenate(groups, axis=-1).astype(BF16)


def _uq_weights(w_uq):
    hd = MLA_NOPE + MLA_ROPE
    main, rot = [], []
    for h in range(MLA_HEADS):
        head = w_uq[..., h * hd:(h + 1) * hd]
        main.append(jnp.concatenate([head, _zero_cols(w_uq, MLA_SLOT - hd)], axis=-1))
        rot.append(_rope_slot(_rot_cols(head[..., MLA_NOPE:])))
    return jnp.concatenate(main + rot, axis=-1).astype(BF16)


def _ukv_weights(w_ukv):
    hd = MLA_NOPE + MLA_V
    ks, vs = [], []
    for h in range(MLA_HEADS):
        ks.append(jnp.concatenate([w_ukv[..., h * hd:h * hd + MLA_NOPE], _zero_cols(w_ukv, MLA_SLOT - MLA_NOPE)],
                                  axis=-1))
        vs.append(w_ukv[..., h * hd + MLA_NOPE:(h + 1) * hd])
    return jnp.concatenate(ks + vs, axis=-1).astype(BF16)


def _rope_tables(seq):
    pos = jnp.arange(seq, dtype=F32)
    inv_freq = ROPE_THETA ** (-jnp.arange(0, MLA_ROPE, 2, dtype=F32) / MLA_ROPE)
    ang = pos[:, None] * inv_freq[None, :]
    cos, sin = jnp.cos(ang), jnp.sin(ang)
    ones = jnp.ones((seq, MLA_NOPE), F32)
    zeros = jnp.zeros((seq, MLA_NOPE), F32)
    pad = jnp.zeros((seq, MLA_SLOT - MLA_NOPE - MLA_ROPE), F32)
    cos_t = jnp.concatenate([ones, cos, cos, pad], axis=1)
    sin_t = jnp.concatenate([zeros, sin, sin, pad], axis=1)
    return cos_t, sin_t


def _ssm_weights(a_re, a_im, log_dt, b_re, b_im, c_re, c_im):
    g, p, h = SSM_GROUPS, SSM_STATE, SSM_GROUP
    nl = a_re.shape[0]
    dt = jnp.exp(log_dt.astype(F32))[..., None]
    lr, li = a_re.astype(F32), a_im.astype(F32)
    mag = jnp.exp(lr * dt)
    lb_re, lb_im = mag * jnp.cos(li * dt), mag * jnp.sin(li * dt)
    den = lr * lr + li * li
    nr, ni = lb_re - 1.0, lb_im
    f_re = ((nr * lr + ni * li) / den)[..., None]
    f_im = ((ni * lr - nr * li) / den)[..., None]
    bb_re = (f_re * b_re - f_im * b_im)[:, None]
    bb_im = (f_re * b_im + f_im * b_re)[:, None]
    c_re, c_im = c_re.astype(F32)[:, None], c_im.astype(F32)[:, None]
    prs, pis = [jnp.ones_like(lb_re)], [jnp.zeros_like(lb_im)]
    for _ in range(SSM_SUB):
        qr, qi = prs[-1], pis[-1]
        prs.append(qr * lb_re - qi * lb_im)
        pis.append(qr * lb_im + qi * lb_re)
    pw_r, pw_i = jnp.stack(prs, axis=1), jnp.stack(pis, axis=1)
    eye = jnp.eye(g, dtype=F32)[:, None, :, None]
    place = lambda m, rows, cols: (m[..., None, :] * eye).reshape(nl, SSM_SUB, rows, cols)
    qr, qi = pw_r[:, :SSM_SUB, :, :, None], pw_i[:, :SSM_SUB, :, :, None]
    vr = jnp.swapaxes(qr * bb_re - qi * bb_im, -1, -2)
    vi = jnp.swapaxes(qr * bb_im + qi * bb_re, -1, -2)
    bp = jnp.concatenate([place(vr[:, ::-1], g * h, g * p), place(vi[:, ::-1], g * h, g * p)], axis=-1)
    qr, qi = pw_r[:, 1:, :, None, :], pw_i[:, 1:, :, None, :]
    ur = jnp.swapaxes(c_re * qr - c_im * qi, -1, -2)
    ui = jnp.swapaxes(c_re * qi + c_im * qr, -1, -2)
    cp = jnp.concatenate([place(ur, g * p, g * h), -place(ui, g * p, g * h)], axis=-2)
    lag = jnp.sum(vr[..., None, :] * c_re[:, :, :, None] - vi[..., None, :] * c_im[:, :, :, None], axis=-1)
    gd = place(lag, g * h, g * h)
    mr, mi = pw_r[:, SSM_SUB].reshape(nl, 1, g * p), pw_i[:, SSM_SUB].reshape(nl, 1, g * p)
    mu = [(mr, mi)]
    for _ in range(7):
        qr, qi = mu[-1]
        mu.append((qr * mr - qi * mi, qr * mi + qi * mr))
    cat = lambda q: jnp.concatenate(q, axis=-1)
    rowid = jnp.arange(8)[:, None]
    tables = [jnp.where(rowid >= (1 << lvl), cat(mu[(1 << lvl) - 1]), 0.0) for lvl in range(3)]
    tables.append(jnp.concatenate([cat(q) for q in mu], axis=1))
    return bp.astype(BF16), cp.astype(BF16), gd.astype(BF16), jnp.stack(tables, axis=1)


def kernel(x, p, w_in, w_merge, b_merge, conv_w, conv_b, conv_norm_g, conv_norm_b, w_pw2, mla_q_norm_g,
           mla_kv_norm_g, w_uq, w_ukv, ssm_a_re, ssm_a_im, ssm_log_dt, ssm_b_re, ssm_b_im, ssm_c_re,
           ssm_c_im, ssm_d, w_glu, attn_sinks, w_branch, w_out, ln_g, ln_b, w_ple, w_ple_gate, ple_norm_g):
    batch, seq, d = x.shape
    t = batch * seq
    rows = lambda a: a.reshape(DEPTH, 1, -1).astype(F32)
    bf = lambda a: a.astype(BF16)
    cos_t, sin_t = _rope_tables(seq)
    w_in_ext, w_uq_ext, w_ukv_ext = _in_weights(w_in), _uq_weights(w_uq), _ukv_weights(w_ukv)
    bp, cp, gd, sc = _ssm_weights(ssm_a_re, ssm_a_im, ssm_log_dt, ssm_b_re, ssm_b_im, ssm_c_re, ssm_c_im)
    conv_p = (conv_w.astype(F32), rows(conv_b), rows(conv_norm_g), rows(conv_norm_b), bf(w_pw2))
    mla_p = (rows(mla_q_norm_g), rows(mla_kv_norm_g), w_uq_ext, w_ukv_ext)
    ssm_p = (bp, cp, gd, sc, rows(ssm_d), bf(w_glu))
    merge_p = (bf(w_merge), rows(b_merge), bf(w_branch), bf(w_out), rows(ln_g), rows(ln_b), bf(w_ple),
               bf(w_ple_gate), rows(ple_norm_g))
    sinks = attn_sinks.astype(F32)
    p3d = p.reshape(DEPTH, t, PLE_DIM)
    x2d = x.reshape(t, d)
    for l in range(DEPTH):
        conv_in, mla_in, bz, ssm_in, swa_in, dz = _in_proj(l, x2d, w_in_ext)
        y_a = _conv_branch(l, conv_in, *conv_p, batch, seq)
        qt, k, vt = _mla_prep(l, mla_in, cos_t, sin_t, *mla_p, seq)
        y_b = _mla_attention(qt, k, vt, bz, batch, seq)
        y_c = _ssm_branch(l, ssm_in, *ssm_p, batch, seq)
        y_d = _swa_branch(l, swa_in, dz, sinks, batch, seq)
        ys = [y.reshape(t, BRANCH_W) for y in (y_a, y_b, y_c, y_d)]
        x2d = _merge(l, x2d, ys, p3d, *merge_p)
    return x2d.reshape(batch, seq, d)
```

```python
import functools
import math

import jax
import jax.numpy as jnp
from jax import lax
from jax.experimental import pallas as pl
from jax.experimental.pallas import tpu as pltpu

D_MODEL = 1024
DEPTH = 4
PLE_DIM = 256
N_BRANCH = 4
BRANCH_W = 256
CONV_W = 31
MLA_HEADS = 4
MLA_NOPE = 64
MLA_ROPE = 32
MLA_V = 64
MLA_Q_RANK = 256
MLA_KV_RANK = 128
ROPE_THETA = 10000.0
SSM_GROUP = 16
SSM_GROUPS = 16
SSM_STATE = 64
SWA_HEADS = 4
SWA_KV_HEADS = 2
SWA_HEAD_DIM = 64
WINDOW = 128
DEEPNORM_ALPHA = (2.0 * DEPTH) ** 0.25
LN_EPS = 1e-5
RMS_EPS = 1e-6

LANES = 128
VMEM_LIMIT_BYTES = 56 * 1024 * 1024

IN_TM = 512
SEQ_BLOCK = 512
SSM_CHUNK = 2048
SSM_SUB = 8
SSM_ROWS = 512
ATTN_BLK = 512
MERGE_TM = 512
CONV_HALO = 32
CONV_ROWS = 64

MLA_SLOT = LANES
MLA_VT_ROWS = MLA_V + 16
LOG2E = math.log2(math.e)
NSTATE = SSM_GROUPS * SSM_STATE
NEG_BIG = -1e30

F32 = jnp.float32
BF16 = jnp.bfloat16


def _params(*sem):
    return pltpu.CompilerParams(dimension_semantics=sem, vmem_limit_bytes=VMEM_LIMIT_BYTES)


def _resident(l, shape):
    nd = len(shape)
    return pl.BlockSpec((None,) + tuple(shape), lambda *_: (l,) + (0,) * nd, pipeline_mode=pl.Buffered(1))


def _dot(a, b):
    return jnp.dot(a, b, preferred_element_type=F32)


def _dot_nt(a, b):
    return lax.dot_general(a, b, (((1,), (1,)), ((), ())), preferred_element_type=F32)


def _sigmoid(x):
    return 1.0 / (1.0 + jnp.exp(-x))


def _silu(x):
    return x * _sigmoid(x)


IN_GROUPS = (
    ("conv", 3 * BRANCH_W, F32),
    ("mla", MLA_Q_RANK + MLA_KV_RANK + 2 * LANES, F32),
    ("bz", BRANCH_W, F32),
    ("ssm", 2 * BRANCH_W, F32),
    ("swa", 2 * BRANCH_W, BF16),
    ("dz", BRANCH_W, F32),
)
IN_EXT_WIDTH = sum(w for _, w, _ in IN_GROUPS)


def _in_proj_kernel(x_ref, w_ref, *out_refs):
    xb = x_ref[...].astype(BF16)
    start = 0
    for (_, width, dtype), o_ref in zip(IN_GROUPS, out_refs):
        o_ref[...] = _dot(xb, w_ref[:, start:start + width]).astype(dtype)
        start += width


def _in_proj(l, x2d, w_in_ext):
    t = x2d.shape[0]
    return pl.pallas_call(
        _in_proj_kernel,
        grid=(t // IN_TM,),
        in_specs=[pl.BlockSpec((IN_TM, D_MODEL), lambda i: (i, 0)),
                  _resident(l, (D_MODEL, IN_EXT_WIDTH))],
        out_specs=[pl.BlockSpec((IN_TM, w), lambda i: (i, 0)) for _, w, _ in IN_GROUPS],
        out_shape=[jax.ShapeDtypeStruct((t, w), d) for _, w, d in IN_GROUPS],
        compiler_params=_params("parallel"),
        name="in_proj",
    )(x2d, w_in_ext)


def _conv_kernel(h_ref, cw_ref, cb_ref, ng_ref, nb_ref, pw2_ref, o_ref, hbuf, ybuf):
    i = pl.program_id(1)
    bs = SEQ_BLOCK
    w = BRANCH_W

    @pl.when(i == 0)
    def _():
        hbuf[0:CONV_HALO, :] = jnp.zeros((CONV_HALO, w), F32)

    @pl.when(i > 0)
    def _():
        hbuf[0:CONV_HALO, :] = hbuf[bs:bs + CONV_HALO, :]

    hbuf[CONV_HALO:, :] = h_ref[:, 0:w] * _sigmoid(h_ref[:, w:2 * w])

    base = CONV_HALO - (CONV_W - 1)
    tile = 8
    for c in range(bs // CONV_ROWS):
        r0 = c * CONV_ROWS
        acc = jnp.broadcast_to(cb_ref[...], (CONV_ROWS, w))
        for r in range(tile):
            span = CONV_ROWS + (tile if r else 0)
            part = None
            for j in range((r - base) % tile, CONV_W, tile):
                lo = r0 + base + j - r
                term = cw_ref[j:j + 1, :] * hbuf[lo:lo + span, :]
                part = term if part is None else part + term
            acc = acc + part[r:r + CONV_ROWS, :]
        mu = jnp.mean(acc, axis=-1, keepdims=True)
        d = acc - mu
        var = jnp.mean(d * d, axis=-1, keepdims=True)
        y = d * lax.rsqrt(var + LN_EPS) * ng_ref[...] + nb_ref[...]
        ybuf[r0:r0 + CONV_ROWS, :] = _silu(y).astype(BF16)
    o_ref[...] = (_dot(ybuf[...], pw2_ref[...]) * _silu(h_ref[:, 2 * w:3 * w])).astype(o_ref.dtype)


def _conv_branch(l, conv_in, conv_w, conv_b, norm_g, norm_b, w_pw2, batch, seq):
    bs = SEQ_BLOCK
    w = BRANCH_W
    return pl.pallas_call(
        _conv_kernel,
        grid=(batch, seq // bs),
        in_specs=[pl.BlockSpec((None, bs, 3 * w), lambda b, i: (b, i, 0)),
                  _resident(l, (CONV_W, w)), _resident(l, (1, w)), _resident(l, (1, w)), _resident(l, (1, w)),
                  _resident(l, (w, w))],
        out_specs=pl.BlockSpec((None, bs, w), lambda b, i: (b, i, 0)),
        out_shape=jax.ShapeDtypeStruct((batch, seq, w), BF16),
        scratch_shapes=[pltpu.VMEM((bs + CONV_HALO, w), F32), pltpu.VMEM((bs, w), BF16)],
        compiler_params=_params("parallel", "arbitrary"),
        name="conv_branch",
    )(conv_in.reshape(batch, seq, 3 * w), conv_w, conv_b, norm_g, norm_b, w_pw2)


SSM_BOUND = SSM_CHUNK // SSM_SUB
assert SSM_BOUND % 8 == 0


def _ssm_kernel(in_ref, bc_ref, cc_ref, gc_ref, sc_ref, d_ref, wglu_ref, o_ref,
                carry, ubuf, ybuf, hbuf, bp_ref, cpt_ref, g_ref):
    i = pl.program_id(1)
    n = NSTATE
    w = BRANCH_W
    m = SSM_BOUND

    @pl.when((pl.program_id(0) == 0) & (i == 0))
    def _():
        shift = lambda a, k: lax.shift_right_logical(a, jnp.full(a.shape, k, jnp.int32))
        row_g = shift(lax.broadcasted_iota(jnp.int32, (w, n), 0), int(math.log2(SSM_GROUP)))
        col_g = shift(lax.broadcasted_iota(jnp.int32, (w, n), 1), int(math.log2(SSM_STATE)))
        same_group = row_g == col_g
        in_g = shift(lax.broadcasted_iota(jnp.int32, (w, w), 0), int(math.log2(SSM_GROUP)))
        out_g = shift(lax.broadcasted_iota(jnp.int32, (w, w), 1), int(math.log2(SSM_GROUP)))
        same_group_out = in_g == out_g
        for k in range(SSM_SUB):
            for c in range(2):
                for src, dst in ((bc_ref, bp_ref), (cc_ref, cpt_ref)):
                    wide = jnp.concatenate([src[k, c]] * (n // LANES), axis=1)
                    dst[k, :, c * n:(c + 1) * n] = jnp.where(same_group, wide, 0.0).astype(BF16)
            wide = jnp.concatenate([gc_ref[k]] * (w // LANES), axis=1)
            g_ref[k] = jnp.where(same_group_out, wide, 0.0).astype(BF16)

    @pl.when(i == 0)
    def _():
        carry[...] = jnp.zeros_like(carry)

    nt = w // LANES
    for k in range(nt):
        ubuf[k] = in_ref[:, k * LANES:(k + 1) * LANES]

    x = None
    for p in range(SSM_SUB):
        up = jnp.concatenate([ubuf[k, pl.ds(p, m, stride=SSM_SUB), :] for k in range(nt)], axis=1).astype(BF16)
        t = _dot(up, bp_ref[p])
        x = t if x is None else x + t
    tile = 8
    first = lax.broadcasted_iota(jnp.int32, (tile, n), 0) == 0
    cr = carry[0:1, :]
    ci = carry[1:2, :]
    for k in range(m // tile):
        tr = x[k * tile:(k + 1) * tile, 0:n]
        ti = x[k * tile:(k + 1) * tile, n:2 * n]
        for lvl in range(3):
            s = 1 << lvl
            ar = sc_ref[lvl, :, 0:n]
            ai = sc_ref[lvl, :, n:2 * n]
            sr = pltpu.roll(tr, s, 0)
            si = pltpu.roll(ti, s, 0)
            tr, ti = tr + (ar * sr - ai * si), ti + (ar * si + ai * sr)
        pr = sc_ref[3, :, 0:n]
        pi = sc_ref[3, :, n:2 * n]
        tr, ti = tr + (pr * cr - pi * ci), ti + (pr * ci + pi * cr)
        hbuf[k * tile:(k + 1) * tile, 0:n] = jnp.where(first, cr, pltpu.roll(tr, 1, 0))
        hbuf[k * tile:(k + 1) * tile, n:2 * n] = jnp.where(first, ci, pltpu.roll(ti, 1, 0))
        cr = tr[tile - 1:tile, :]
        ci = ti[tile - 1:tile, :]
    carry[0:1, :] = cr
    carry[1:2, :] = ci
    hcat = hbuf[...].astype(BF16)
    for p in range(SSM_SUB):
        yp = _dot_nt(hcat, cpt_ref[p])
        for k in range(nt):
            ybuf[k, pl.ds(p, m, stride=SSM_SUB), :] = yp[:, k * LANES:(k + 1) * LANES]

    rb = SSM_ROWS
    pos = lax.broadcasted_iota(jnp.int32, (rb, w), 0) % SSM_SUB
    for r0 in range(0, SSM_CHUNK, rb):
        u = in_ref[r0:r0 + rb, 0:w]
        yb = jnp.concatenate([ybuf[k, r0:r0 + rb, :] for k in range(nt)], axis=1)
        y = yb + _dot(u.astype(BF16), g_ref[0])
        for d in range(1, SSM_SUB):
            ud = jnp.where(pos >= d, pltpu.roll(u, d, 0), 0.0)
            y = y + _dot(ud.astype(BF16), g_ref[d])
        y = jax.nn.gelu(y + d_ref[...] * u)
        g = _dot(y.astype(BF16), wglu_ref[...])
        out = g[:, 0:w] * _sigmoid(g[:, w:2 * w]) * _silu(in_ref[r0:r0 + rb, w:2 * w])
        o_ref[r0:r0 + rb, :] = out.astype(o_ref.dtype)


def _ssm_branch(l, ssm_in, bc, cc, gc, powers, d_skip, w_glu, batch, seq):
    rows = SSM_CHUNK
    w = BRANCH_W
    return pl.pallas_call(
        _ssm_kernel,
        grid=(batch, seq // rows),
        in_specs=[pl.BlockSpec((None, rows, 2 * w), lambda b, i: (b, i, 0)),
                  _resident(l, (SSM_SUB, 2, w, LANES)), _resident(l, (SSM_SUB, 2, w, LANES)),
                  _resident(l, (SSM_SUB, w, LANES)),
                  _resident(l, (4, 8, 2 * NSTATE)), _resident(l, (1, w)), _resident(l, (w, 2 * w))],
        out_specs=pl.BlockSpec((None, rows, w), lambda b, i: (b, i, 0)),
        out_shape=jax.ShapeDtypeStruct((batch, seq, w), BF16),
        scratch_shapes=[pltpu.VMEM((8, NSTATE), F32), pltpu.VMEM((w // LANES, rows, LANES), F32),
                        pltpu.VMEM((w // LANES, rows, LANES), F32), pltpu.VMEM((SSM_BOUND, 2 * NSTATE), F32),
                        pltpu.VMEM((SSM_SUB, w, 2 * NSTATE), BF16), pltpu.VMEM((SSM_SUB, w, 2 * NSTATE), BF16),
                        pltpu.VMEM((SSM_SUB, w, w), BF16)],
        compiler_params=_params("arbitrary", "arbitrary"),
        name="ssm_branch",
    )(ssm_in.reshape(batch, seq, 2 * w), bc, cc, gc, powers, d_skip, w_glu)


SWA_GROUP = SWA_HEADS // SWA_KV_HEADS
SWA_BLOCKS = SEQ_BLOCK // WINDOW


def _swa_kernel(l, sink_ref, cur_ref, prev_ref, dz_ref, o_ref, kvbuf, ybuf):
    i = pl.program_id(1)
    hd = SWA_HEAD_DIM
    kv_w = SWA_KV_HEADS * hd
    q_w = SWA_HEADS * hd
    win = WINDOW
    scale = hd ** -0.5

    kvbuf[0:win, :] = prev_ref[:, q_w:q_w + 2 * kv_w]
    kvbuf[win:, :] = cur_ref[:, q_w:q_w + 2 * kv_w]

    rows = SWA_GROUP * win
    qrow = lax.broadcasted_iota(jnp.int32, (rows, 2 * win), 0) % win + win
    kcol = lax.broadcasted_iota(jnp.int32, (rows, 2 * win), 1)
    rel = qrow - kcol
    band = (rel >= 0) & (rel < win)
    head_row = lax.broadcasted_iota(jnp.int32, (rows, 1), 0) // win

    for j in range(SWA_BLOCKS):
        first_block = (i * SWA_BLOCKS + j) == 0
        valid = band & (jnp.logical_not(first_block) | (kcol >= win))
        for kvh in range(SWA_KV_HEADS):
            q2 = jnp.concatenate(
                [cur_ref[j * win:(j + 1) * win, (kvh * SWA_GROUP + g) * hd:(kvh * SWA_GROUP + g + 1) * hd]
                 for g in range(SWA_GROUP)], axis=0)
            k2 = kvbuf[j * win:(j + 2) * win, kvh * hd:(kvh + 1) * hd]
            v2 = kvbuf[j * win:(j + 2) * win, kv_w + kvh * hd:kv_w + (kvh + 1) * hd]
            s = jnp.where(valid, _dot_nt(q2, k2) * scale, NEG_BIG)
            sink = jnp.where(head_row == 0, sink_ref[l, kvh * SWA_GROUP], sink_ref[l, kvh * SWA_GROUP + 1])
            m = jnp.maximum(jnp.max(s, axis=-1, keepdims=True), sink)
            p = jnp.exp(s - m)
            denom = jnp.sum(p, axis=-1, keepdims=True) + jnp.exp(sink - m)
            o = _dot(p.astype(BF16), v2) / denom
            for g in range(SWA_GROUP):
                h = kvh * SWA_GROUP + g
                ybuf[j * win:(j + 1) * win, h * hd:(h + 1) * hd] = o[g * win:(g + 1) * win, :]
    o_ref[...] = (ybuf[...] * _silu(dz_ref[...])).astype(o_ref.dtype)


def _swa_branch(l, swa_in, dz, sinks, batch, seq):
    bs = SEQ_BLOCK
    w = BRANCH_W
    per = SWA_BLOCKS
    return pl.pallas_call(
        functools.partial(_swa_kernel, l),
        grid=(batch, seq // bs),
        in_specs=[pl.BlockSpec(memory_space=pltpu.SMEM),
                  pl.BlockSpec((None, bs, 2 * w), lambda b, i: (b, i, 0)),
                  pl.BlockSpec((None, WINDOW, 2 * w), lambda b, i: (b, jnp.maximum(i * per - 1, 0), 0)),
                  pl.BlockSpec((None, bs, w), lambda b, i: (b, i, 0))],
        out_specs=pl.BlockSpec((None, bs, w), lambda b, i: (b, i, 0)),
        out_shape=jax.ShapeDtypeStruct((batch, seq, w), BF16),
        scratch_shapes=[pltpu.VMEM((bs + WINDOW, 2 * SWA_KV_HEADS * SWA_HEAD_DIM), BF16),
                        pltpu.VMEM((bs, w), F32)],
        compiler_params=_params("parallel", "arbitrary"),
        name="swa_branch",
    )(sinks, swa_in.reshape(batch, seq, 2 * w), swa_in.reshape(batch, seq, 2 * w), dz.reshape(batch, seq, w))


MLA_QK = MLA_HEADS * MLA_SLOT


def _rms(x, g):
    ms = jnp.mean(x * x, axis=-1, keepdims=True)
    return x * lax.rsqrt(ms + RMS_EPS) * g


def _mla_prep_kernel(in_ref, cos_ref, sin_ref, qg_ref, kvg_ref, wuq_ref, wukv_ref, qt_ref, k_ref, vt_ref):
    scale = (MLA_NOPE + MLA_ROPE) ** -0.5 * LOG2E
    cq = _rms(in_ref[:, 0:MLA_Q_RANK], qg_ref[...]).astype(BF16)
    ckv = _rms(in_ref[:, MLA_Q_RANK:MLA_Q_RANK + MLA_KV_RANK], kvg_ref[...]).astype(BF16)
    off = MLA_Q_RANK + MLA_KV_RANK
    cos = cos_ref[...]
    sin = sin_ref[...]
    k_rope = in_ref[:, off:off + LANES] * cos + in_ref[:, off + LANES:off + 2 * LANES] * sin
    kv = _dot(ckv, wukv_ref[...])
    for h in range(MLA_HEADS):
        lo = h * MLA_SLOT
        qm = _dot(cq, wuq_ref[:, lo:lo + MLA_SLOT])
        qr = _dot(cq, wuq_ref[:, MLA_QK + lo:MLA_QK + lo + MLA_SLOT])
        qt_ref[lo:lo + MLA_SLOT, :] = ((qm * cos + qr * sin) * scale).T.astype(BF16)
        k_ref[:, lo:lo + MLA_SLOT] = (kv[:, lo:lo + MLA_SLOT] + k_rope).astype(BF16)
    vt = kv[:, MLA_QK:].T.astype(BF16)
    ones = jnp.ones((MLA_VT_ROWS - MLA_V, vt.shape[1]), BF16)
    for h in range(MLA_HEADS):
        vt_ref[h * MLA_VT_ROWS:h * MLA_VT_ROWS + MLA_V, :] = vt[h * MLA_V:(h + 1) * MLA_V, :]
        vt_ref[h * MLA_VT_ROWS + MLA_V:(h + 1) * MLA_VT_ROWS, :] = ones


def _mla_prep(l, mla_in, cos_t, sin_t, q_norm_g, kv_norm_g, w_uq_ext, w_ukv_ext, seq):
    t = mla_in.shape[0]
    tm = ATTN_BLK
    nsb = seq // tm
    vw = MLA_HEADS * MLA_V
    vt_rows = MLA_HEADS * MLA_VT_ROWS
    return pl.pallas_call(
        _mla_prep_kernel,
        grid=(t // tm,),
        in_specs=[pl.BlockSpec((tm, mla_in.shape[1]), lambda i: (i, 0)),
                  pl.BlockSpec((tm, LANES), lambda i: (i % nsb, 0)),
                  pl.BlockSpec((tm, LANES), lambda i: (i % nsb, 0)),
                  _resident(l, (1, MLA_Q_RANK)), _resident(l, (1, MLA_KV_RANK)),
                  _resident(l, (MLA_Q_RANK, 2 * MLA_QK)), _resident(l, (MLA_KV_RANK, MLA_QK + vw))],
        out_specs=[pl.BlockSpec((None, MLA_QK, tm), lambda i: (i, 0, 0)),
                   pl.BlockSpec((tm, MLA_QK), lambda i: (i, 0)),
                   pl.BlockSpec((None, vt_rows, tm), lambda i: (i, 0, 0))],
        out_shape=[jax.ShapeDtypeStruct((t // tm, MLA_QK, tm), BF16), jax.ShapeDtypeStruct((t, MLA_QK), BF16),
                   jax.ShapeDtypeStruct((t // tm, vt_rows, tm), BF16)],
        compiler_params=_params("parallel"),
        name="mla_prep",
    )(mla_in, cos_t, sin_t, q_norm_g, kv_norm_g, w_uq_ext, w_ukv_ext)


def _attn_kernel(qt_ref, k_ref, vt_ref, bz_ref, o_ref, m_sc, acc_sc, s_sc):
    i = pl.program_id(1)
    blk = ATTN_BLK
    krow = lax.broadcasted_iota(jnp.int32, (blk, blk), 0)
    qcol = lax.broadcasted_iota(jnp.int32, (blk, blk), 1)
    causal = krow <= qcol
    m_sc[...] = jnp.full(m_sc.shape, NEG_BIG, F32)
    acc_sc[...] = jnp.zeros(acc_sc.shape, F32)

    def score(j, slot):
        start = j * blk if isinstance(j, int) else pl.multiple_of(j * blk, blk)
        for h in range(MLA_HEADS):
            kb = k_ref[pl.ds(start, blk), h * MLA_SLOT:(h + 1) * MLA_SLOT]
            s_sc[slot, h] = _dot(kb, qt_ref[h * MLA_SLOT:(h + 1) * MLA_SLOT, :])

    def absorb(j, slot, masked):
        for h in range(MLA_HEADS):
            s = s_sc[slot, h]
            if masked:
                s = jnp.where(causal, s, NEG_BIG)
            m_prev = m_sc[h]
            m_new = jnp.maximum(m_prev, jnp.max(s, axis=0, keepdims=True))
            alpha = jnp.exp2(m_prev - m_new)
            p = jnp.exp2(s - m_new).astype(BF16)
            vb = vt_ref[j, h * MLA_VT_ROWS:(h + 1) * MLA_VT_ROWS, :]
            acc_sc[h] = alpha * acc_sc[h] + _dot(vb, p)
            m_sc[h] = m_new

    score(0, 0)

    def body(t, c):
        j = 2 * t
        score(j + 1, 1)
        absorb(j, 0, False)
        score(j + 2, 0)
        absorb(j + 1, 1, False)
        return c

    lax.fori_loop(0, i // 2, body, 0)

    @pl.when(i % 2 == 0)
    def _():
        absorb(i, 0, True)

    @pl.when(i % 2 == 1)
    def _():
        score(i, 1)
        absorb(i - 1, 0, False)
        absorb(i, 1, True)
    outs = [acc_sc[h, 0:MLA_V, :] * (1.0 / acc_sc[h, MLA_V:MLA_V + 1, :]) for h in range(MLA_HEADS)]
    y = jnp.concatenate(outs, axis=0).T
    o_ref[...] = (y * _silu(bz_ref[...])).astype(o_ref.dtype)


def _mla_attention(qt, k, vt, bz, batch, seq):
    blk = ATTN_BLK
    nq = seq // blk
    vt_rows = MLA_HEADS * MLA_VT_ROWS
    return pl.pallas_call(
        _attn_kernel,
        grid=(batch, nq),
        in_specs=[pl.BlockSpec((None, MLA_QK, blk), lambda b, i: (b * nq + i, 0, 0)),
                  pl.BlockSpec((None, seq, MLA_QK), lambda b, i: (b, 0, 0)),
                  pl.BlockSpec((None, nq, vt_rows, blk), lambda b, i: (b, 0, 0, 0)),
                  pl.BlockSpec((None, blk, BRANCH_W), lambda b, i: (b, i, 0))],
        out_specs=pl.BlockSpec((None, blk, BRANCH_W), lambda b, i: (b, i, 0)),
        out_shape=jax.ShapeDtypeStruct((batch, seq, BRANCH_W), BF16),
        scratch_shapes=[pltpu.VMEM((MLA_HEADS, 1, blk), F32), pltpu.VMEM((MLA_HEADS, MLA_VT_ROWS, blk), F32),
                        pltpu.VMEM((2, MLA_HEADS, blk, blk), F32)],
        compiler_params=_params("parallel", "arbitrary"),
        name="mla_attention",
    )(qt, k.reshape(batch, seq, MLA_QK), vt.reshape(batch, nq, vt_rows, blk), bz.reshape(batch, seq, BRANCH_W))


def _merge_kernel(x_ref, ya_ref, yb_ref, yc_ref, yd_ref, p_ref, wm_ref, bm_ref, wb_ref, wo_ref,
                  lng_ref, lnb_ref, wp_ref, wpg_ref, pg_ref, o_ref):
    d = D_MODEL
    x = x_ref[...]
    xb = x.astype(BF16)
    merged = None
    for n, y_ref in enumerate((ya_ref, yb_ref, yc_ref, yd_ref)):
        gate = _sigmoid(_dot(xb, wm_ref[:, n * d:(n + 1) * d]) + bm_ref[:, n * d:(n + 1) * d])
        term = gate * _dot(y_ref[...], wb_ref[n])
        merged = term if merged is None else merged + term
    z = DEEPNORM_ALPHA * x + _dot(merged.astype(BF16), wo_ref[...])
    mu = jnp.mean(z, axis=-1, keepdims=True)
    zc = z - mu
    var = jnp.mean(zc * zc, axis=-1, keepdims=True)
    xn = zc * lax.rsqrt(var + LN_EPS) * lng_ref[...] + lnb_ref[...]
    e = _dot(p_ref[...].astype(BF16), wp_ref[...]) * _sigmoid(_dot(xn.astype(BF16), wpg_ref[...]))
    o_ref[...] = xn + _rms(e, pg_ref[...])


def _merge(l, x2d, ys, p3d, w_merge, b_merge, w_branch, w_out, ln_g, ln_b, w_ple, w_ple_gate, ple_norm_g):
    t = x2d.shape[0]
    tm = MERGE_TM
    d = D_MODEL
    row_spec = lambda width: pl.BlockSpec((tm, width), lambda i: (i, 0))
    return pl.pallas_call(
        _merge_kernel,
        grid=(t // tm,),
        in_specs=[row_spec(d)] + [row_spec(BRANCH_W)] * N_BRANCH + [
                  pl.BlockSpec((None, tm, PLE_DIM), lambda i: (l, i, 0)),
                  _resident(l, (d, N_BRANCH * d)), _resident(l, (1, N_BRANCH * d)),
                  _resident(l, (N_BRANCH, BRANCH_W, d)), _resident(l, (d, d)),
                  _resident(l, (1, d)), _resident(l, (1, d)), _resident(l, (PLE_DIM, d)), _resident(l, (d, d)),
                  _resident(l, (1, d))],
        out_specs=row_spec(d),
        out_shape=jax.ShapeDtypeStruct((t, d), F32),
        compiler_params=_params("parallel"),
        name="merge",
    )(x2d, *ys, p3d, w_merge, b_merge, w_branch, w_out, ln_g, ln_b, w_ple, w_ple_gate, ple_norm_g)


def _rot_cols(w):
    half = w.shape[-1] // 2
    return jnp.concatenate([-w[..., half:], w[..., :half]], axis=-1)


def _zero_cols(w, n):
    return jnp.zeros(w.shape[:-1] + (n,), w.dtype)


def _rope_slot(w):
    return jnp.concatenate([_zero_cols(w, MLA_NOPE), w, _zero_cols(w, MLA_SLOT - MLA_NOPE - MLA_ROPE)], axis=-1)


def _in_weights(w_in):
    o = [0]
    for s in (256, 256, 256, 256, 128, 32, 256, 256, 256, 256, 128, 128, 256):
        o.append(o[-1] + s)
    col = lambda k: w_in[..., o[k]:o[k + 1]]
    a_val, a_gate, a_z, c_q, c_kv, k_r, b_z, u, c_z, q, k, v, d_z = (col(k) for k in range(13))
    groups = [a_val, a_gate, a_z, c_q, c_kv, _rope_slot(k_r), _rope_slot(_rot_cols(k_r)), b_z, u, c_z, q, k, v, d_z]
    return jnp.concatenate(groups, axis=-1).astype(BF16)


def _uq_weights(w_uq):
    hd = MLA_NOPE + MLA_ROPE
    main, rot = [], []
    for h in range(MLA_HEADS):
        head = w_uq[..., h * hd:(h + 1) * hd]
        main.append(jnp.concatenate([head, _zero_cols(w_uq, MLA_SLOT - hd)], axis=-1))
        rot.append(_rope_slot(_rot_cols(head[..., MLA_NOPE:])))
    return jnp.concatenate(main + rot, axis=-1).astype(BF16)


def _ukv_weights(w_ukv):
    hd = MLA_NOPE + MLA_V
    ks, vs = [], []
    for h in range(MLA_HEADS):
        ks.append(jnp.concatenate([w_ukv[..., h * hd:h * hd + MLA_NOPE], _zero_cols(w_ukv, MLA_SLOT - MLA_NOPE)],
                                  axis=-1))
        vs.append(w_ukv[..., h * hd + MLA_NOPE:(h + 1) * hd])
    return jnp.concatenate(ks + vs, axis=-1).astype(BF16)


def _rope_tables(seq):
    pos = jnp.arange(seq, dtype=F32)
    inv_freq = ROPE_THETA ** (-jnp.arange(0, MLA_ROPE, 2, dtype=F32) / MLA_ROPE)
    ang = pos[:, None] * inv_freq[None, :]
    cos, sin = jnp.cos(ang), jnp.sin(ang)
    ones = jnp.ones((seq, MLA_NOPE), F32)
    zeros = jnp.zeros((seq, MLA_NOPE), F32)
    pad = jnp.zeros((seq, MLA_SLOT - MLA_NOPE - MLA_ROPE), F32)
    cos_t = jnp.concatenate([ones, cos, cos, pad], axis=1)
    sin_t = jnp.concatenate([zeros, sin, sin, pad], axis=1)
    return cos_t, sin_t


def _ssm_weights(a_re, a_im, log_dt, b_re, b_im, c_re, c_im):
    g, p, h = SSM_GROUPS, SSM_STATE, SSM_GROUP
    nl = a_re.shape[0]
    dt = jnp.exp(log_dt.astype(F32))[..., None]
    lr, li = a_re.astype(F32), a_im.astype(F32)
    mag = jnp.exp(lr * dt)
    lb_re, lb_im = mag * jnp.cos(li * dt), mag * jnp.sin(li * dt)
    den = lr * lr + li * li
    nr, ni = lb_re - 1.0, lb_im
    f_re = ((nr * lr + ni * li) / den)[..., None]
    f_im = ((ni * lr - nr * li) / den)[..., None]
    bb_re = (f_re * b_re - f_im * b_im)[:, None]
    bb_im = (f_re * b_im + f_im * b_re)[:, None]
    c_re, c_im = c_re.astype(F32)[:, None], c_im.astype(F32)[:, None]
    prs, pis = [jnp.ones_like(lb_re)], [jnp.zeros_like(lb_im)]
    for _ in range(SSM_SUB):
        qr, qi = prs[-1], pis[-1]
        prs.append(qr * lb_re - qi * lb_im)
        pis.append(qr * lb_im + qi * lb_re)
    pw_r, pw_i = jnp.stack(prs, axis=1), jnp.stack(pis, axis=1)
    rows_gh = lambda m: m.reshape(nl, SSM_SUB, g * h, m.shape[-1])
    lanes = lambda m: jnp.tile(m, LANES // m.shape[-1])
    qr, qi = pw_r[:, :SSM_SUB, :, :, None], pw_i[:, :SSM_SUB, :, :, None]
    vr = jnp.swapaxes(qr * bb_re - qi * bb_im, -1, -2)
    vi = jnp.swapaxes(qr * bb_im + qi * bb_re, -1, -2)
    bc = lanes(jnp.stack([rows_gh(vr), rows_gh(vi)], axis=2))[:, ::-1]
    qr, qi = pw_r[:, 1:, :, None, :], pw_i[:, 1:, :, None, :]
    cc = lanes(jnp.stack([rows_gh(c_re * qr - c_im * qi), -rows_gh(c_re * qi + c_im * qr)], axis=2))
    lag = jnp.sum(vr[..., None, :] * c_re[:, :, :, None] - vi[..., None, :] * c_im[:, :, :, None], axis=-1)
    gc = lanes(rows_gh(lag))
    mr, mi = pw_r[:, SSM_SUB].reshape(nl, 1, g * p), pw_i[:, SSM_SUB].reshape(nl, 1, g * p)
    mu = [(mr, mi)]
    for _ in range(7):
        qr, qi = mu[-1]
        mu.append((qr * mr - qi * mi, qr * mi + qi * mr))
    cat = lambda q: jnp.concatenate(q, axis=-1)
    rowid = jnp.arange(8)[:, None]
    tables = [jnp.where(rowid >= (1 << lvl), cat(mu[(1 << lvl) - 1]), 0.0) for lvl in range(3)]
    tables.append(jnp.concatenate([cat(q) for q in mu], axis=1))
    return bc, cc, gc, jnp.stack(tables, axis=1)


def kernel(x, p, w_in, w_merge, b_merge, conv_w, conv_b, conv_norm_g, conv_norm_b, w_pw2, mla_q_norm_g,
           mla_kv_norm_g, w_uq, w_ukv, ssm_a_re, ssm_a_im, ssm_log_dt, ssm_b_re, ssm_b_im, ssm_c_re,
           ssm_c_im, ssm_d, w_glu, attn_sinks, w_branch, w_out, ln_g, ln_b, w_ple, w_ple_gate, ple_norm_g):
    batch, seq, d = x.shape
    t = batch * seq
    rows = lambda a: a.reshape(DEPTH, 1, -1).astype(F32)
    bf = lambda a: a.astype(BF16)
    cos_t, sin_t = _rope_tables(seq)
    w_in_ext, w_uq_ext, w_ukv_ext = _in_weights(w_in), _uq_weights(w_uq), _ukv_weights(w_ukv)
    bp, cp, gd, sc = _ssm_weights(ssm_a_re, ssm_a_im, ssm_log_dt, ssm_b_re, ssm_b_im, ssm_c_re, ssm_c_im)
    conv_p = (conv_w.astype(F32), rows(conv_b), rows(conv_norm_g), rows(conv_norm_b), bf(w_pw2))
    mla_p = (rows(mla_q_norm_g), rows(mla_kv_norm_g), w_uq_ext, w_ukv_ext)
    ssm_p = (bp, cp, gd, sc, rows(ssm_d), bf(w_glu))
    merge_p = (bf(w_merge), rows(b_merge), bf(w_branch), bf(w_out), rows(ln_g), rows(ln_b), bf(w_ple),
               bf(w_ple_gate), rows(ple_norm_g))
    sinks = attn_sinks.astype(F32)
    p3d = p.reshape(DEPTH, t, PLE_DIM)
    x2d = x.reshape(t, d)
    for l in range(DEPTH):
        conv_in, mla_in, bz, ssm_in, swa_in, dz = _in_proj(l, x2d, w_in_ext)
        y_a = _conv_branch(l, conv_in, *conv_p, batch, seq)
        qt, k, vt = _mla_prep(l, mla_in, cos_t, sin_t, *mla_p, seq)
        y_b = _mla_attention(qt, k, vt, bz, batch, seq)
        y_c = _ssm_branch(l, ssm_in, *ssm_p, batch, seq)
        y_d = _swa_branch(l, swa_in, dz, sinks, batch, seq)
        ys = [y.reshape(t, BRANCH_W) for y in (y_a, y_b, y_c, y_d)]
        x2d = _merge(l, x2d, ys, p3d, *merge_p)
    return x2d.reshape(batch, seq, d)
```

```python
import functools
import math

import jax
import jax.numpy as jnp
from jax import lax
from jax.experimental import pallas as pl
from jax.experimental.pallas import tpu as pltpu

D_MODEL = 1024
DEPTH = 4
PLE_DIM = 256
N_BRANCH = 4
BRANCH_W = 256
CONV_W = 31
MLA_HEADS = 4
MLA_NOPE = 64
MLA_ROPE = 32
MLA_V = 64
MLA_Q_RANK = 256
MLA_KV_RANK = 128
ROPE_THETA = 10000.0
SSM_GROUP = 16
SSM_GROUPS = 16
SSM_STATE = 64
SWA_HEADS = 4
SWA_KV_HEADS = 2
SWA_HEAD_DIM = 64
WINDOW = 128
DEEPNORM_ALPHA = (2.0 * DEPTH) ** 0.25
LN_EPS = 1e-5
RMS_EPS = 1e-6

LANES = 128
VMEM_LIMIT_BYTES = 56 * 1024 * 1024

IN_TM = 512
SEQ_BLOCK = 512
SSM_CHUNK = 2048
SSM_SUB = 8
SSM_ROWS = 512
ATTN_BLK = 512
MERGE_TM = 512
CONV_HALO = 32
CONV_ROWS = 64

MLA_SLOT = LANES
MLA_VT_ROWS = MLA_V + 16
LOG2E = math.log2(math.e)
NSTATE = SSM_GROUPS * SSM_STATE
NEG_BIG = -1e30

F32 = jnp.float32
BF16 = jnp.bfloat16


def _params(*sem):
    return pltpu.CompilerParams(dimension_semantics=sem, vmem_limit_bytes=VMEM_LIMIT_BYTES)


def _resident(l, shape):
    nd = len(shape)
    return pl.BlockSpec((None,) + tuple(shape), lambda *_: (l,) + (0,) * nd, pipeline_mode=pl.Buffered(1))


def _dot(a, b):
    return jnp.dot(a, b, preferred_element_type=F32)


def _dot_nt(a, b):
    return lax.dot_general(a, b, (((1,), (1,)), ((), ())), preferred_element_type=F32)


def _sigmoid(x):
    return 1.0 / (1.0 + jnp.exp(-x))


def _silu(x):
    return x * _sigmoid(x)


PIECE_W = 256
PIECES = ("bz", "u", "c_z", "dz", "swa_q", "swa_kv", "c_q", "ckv_kr")
N_PIECES = len(PIECES)
P_BZ, P_U, P_CZ, P_DZ, P_SWAQ, P_SWAKV, P_CQ, P_CKV = range(N_PIECES)
CHUNKS_PER_PIECE = SEQ_BLOCK // (N_PIECES * CONV_ROWS)
PIECES_PER_STEP = 1


def _in_conv_kernel(x_ref, wh_ref, wl_ref, wt_ref, cw_ref, cb_ref, ng_ref, nb_ref, pw2_ref,
                    o_ref, pf_ref, krot_ref, xb_ref, h_ref, hbuf, ybuf):
    i = pl.program_id(1)
    bs = SEQ_BLOCK
    w = BRANCH_W
    xb_ref[...] = x_ref[...].astype(BF16)
    h_ref[...] = _dot(xb_ref[...], wh_ref[...])
    krot_ref[...] = _dot(xb_ref[...], wt_ref[...])

    @pl.when(i == 0)
    def _():
        hbuf[0:CONV_HALO, :] = jnp.zeros((CONV_HALO, w), F32)

    @pl.when(i > 0)
    def _():
        hbuf[0:CONV_HALO, :] = hbuf[bs:bs + CONV_HALO, :]

    hbuf[CONV_HALO:, :] = h_ref[:, 0:w] * _sigmoid(h_ref[:, w:2 * w])

    base = CONV_HALO - (CONV_W - 1)
    tile = 8

    def step(s, carry):
        for q in range(PIECES_PER_STEP * CHUNKS_PER_PIECE):
            c = s * PIECES_PER_STEP * CHUNKS_PER_PIECE + q
            if q % CHUNKS_PER_PIECE == 0:
                piece = s * PIECES_PER_STEP + q // CHUNKS_PER_PIECE
                pf_ref[piece] = _dot(xb_ref[...], wl_ref[piece])
            r0 = c * CONV_ROWS
            acc = jnp.broadcast_to(cb_ref[...], (CONV_ROWS, w))
            for r in range(tile):
                span = CONV_ROWS + (tile if r else 0)
                part = None
                for j in range((r - base) % tile, CONV_W, tile):
                    lo = pl.multiple_of(r0 + (base + j - r), tile)
                    term = cw_ref[j:j + 1, :] * hbuf[pl.ds(lo, span), :]
                    part = term if part is None else part + term
                acc = acc + part[r:r + CONV_ROWS, :]
            mu = jnp.mean(acc, axis=-1, keepdims=True)
            d = acc - mu
            var = jnp.mean(d * d, axis=-1, keepdims=True)
            y = d * lax.rsqrt(var + LN_EPS) * ng_ref[...] + nb_ref[...]
            ybuf[pl.ds(pl.multiple_of(r0, CONV_ROWS), CONV_ROWS), :] = _silu(y).astype(BF16)
        return carry

    lax.fori_loop(0, N_PIECES // PIECES_PER_STEP, step, 0)
    o_ref[...] = (_dot(ybuf[...], pw2_ref[...]) * _silu(h_ref[:, 2 * w:3 * w])).astype(o_ref.dtype)


def _in_conv(l, x3d, w_head, w_loop, w_tail, conv_w, conv_b, norm_g, norm_b, w_pw2):
    batch, seq, _ = x3d.shape
    bs = SEQ_BLOCK
    w = BRANCH_W
    nsb = seq // bs
    return pl.pallas_call(
        _in_conv_kernel,
        grid=(batch, nsb),
        in_specs=[pl.BlockSpec((None, bs, D_MODEL), lambda b, i: (b, i, 0)),
                  _resident(l, (D_MODEL, 3 * w)), _resident(l, (N_PIECES, D_MODEL, PIECE_W)),
                  _resident(l, (D_MODEL, LANES)),
                  _resident(l, (CONV_W, w)), _resident(l, (1, w)), _resident(l, (1, w)), _resident(l, (1, w)),
                  _resident(l, (w, w))],
        out_specs=[pl.BlockSpec((None, bs, w), lambda b, i: (b, i, 0)),
                   pl.BlockSpec((N_PIECES, bs, PIECE_W), lambda b, i: (0, b * nsb + i, 0)),
                   pl.BlockSpec((bs, LANES), lambda b, i: (b * nsb + i, 0))],
        out_shape=[jax.ShapeDtypeStruct((batch, seq, w), BF16),
                   jax.ShapeDtypeStruct((N_PIECES, batch * seq, PIECE_W), F32),
                   jax.ShapeDtypeStruct((batch * seq, LANES), F32)],
        scratch_shapes=[pltpu.VMEM((bs, D_MODEL), BF16), pltpu.VMEM((bs, 3 * w), F32),
                        pltpu.VMEM((bs + CONV_HALO, w), F32), pltpu.VMEM((bs, w), BF16)],
        compiler_params=_params("parallel", "arbitrary"),
        name="in_conv",
    )(x3d, w_head, w_loop, w_tail, conv_w, conv_b, norm_g, norm_b, w_pw2)


SSM_BOUND = SSM_CHUNK // SSM_SUB
assert SSM_BOUND % 8 == 0


def _ssm_kernel(u_ref, cz_ref, bc_ref, cc_ref, gc_ref, sc_ref, d_ref, wglu_ref, o_ref,
                carry, ubuf, ybuf, hbuf, bp_ref, cpt_ref, g_ref):
    i = pl.program_id(1)
    n = NSTATE
    w = BRANCH_W
    m = SSM_BOUND

    @pl.when((pl.program_id(0) == 0) & (i == 0))
    def _():
        shift = lambda a, k: lax.shift_right_logical(a, jnp.full(a.shape, k, jnp.int32))
        row_g = shift(lax.broadcasted_iota(jnp.int32, (w, n), 0), int(math.log2(SSM_GROUP)))
        col_g = shift(lax.broadcasted_iota(jnp.int32, (w, n), 1), int(math.log2(SSM_STATE)))
        same_group = row_g == col_g
        in_g = shift(lax.broadcasted_iota(jnp.int32, (w, w), 0), int(math.log2(SSM_GROUP)))
        out_g = shift(lax.broadcasted_iota(jnp.int32, (w, w), 1), int(math.log2(SSM_GROUP)))
        same_group_out = in_g == out_g
        for k in range(SSM_SUB):
            for c in range(2):
                for src, dst in ((bc_ref, bp_ref), (cc_ref, cpt_ref)):
                    wide = jnp.concatenate([src[k, c]] * (n // LANES), axis=1)
                    dst[k, :, c * n:(c + 1) * n] = jnp.where(same_group, wide, 0.0).astype(BF16)
            wide = jnp.concatenate([gc_ref[k]] * (w // LANES), axis=1)
            g_ref[k] = jnp.where(same_group_out, wide, 0.0).astype(BF16)

    @pl.when(i == 0)
    def _():
        carry[...] = jnp.zeros_like(carry)

    nt = w // LANES
    for k in range(nt):
        ubuf[k] = u_ref[:, k * LANES:(k + 1) * LANES]

    x = None
    for p in range(SSM_SUB):
        up = jnp.concatenate([ubuf[k, pl.ds(p, m, stride=SSM_SUB), :] for k in range(nt)], axis=1).astype(BF16)
        t = _dot(up, bp_ref[p])
        x = t if x is None else x + t
    tile = 8
    first = lax.broadcasted_iota(jnp.int32, (tile, n), 0) == 0
    cr = carry[0:1, :]
    ci = carry[1:2, :]
    for k in range(m // tile):
        tr = x[k * tile:(k + 1) * tile, 0:n]
        ti = x[k * tile:(k + 1) * tile, n:2 * n]
        for lvl in range(3):
            s = 1 << lvl
            ar = sc_ref[lvl, :, 0:n]
            ai = sc_ref[lvl, :, n:2 * n]
            sr = pltpu.roll(tr, s, 0)
            si = pltpu.roll(ti, s, 0)
            tr, ti = tr + (ar * sr - ai * si), ti + (ar * si + ai * sr)
        pr = sc_ref[3, :, 0:n]
        pi = sc_ref[3, :, n:2 * n]
        tr, ti = tr + (pr * cr - pi * ci), ti + (pr * ci + pi * cr)
        hbuf[k * tile:(k + 1) * tile, 0:n] = jnp.where(first, cr, pltpu.roll(tr, 1, 0))
        hbuf[k * tile:(k + 1) * tile, n:2 * n] = jnp.where(first, ci, pltpu.roll(ti, 1, 0))
        cr = tr[tile - 1:tile, :]
        ci = ti[tile - 1:tile, :]
    carry[0:1, :] = cr
    carry[1:2, :] = ci
    hcat = hbuf[...].astype(BF16)
    for p in range(SSM_SUB):
        yp = _dot_nt(hcat, cpt_ref[p])
        for k in range(nt):
            ybuf[k, pl.ds(p, m, stride=SSM_SUB), :] = yp[:, k * LANES:(k + 1) * LANES]

    rb = SSM_ROWS
    pos = lax.broadcasted_iota(jnp.int32, (rb, w), 0) % SSM_SUB
    for r0 in range(0, SSM_CHUNK, rb):
        u = u_ref[r0:r0 + rb, :]
        yb = jnp.concatenate([ybuf[k, r0:r0 + rb, :] for k in range(nt)], axis=1)
        y = yb + _dot(u.astype(BF16), g_ref[0])
        for d in range(1, SSM_SUB):
            ud = jnp.where(pos >= d, pltpu.roll(u, d, 0), 0.0)
            y = y + _dot(ud.astype(BF16), g_ref[d])
        y = jax.nn.gelu(y + d_ref[...] * u)
        g = _dot(y.astype(BF16), wglu_ref[...])
        out = g[:, 0:w] * _sigmoid(g[:, w:2 * w]) * _silu(cz_ref[r0:r0 + rb, :])
        o_ref[r0:r0 + rb, :] = out.astype(o_ref.dtype)


def _piece(p, rows, per_batch):
    return pl.BlockSpec((None, rows, PIECE_W), lambda b, i: (p, b * per_batch + i, 0))


def _ssm_branch(l, pf, bc, cc, gc, powers, d_skip, w_glu, batch, seq):
    rows = SSM_CHUNK
    w = BRANCH_W
    return pl.pallas_call(
        _ssm_kernel,
        grid=(batch, seq // rows),
        in_specs=[_piece(P_U, rows, seq // rows), _piece(P_CZ, rows, seq // rows),
                  _resident(l, (SSM_SUB, 2, w, LANES)), _resident(l, (SSM_SUB, 2, w, LANES)),
                  _resident(l, (SSM_SUB, w, LANES)),
                  _resident(l, (4, 8, 2 * NSTATE)), _resident(l, (1, w)), _resident(l, (w, 2 * w))],
        out_specs=pl.BlockSpec((None, rows, w), lambda b, i: (b, i, 0)),
        out_shape=jax.ShapeDtypeStruct((batch, seq, w), BF16),
        scratch_shapes=[pltpu.VMEM((8, NSTATE), F32), pltpu.VMEM((w // LANES, rows, LANES), F32),
                        pltpu.VMEM((w // LANES, rows, LANES), F32), pltpu.VMEM((SSM_BOUND, 2 * NSTATE), F32),
                        pltpu.VMEM((SSM_SUB, w, 2 * NSTATE), BF16), pltpu.VMEM((SSM_SUB, w, 2 * NSTATE), BF16),
                        pltpu.VMEM((SSM_SUB, w, w), BF16)],
        compiler_params=_params("arbitrary", "arbitrary"),
        name="ssm_branch",
    )(pf, pf, bc, cc, gc, powers, d_skip, w_glu)


SWA_GROUP = SWA_HEADS // SWA_KV_HEADS
SWA_BLOCKS = SEQ_BLOCK // WINDOW


SWA_VT_ROWS = SWA_HEAD_DIM + 16


def _swa_kernel(l, sink_ref, q_ref, kv_ref, prev_ref, dz_ref, o_ref, qbuf, kbuf, vtbuf, ytbuf):
    i = pl.program_id(1)
    hd = SWA_HEAD_DIM
    kv_w = SWA_KV_HEADS * hd
    win = WINDOW
    cols = SWA_GROUP * win

    qbuf[...] = (q_ref[...] * (hd ** -0.5 * LOG2E)).astype(BF16)
    kbuf[0:win, :] = prev_ref[:, 0:kv_w].astype(BF16)
    kbuf[win:, :] = kv_ref[:, 0:kv_w].astype(BF16)
    vt = jnp.concatenate([prev_ref[:, kv_w:2 * kv_w], kv_ref[:, kv_w:2 * kv_w]], axis=0).T
    ones = jnp.ones((SWA_VT_ROWS - hd, vt.shape[1]), BF16)
    for kvh in range(SWA_KV_HEADS):
        vtbuf[kvh * SWA_VT_ROWS:kvh * SWA_VT_ROWS + hd, :] = vt[kvh * hd:(kvh + 1) * hd, :].astype(BF16)
        vtbuf[kvh * SWA_VT_ROWS + hd:(kvh + 1) * SWA_VT_ROWS, :] = ones

    krow = lax.broadcasted_iota(jnp.int32, (2 * win, cols), 0)
    qpos = lax.broadcasted_iota(jnp.int32, (2 * win, cols), 1) % win + win
    rel = qpos - krow
    band = (rel >= 0) & (rel < win)
    first_head = lax.broadcasted_iota(jnp.int32, (1, cols), 1) < win
    kmin = jnp.where(i == 0, win, 0)

    scores = {}
    for j in range(SWA_BLOCKS):
        for kvh in range(SWA_KV_HEADS):
            q2 = jnp.concatenate(
                [qbuf[j * win:(j + 1) * win, (kvh * SWA_GROUP + g) * hd:(kvh * SWA_GROUP + g + 1) * hd]
                 for g in range(SWA_GROUP)], axis=0)
            k2 = kbuf[j * win:(j + 2) * win, kvh * hd:(kvh + 1) * hd]
            scores[j, kvh] = _dot_nt(k2, q2)
    for j in range(SWA_BLOCKS):
        valid = band & (krow >= kmin) if j == 0 else band
        for kvh in range(SWA_KV_HEADS):
            s = jnp.where(valid, scores[j, kvh], NEG_BIG)
            sink = jnp.where(first_head, sink_ref[l, kvh * SWA_GROUP] * LOG2E,
                             sink_ref[l, kvh * SWA_GROUP + 1] * LOG2E)
            m = jnp.maximum(jnp.max(s, axis=0, keepdims=True), sink)
            p = jnp.exp2(s - m).astype(BF16)
            acc = _dot(vtbuf[kvh * SWA_VT_ROWS:(kvh + 1) * SWA_VT_ROWS, j * win:(j + 2) * win], p)
            denom = acc[hd:hd + 1, :] + jnp.exp2(sink - m)
            o = acc[0:hd, :] * (1.0 / denom)
            for g in range(SWA_GROUP):
                h = kvh * SWA_GROUP + g
                ytbuf[h * hd:(h + 1) * hd, j * win:(j + 1) * win] = o[:, g * win:(g + 1) * win]
    o_ref[...] = (ytbuf[...].T * _silu(dz_ref[...])).astype(o_ref.dtype)


def _swa_branch(l, pf, sinks, batch, seq):
    bs = SEQ_BLOCK
    w = BRANCH_W
    nsb = seq // bs
    wins = seq // WINDOW
    prev_spec = pl.BlockSpec((None, WINDOW, PIECE_W),
                             lambda b, i: (P_SWAKV, b * wins + jnp.maximum(i * SWA_BLOCKS - 1, 0), 0))
    return pl.pallas_call(
        functools.partial(_swa_kernel, l),
        grid=(batch, nsb),
        in_specs=[pl.BlockSpec(memory_space=pltpu.SMEM),
                  _piece(P_SWAQ, bs, nsb), _piece(P_SWAKV, bs, nsb), prev_spec, _piece(P_DZ, bs, nsb)],
        out_specs=pl.BlockSpec((None, bs, w), lambda b, i: (b, i, 0)),
        out_shape=jax.ShapeDtypeStruct((batch, seq, w), BF16),
        scratch_shapes=[pltpu.VMEM((bs, w), BF16), pltpu.VMEM((bs + WINDOW, SWA_KV_HEADS * SWA_HEAD_DIM), BF16),
                        pltpu.VMEM((SWA_KV_HEADS * SWA_VT_ROWS, bs + WINDOW), BF16), pltpu.VMEM((w, bs), F32)],
        compiler_params=_params("parallel", "arbitrary"),
        name="swa_branch",
    )(sinks, pf, pf, pf, pf)


MLA_QK = MLA_HEADS * MLA_SLOT


def _rms(x, g):
    ms = jnp.mean(x * x, axis=-1, keepdims=True)
    return x * lax.rsqrt(ms + RMS_EPS) * g


def _mla_prep_kernel(cq_ref, ckv_ref, krot_ref, cos_ref, sin_ref, qg_ref, kvg_ref, wuq_ref, wukv_ref,
                     qt_ref, k_ref, vt_ref):
    scale = (MLA_NOPE + MLA_ROPE) ** -0.5 * LOG2E
    cq = _rms(cq_ref[...], qg_ref[...]).astype(BF16)
    ckv = _rms(ckv_ref[:, 0:MLA_KV_RANK], kvg_ref[...]).astype(BF16)
    cos = cos_ref[...]
    sin = sin_ref[...]
    k_rope = ckv_ref[:, MLA_KV_RANK:MLA_KV_RANK + LANES] * cos + krot_ref[...] * sin
    kv = _dot(ckv, wukv_ref[...])
    for h in range(MLA_HEADS):
        lo = h * MLA_SLOT
        qm = _dot(cq, wuq_ref[:, lo:lo + MLA_SLOT])
        qr = _dot(cq, wuq_ref[:, MLA_QK + lo:MLA_QK + lo + MLA_SLOT])
        qt_ref[lo:lo + MLA_SLOT, :] = ((qm * cos + qr * sin) * scale).T.astype(BF16)
        k_ref[:, lo:lo + MLA_SLOT] = (kv[:, lo:lo + MLA_SLOT] + k_rope).astype(BF16)
    vt = kv[:, MLA_QK:].T.astype(BF16)
    ones = jnp.ones((MLA_VT_ROWS - MLA_V, vt.shape[1]), BF16)
    for h in range(MLA_HEADS):
        vt_ref[h * MLA_VT_ROWS:h * MLA_VT_ROWS + MLA_V, :] = vt[h * MLA_V:(h + 1) * MLA_V, :]
        vt_ref[h * MLA_VT_ROWS + MLA_V:(h + 1) * MLA_VT_ROWS, :] = ones


def _mla_prep(l, pf, krot, cos_t, sin_t, q_norm_g, kv_norm_g, w_uq_ext, w_ukv_ext, seq):
    t = krot.shape[0]
    tm = ATTN_BLK
    nsb = seq // tm
    vw = MLA_HEADS * MLA_V
    vt_rows = MLA_HEADS * MLA_VT_ROWS
    return pl.pallas_call(
        _mla_prep_kernel,
        grid=(t // tm,),
        in_specs=[pl.BlockSpec((None, tm, PIECE_W), lambda i: (P_CQ, i, 0)),
                  pl.BlockSpec((None, tm, PIECE_W), lambda i: (P_CKV, i, 0)),
                  pl.BlockSpec((tm, LANES), lambda i: (i, 0)),
                  pl.BlockSpec((tm, LANES), lambda i: (i % nsb, 0)),
                  pl.BlockSpec((tm, LANES), lambda i: (i % nsb, 0)),
                  _resident(l, (1, MLA_Q_RANK)), _resident(l, (1, MLA_KV_RANK)),
                  _resident(l, (MLA_Q_RANK, 2 * MLA_QK)), _resident(l, (MLA_KV_RANK, MLA_QK + vw))],
        out_specs=[pl.BlockSpec((None, MLA_QK, tm), lambda i: (i, 0, 0)),
                   pl.BlockSpec((tm, MLA_QK), lambda i: (i, 0)),
                   pl.BlockSpec((None, vt_rows, tm), lambda i: (i, 0, 0))],
        out_shape=[jax.ShapeDtypeStruct((t // tm, MLA_QK, tm), BF16), jax.ShapeDtypeStruct((t, MLA_QK), BF16),
                   jax.ShapeDtypeStruct((t // tm, vt_rows, tm), BF16)],
        compiler_params=_params("parallel"),
        name="mla_prep",
    )(pf, pf, krot, cos_t, sin_t, q_norm_g, kv_norm_g, w_uq_ext, w_ukv_ext)


def _attn_kernel(qt_ref, k_ref, vt_ref, bz_ref, o_ref, m_sc, acc_sc, s_sc):
    i = pl.program_id(1)
    blk = ATTN_BLK
    krow = lax.broadcasted_iota(jnp.int32, (blk, blk), 0)
    qcol = lax.broadcasted_iota(jnp.int32, (blk, blk), 1)
    causal = krow <= qcol
    m_sc[...] = jnp.full(m_sc.shape, NEG_BIG, F32)
    acc_sc[...] = jnp.zeros(acc_sc.shape, F32)

    def score(j, slot):
        start = j * blk if isinstance(j, int) else pl.multiple_of(j * blk, blk)
        for h in range(MLA_HEADS):
            kb = k_ref[pl.ds(start, blk), h * MLA_SLOT:(h + 1) * MLA_SLOT]
            s_sc[slot, h] = _dot(kb, qt_ref[h * MLA_SLOT:(h + 1) * MLA_SLOT, :])

    def absorb(j, slot, masked):
        for h in range(MLA_HEADS):
            s = s_sc[slot, h]
            if masked:
                s = jnp.where(causal, s, NEG_BIG)
            m_prev = m_sc[h]
            m_new = jnp.maximum(m_prev, jnp.max(s, axis=0, keepdims=True))
            alpha = jnp.exp2(m_prev - m_new)
            p = jnp.exp2(s - m_new).astype(BF16)
            vb = vt_ref[j, h * MLA_VT_ROWS:(h + 1) * MLA_VT_ROWS, :]
            acc_sc[h] = alpha * acc_sc[h] + _dot(vb, p)
            m_sc[h] = m_new

    score(0, 0)

    def body(t, c):
        j = 2 * t
        score(j + 1, 1)
        absorb(j, 0, False)
        score(j + 2, 0)
        absorb(j + 1, 1, False)
        return c

    lax.fori_loop(0, i // 2, body, 0)

    @pl.when(i % 2 == 0)
    def _():
        absorb(i, 0, True)

    @pl.when(i % 2 == 1)
    def _():
        score(i, 1)
        absorb(i - 1, 0, False)
        absorb(i, 1, True)
    outs = [acc_sc[h, 0:MLA_V, :] * (1.0 / acc_sc[h, MLA_V:MLA_V + 1, :]) for h in range(MLA_HEADS)]
    y = jnp.concatenate(outs, axis=0).T
    o_ref[...] = (y * _silu(bz_ref[...])).astype(o_ref.dtype)


def _mla_attention(qt, k, vt, pf, batch, seq):
    blk = ATTN_BLK
    nq = seq // blk
    vt_rows = MLA_HEADS * MLA_VT_ROWS
    return pl.pallas_call(
        _attn_kernel,
        grid=(batch, nq),
        in_specs=[pl.BlockSpec((None, MLA_QK, blk), lambda b, i: (b * nq + i, 0, 0)),
                  pl.BlockSpec((None, seq, MLA_QK), lambda b, i: (b, 0, 0)),
                  pl.BlockSpec((None, nq, vt_rows, blk), lambda b, i: (b, 0, 0, 0)),
                  _piece(P_BZ, blk, nq)],
        out_specs=pl.BlockSpec((None, blk, BRANCH_W), lambda b, i: (b, i, 0)),
        out_shape=jax.ShapeDtypeStruct((batch, seq, BRANCH_W), BF16),
        scratch_shapes=[pltpu.VMEM((MLA_HEADS, 1, blk), F32), pltpu.VMEM((MLA_HEADS, MLA_VT_ROWS, blk), F32),
                        pltpu.VMEM((2, MLA_HEADS, blk, blk), F32)],
        compiler_params=_params("parallel", "arbitrary"),
        name="mla_attention",
    )(qt, k.reshape(batch, seq, MLA_QK), vt.reshape(batch, nq, vt_rows, blk), pf)


def _merge_kernel(x_ref, ya_ref, yb_ref, yc_ref, yd_ref, p_ref, wm_ref, bm_ref, wb_ref, wo_ref,
                  lng_ref, lnb_ref, wp_ref, wpg_ref, pg_ref, o_ref):
    d = D_MODEL
    x = x_ref[...]
    xb = x.astype(BF16)
    merged = None
    for n, y_ref in enumerate((ya_ref, yb_ref, yc_ref, yd_ref)):
        gate = _sigmoid(_dot(xb, wm_ref[:, n * d:(n + 1) * d]) + bm_ref[:, n * d:(n + 1) * d])
        term = gate * _dot(y_ref[...], wb_ref[n])
        merged = term if merged is None else merged + term
    z = DEEPNORM_ALPHA * x + _dot(merged.astype(BF16), wo_ref[...])
    mu = jnp.mean(z, axis=-1, keepdims=True)
    zc = z - mu
    var = jnp.mean(zc * zc, axis=-1, keepdims=True)
    xn = zc * lax.rsqrt(var + LN_EPS) * lng_ref[...] + lnb_ref[...]
    e = _dot(p_ref[...].astype(BF16), wp_ref[...]) * _sigmoid(_dot(xn.astype(BF16), wpg_ref[...]))
    o_ref[...] = xn + _rms(e, pg_ref[...])


def _merge(l, x2d, ys, p3d, w_merge, b_merge, w_branch, w_out, ln_g, ln_b, w_ple, w_ple_gate, ple_norm_g):
    t = x2d.shape[0]
    tm = MERGE_TM
    d = D_MODEL
    row_spec = lambda width: pl.BlockSpec((tm, width), lambda i: (i, 0))
    return pl.pallas_call(
        _merge_kernel,
        grid=(t // tm,),
        in_specs=[row_spec(d)] + [row_spec(BRANCH_W)] * N_BRANCH + [
                  pl.BlockSpec((None, tm, PLE_DIM), lambda i: (l, i, 0)),
                  _resident(l, (d, N_BRANCH * d)), _resident(l, (1, N_BRANCH * d)),
                  _resident(l, (N_BRANCH, BRANCH_W, d)), _resident(l, (d, d)),
                  _resident(l, (1, d)), _resident(l, (1, d)), _resident(l, (PLE_DIM, d)), _resident(l, (d, d)),
                  _resident(l, (1, d))],
        out_specs=row_spec(d),
        out_shape=jax.ShapeDtypeStruct((t, d), F32),
        compiler_params=_params("parallel"),
        name="merge",
    )(x2d, *ys, p3d, w_merge, b_merge, w_branch, w_out, ln_g, ln_b, w_ple, w_ple_gate, ple_norm_g)


def _rot_cols(w):
    half = w.shape[-1] // 2
    return jnp.concatenate([-w[..., half:], w[..., :half]], axis=-1)


def _zero_cols(w, n):
    return jnp.zeros(w.shape[:-1] + (n,), w.dtype)


def _rope_slot(w):
    return jnp.concatenate([_zero_cols(w, MLA_NOPE), w, _zero_cols(w, MLA_SLOT - MLA_NOPE - MLA_ROPE)], axis=-1)


def _in_weights(w_in):
    o = [0]
    for s in (256, 256, 256, 256, 128, 32, 256, 256, 256, 256, 128, 128, 256):
        o.append(o[-1] + s)
    col = lambda k: w_in[..., o[k]:o[k + 1]]
    a_val, a_gate, a_z, c_q, c_kv, k_r, b_z, u, c_z, q, k, v, d_z = (col(k) for k in range(13))
    cat = lambda parts: jnp.concatenate(parts, axis=-1)
    head = cat([a_val, a_gate, a_z])
    pieces = {"bz": b_z, "u": u, "c_z": c_z, "dz": d_z, "swa_q": q, "swa_kv": cat([k, v]), "c_q": c_q,
              "ckv_kr": cat([c_kv, _rope_slot(k_r)])}
    loop = jnp.stack([pieces[name] for name in PIECES], axis=1)
    tail = _rope_slot(_rot_cols(k_r))
    return head.astype(BF16), loop.astype(BF16), tail.astype(BF16)


def _uq_weights(w_uq):
    hd = MLA_NOPE + MLA_ROPE
    main, rot = [], []
    for h in range(MLA_HEADS):
        head = w_uq[..., h * hd:(h + 1) * hd]
        main.append(jnp.concatenate([head, _zero_cols(w_uq, MLA_SLOT - hd)], axis=-1))
        rot.append(_rope_slot(_rot_cols(head[..., MLA_NOPE:])))
    return jnp.concatenate(main + rot, axis=-1).astype(BF16)


def _ukv_weights(w_ukv):
    hd = MLA_NOPE + MLA_V
    ks, vs = [], []
    for h in range(MLA_HEADS):
        ks.append(jnp.concatenate([w_ukv[..., h * hd:h * hd + MLA_NOPE], _zero_cols(w_ukv, MLA_SLOT - MLA_NOPE)],
                                  axis=-1))
        vs.append(w_ukv[..., h * hd + MLA_NOPE:(h + 1) * hd])
    return jnp.concatenate(ks + vs, axis=-1).astype(BF16)


def _rope_tables(seq):
    pos = jnp.arange(seq, dtype=F32)
    inv_freq = ROPE_THETA ** (-jnp.arange(0, MLA_ROPE, 2, dtype=F32) / MLA_ROPE)
    ang = pos[:, None] * inv_freq[None, :]
    cos, sin = jnp.cos(ang), jnp.sin(ang)
    ones = jnp.ones((seq, MLA_NOPE), F32)
    zeros = jnp.zeros((seq, MLA_NOPE), F32)
    pad = jnp.zeros((seq, MLA_SLOT - MLA_NOPE - MLA_ROPE), F32)
    cos_t = jnp.concatenate([ones, cos, cos, pad], axis=1)
    sin_t = jnp.concatenate([zeros, sin, sin, pad], axis=1)
    return cos_t, sin_t


def _ssm_weights(a_re, a_im, log_dt, b_re, b_im, c_re, c_im):
    g, p, h = SSM_GROUPS, SSM_STATE, SSM_GROUP
    nl = a_re.shape[0]
    dt = jnp.exp(log_dt.astype(F32))[..., None]
    lr, li = a_re.astype(F32), a_im.astype(F32)
    mag = jnp.exp(lr * dt)
    lb_re, lb_im = mag * jnp.cos(li * dt), mag * jnp.sin(li * dt)
    den = lr * lr + li * li
    nr, ni = lb_re - 1.0, lb_im
    f_re = ((nr * lr + ni * li) / den)[..., None]
    f_im = ((ni * lr - nr * li) / den)[..., None]
    bb_re = (f_re * b_re - f_im * b_im)[:, None]
    bb_im = (f_re * b_im + f_im * b_re)[:, None]
    c_re, c_im = c_re.astype(F32)[:, None], c_im.astype(F32)[:, None]
    prs, pis = [jnp.ones_like(lb_re)], [jnp.zeros_like(lb_im)]
    for _ in range(SSM_SUB):
        qr, qi = prs[-1], pis[-1]
        prs.append(qr * lb_re - qi * lb_im)
        pis.append(qr * lb_im + qi * lb_re)
    pw_r, pw_i = jnp.stack(prs, axis=1), jnp.stack(pis, axis=1)
    rows_gh = lambda m: m.reshape(nl, SSM_SUB, g * h, m.shape[-1])
    lanes = lambda m: jnp.tile(m, LANES // m.shape[-1])
    qr, qi = pw_r[:, :SSM_SUB, :, :, None], pw_i[:, :SSM_SUB, :, :, None]
    vr = jnp.swapaxes(qr * bb_re - qi * bb_im, -1, -2)
    vi = jnp.swapaxes(qr * bb_im + qi * bb_re, -1, -2)
    bc = lanes(jnp.stack([rows_gh(vr), rows_gh(vi)], axis=2))[:, ::-1]
    qr, qi = pw_r[:, 1:, :, None, :], pw_i[:, 1:, :, None, :]
    cc = lanes(jnp.stack([rows_gh(c_re * qr - c_im * qi), -rows_gh(c_re * qi + c_im * qr)], axis=2))
    lag = jnp.sum(vr[..., None, :] * c_re[:, :, :, None] - vi[..., None, :] * c_im[:, :, :, None], axis=-1)
    gc = lanes(rows_gh(lag))
    mr, mi = pw_r[:, SSM_SUB].reshape(nl, 1, g * p), pw_i[:, SSM_SUB].reshape(nl, 1, g * p)
    mu = [(mr, mi)]
    for _ in range(7):
        qr, qi = mu[-1]
        mu.append((qr * mr - qi * mi, qr * mi + qi * mr))
    cat = lambda q: jnp.concatenate(q, axis=-1)
    rowid = jnp.arange(8)[:, None]
    tables = [jnp.where(rowid >= (1 << lvl), cat(mu[(1 << lvl) - 1]), 0.0) for lvl in range(3)]
    tables.append(jnp.concatenate([cat(q) for q in mu], axis=1))
    return bc, cc, gc, jnp.stack(tables, axis=1)


def kernel(x, p, w_in, w_merge, b_merge, conv_w, conv_b, conv_norm_g, conv_norm_b, w_pw2, mla_q_norm_g,
           mla_kv_norm_g, w_uq, w_ukv, ssm_a_re, ssm_a_im, ssm_log_dt, ssm_b_re, ssm_b_im, ssm_c_re,
           ssm_c_im, ssm_d, w_glu, attn_sinks, w_branch, w_out, ln_g, ln_b, w_ple, w_ple_gate, ple_norm_g):
    batch, seq, d = x.shape
    t = batch * seq
    rows = lambda a: a.reshape(DEPTH, 1, -1).astype(F32)
    bf = lambda a: a.astype(BF16)
    cos_t, sin_t = _rope_tables(seq)
    w_in_parts, w_uq_ext, w_ukv_ext = _in_weights(w_in), _uq_weights(w_uq), _ukv_weights(w_ukv)
    bp, cp, gd, sc = _ssm_weights(ssm_a_re, ssm_a_im, ssm_log_dt, ssm_b_re, ssm_b_im, ssm_c_re, ssm_c_im)
    conv_p = (conv_w.astype(F32), rows(conv_b), rows(conv_norm_g), rows(conv_norm_b), bf(w_pw2))
    mla_p = (rows(mla_q_norm_g), rows(mla_kv_norm_g), w_uq_ext, w_ukv_ext)
    ssm_p = (bp, cp, gd, sc, rows(ssm_d), bf(w_glu))
    merge_p = (bf(w_merge), rows(b_merge), bf(w_branch), bf(w_out), rows(ln_g), rows(ln_b), bf(w_ple),
               bf(w_ple_gate), rows(ple_norm_g))
    sinks = attn_sinks.astype(F32)
    p3d = p.reshape(DEPTH, t, PLE_DIM)
    x2d = x.reshape(t, d)
    for l in range(DEPTH):
        y_a, pf, krot = _in_conv(l, x2d.reshape(batch, seq, d), *w_in_parts, *conv_p)
        qt, k, vt = _mla_prep(l, pf, krot, cos_t, sin_t, *mla_p, seq)
        y_b = _mla_attention(qt, k, vt, pf, batch, seq)
        y_c = _ssm_branch(l, pf, *ssm_p, batch, seq)
        y_d = _swa_branch(l, pf, sinks, batch, seq)
        ys = [y.reshape(t, BRANCH_W) for y in (y_a, y_b, y_c, y_d)]
        x2d = _merge(l, x2d, ys, p3d, *merge_p)
    return x2d.reshape(batch, seq, d)
```

```python
import functools
import math

import jax
import jax.numpy as jnp
from jax import lax
from jax.experimental import pallas as pl
from jax.experimental.pallas import tpu as pltpu

D_MODEL = 1024
DEPTH = 4
PLE_DIM = 256
N_BRANCH = 4
BRANCH_W = 256
CONV_W = 31
MLA_HEADS = 4
MLA_NOPE = 64
MLA_ROPE = 32
MLA_V = 64
MLA_Q_RANK = 256
MLA_KV_RANK = 128
ROPE_THETA = 10000.0
SSM_GROUP = 16
SSM_GROUPS = 16
SSM_STATE = 64
SWA_HEADS = 4
SWA_KV_HEADS = 2
SWA_HEAD_DIM = 64
WINDOW = 128
DEEPNORM_ALPHA = (2.0 * DEPTH) ** 0.25
LN_EPS = 1e-5
RMS_EPS = 1e-6

LANES = 128
VMEM_LIMIT_BYTES = 56 * 1024 * 1024

IN_TM = 512
SEQ_BLOCK = 512
SSM_CHUNK = 2048
SSM_SUB = 8
SSM_ROWS = 512
ATTN_BLK = 512
MERGE_TM = 512
CONV_HALO = 32
CONV_ROWS = 64

MLA_SLOT = LANES
MLA_VT_ROWS = MLA_V + 16
LOG2E = math.log2(math.e)
NSTATE = SSM_GROUPS * SSM_STATE
NEG_BIG = -1e30

F32 = jnp.float32
BF16 = jnp.bfloat16


def _params(*sem):
    return pltpu.CompilerParams(dimension_semantics=sem, vmem_limit_bytes=VMEM_LIMIT_BYTES)


def _resident(l, shape):
    nd = len(shape)
    return pl.BlockSpec((None,) + tuple(shape), lambda *_: (l,) + (0,) * nd, pipeline_mode=pl.Buffered(1))


def _dot(a, b):
    return jnp.dot(a, b, preferred_element_type=F32)


def _dot_nt(a, b):
    return lax.dot_general(a, b, (((1,), (1,)), ((), ())), preferred_element_type=F32)


def _sigmoid(x):
    return 1.0 / (1.0 + jnp.exp(-x))


def _silu(x):
    return x * _sigmoid(x)


IN_GROUPS = (
    ("conv", 3 * BRANCH_W, F32),
    ("mla", MLA_Q_RANK + MLA_KV_RANK + 2 * LANES, F32),
    ("bz", BRANCH_W, F32),
    ("ssm", 2 * BRANCH_W, F32),
    ("swa", 2 * BRANCH_W, BF16),
    ("dz", BRANCH_W, F32),
)
IN_EXT_WIDTH = sum(w for _, w, _ in IN_GROUPS)


def _in_proj_kernel(x_ref, w_ref, *out_refs):
    xb = x_ref[...].astype(BF16)
    start = 0
    for (_, width, dtype), o_ref in zip(IN_GROUPS, out_refs):
        o_ref[...] = _dot(xb, w_ref[:, start:start + width]).astype(dtype)
        start += width


def _in_proj(l, x2d, w_in_ext):
    t = x2d.shape[0]
    return pl.pallas_call(
        _in_proj_kernel,
        grid=(t // IN_TM,),
        in_specs=[pl.BlockSpec((IN_TM, D_MODEL), lambda i: (i, 0)),
                  _resident(l, (D_MODEL, IN_EXT_WIDTH))],
        out_specs=[pl.BlockSpec((IN_TM, w), lambda i: (i, 0)) for _, w, _ in IN_GROUPS],
        out_shape=[jax.ShapeDtypeStruct((t, w), d) for _, w, d in IN_GROUPS],
        compiler_params=_params("parallel"),
        name="in_proj",
    )(x2d, w_in_ext)


def _conv_kernel(h_ref, cw_ref, cb_ref, ng_ref, nb_ref, pw2_ref, o_ref, hbuf, ybuf):
    i = pl.program_id(1)
    bs = SEQ_BLOCK
    w = BRANCH_W

    @pl.when(i == 0)
    def _():
        hbuf[0:CONV_HALO, :] = jnp.zeros((CONV_HALO, w), F32)

    @pl.when(i > 0)
    def _():
        hbuf[0:CONV_HALO, :] = hbuf[bs:bs + CONV_HALO, :]

    hbuf[CONV_HALO:, :] = h_ref[:, 0:w] * _sigmoid(h_ref[:, w:2 * w])

    base = CONV_HALO - (CONV_W - 1)
    tile = 8

    for c in range(bs // CONV_ROWS):
        r0 = c * CONV_ROWS
        acc = jnp.broadcast_to(cb_ref[...], (CONV_ROWS, w))
        for r in range(tile):
            span = CONV_ROWS + (tile if r else 0)
            part = None
            for j in range((r - base) % tile, CONV_W, tile):
                lo = r0 + base + j - r
                term = cw_ref[j:j + 1, :] * hbuf[lo:lo + span, :]
                part = term if part is None else part + term
            acc = acc + part[r:r + CONV_ROWS, :]
        mu = jnp.mean(acc, axis=-1, keepdims=True)
        d = acc - mu
        var = jnp.mean(d * d, axis=-1, keepdims=True)
        y = d * lax.rsqrt(var + LN_EPS) * ng_ref[...] + nb_ref[...]
        ybuf[r0:r0 + CONV_ROWS, :] = _silu(y).astype(BF16)
    o_ref[...] = (_dot(ybuf[...], pw2_ref[...]) * _silu(h_ref[:, 2 * w:3 * w])).astype(o_ref.dtype)


def _conv_branch(l, conv_in, conv_w, conv_b, norm_g, norm_b, w_pw2, batch, seq):
    bs = SEQ_BLOCK
    w = BRANCH_W
    return pl.pallas_call(
        _conv_kernel,
        grid=(batch, seq // bs),
        in_specs=[pl.BlockSpec((None, bs, 3 * w), lambda b, i: (b, i, 0)),
                  _resident(l, (CONV_W, w)), _resident(l, (1, w)), _resident(l, (1, w)), _resident(l, (1, w)),
                  _resident(l, (w, w))],
        out_specs=pl.BlockSpec((None, bs, w), lambda b, i: (b, i, 0)),
        out_shape=jax.ShapeDtypeStruct((batch, seq, w), BF16),
        scratch_shapes=[pltpu.VMEM((bs + CONV_HALO, w), F32), pltpu.VMEM((bs, w), BF16)],
        compiler_params=_params("parallel", "arbitrary"),
        name="conv_branch",
    )(conv_in.reshape(batch, seq, 3 * w), conv_w, conv_b, norm_g, norm_b, w_pw2)


def _cols(width, col, rows, per_batch):
    return pl.BlockSpec((rows, width), lambda b, i: (b * per_batch + i, col))


SSM_BOUND = SSM_CHUNK // SSM_SUB
assert SSM_BOUND % 8 == 0


def _ssm_kernel(u_ref, cz_ref, bc_ref, cc_ref, gc_ref, sc_ref, d_ref, wglu_ref, o_ref,
                carry, ubuf, ybuf, hbuf, bp_ref, cpt_ref, g_ref):
    i = pl.program_id(1)
    n = NSTATE
    w = BRANCH_W
    m = SSM_BOUND

    @pl.when((pl.program_id(0) == 0) & (i == 0))
    def _():
        shift = lambda a, k: lax.shift_right_logical(a, jnp.full(a.shape, k, jnp.int32))
        row_g = shift(lax.broadcasted_iota(jnp.int32, (w, n), 0), int(math.log2(SSM_GROUP)))
        col_g = shift(lax.broadcasted_iota(jnp.int32, (w, n), 1), int(math.log2(SSM_STATE)))
        same_group = row_g == col_g
        in_g = shift(lax.broadcasted_iota(jnp.int32, (w, w), 0), int(math.log2(SSM_GROUP)))
        out_g = shift(lax.broadcasted_iota(jnp.int32, (w, w), 1), int(math.log2(SSM_GROUP)))
        same_group_out = in_g == out_g
        for k in range(SSM_SUB):
            for c in range(2):
                for src, dst in ((bc_ref, bp_ref), (cc_ref, cpt_ref)):
                    wide = jnp.concatenate([src[k, c]] * (n // LANES), axis=1)
                    dst[k, :, c * n:(c + 1) * n] = jnp.where(same_group, wide, 0.0).astype(BF16)
            wide = jnp.concatenate([gc_ref[k]] * (w // LANES), axis=1)
            g_ref[k] = jnp.where(same_group_out, wide, 0.0).astype(BF16)

    @pl.when(i == 0)
    def _():
        carry[...] = jnp.zeros_like(carry)

    nt = w // LANES
    for k in range(nt):
        ubuf[k] = u_ref[:, k * LANES:(k + 1) * LANES]

    x = None
    for p in range(SSM_SUB):
        up = jnp.concatenate([ubuf[k, pl.ds(p, m, stride=SSM_SUB), :] for k in range(nt)], axis=1).astype(BF16)
        t = _dot(up, bp_ref[p])
        x = t if x is None else x + t
    tile = 8
    first = lax.broadcasted_iota(jnp.int32, (tile, n), 0) == 0
    cr = carry[0:1, :]
    ci = carry[1:2, :]
    for k in range(m // tile):
        tr = x[k * tile:(k + 1) * tile, 0:n]
        ti = x[k * tile:(k + 1) * tile, n:2 * n]
        for lvl in range(3):
            s = 1 << lvl
            ar = sc_ref[lvl, :, 0:n]
            ai = sc_ref[lvl, :, n:2 * n]
            sr = pltpu.roll(tr, s, 0)
            si = pltpu.roll(ti, s, 0)
            tr, ti = tr + (ar * sr - ai * si), ti + (ar * si + ai * sr)
        pr = sc_ref[3, :, 0:n]
        pi = sc_ref[3, :, n:2 * n]
        tr, ti = tr + (pr * cr - pi * ci), ti + (pr * ci + pi * cr)
        hbuf[k * tile:(k + 1) * tile, 0:n] = jnp.where(first, cr, pltpu.roll(tr, 1, 0))
        hbuf[k * tile:(k + 1) * tile, n:2 * n] = jnp.where(first, ci, pltpu.roll(ti, 1, 0))
        cr = tr[tile - 1:tile, :]
        ci = ti[tile - 1:tile, :]
    carry[0:1, :] = cr
    carry[1:2, :] = ci
    hcat = hbuf[...].astype(BF16)
    for p in range(SSM_SUB):
        yp = _dot_nt(hcat, cpt_ref[p])
        for k in range(nt):
            ybuf[k, pl.ds(p, m, stride=SSM_SUB), :] = yp[:, k * LANES:(k + 1) * LANES]

    rb = SSM_ROWS
    pos = lax.broadcasted_iota(jnp.int32, (rb, w), 0) % SSM_SUB
    for r0 in range(0, SSM_CHUNK, rb):
        u = u_ref[r0:r0 + rb, :]
        yb = jnp.concatenate([ybuf[k, r0:r0 + rb, :] for k in range(nt)], axis=1)
        y = yb + _dot(u.astype(BF16), g_ref[0])
        for d in range(1, SSM_SUB):
            ud = jnp.where(pos >= d, pltpu.roll(u, d, 0), 0.0)
            y = y + _dot(ud.astype(BF16), g_ref[d])
        y = jax.nn.gelu(y + d_ref[...] * u)
        g = _dot(y.astype(BF16), wglu_ref[...])
        out = g[:, 0:w] * _sigmoid(g[:, w:2 * w]) * _silu(cz_ref[r0:r0 + rb, :])
        o_ref[r0:r0 + rb, :] = out.astype(o_ref.dtype)


def _ssm_branch(l, ssm_in, bc, cc, gc, powers, d_skip, w_glu, batch, seq):
    rows = SSM_CHUNK
    w = BRANCH_W
    return pl.pallas_call(
        _ssm_kernel,
        grid=(batch, seq // rows),
        in_specs=[_cols(w, 0, rows, seq // rows), _cols(w, 1, rows, seq // rows),
                  _resident(l, (SSM_SUB, 2, w, LANES)), _resident(l, (SSM_SUB, 2, w, LANES)),
                  _resident(l, (SSM_SUB, w, LANES)),
                  _resident(l, (4, 8, 2 * NSTATE)), _resident(l, (1, w)), _resident(l, (w, 2 * w))],
        out_specs=pl.BlockSpec((None, rows, w), lambda b, i: (b, i, 0)),
        out_shape=jax.ShapeDtypeStruct((batch, seq, w), BF16),
        scratch_shapes=[pltpu.VMEM((8, NSTATE), F32), pltpu.VMEM((w // LANES, rows, LANES), F32),
                        pltpu.VMEM((w // LANES, rows, LANES), F32), pltpu.VMEM((SSM_BOUND, 2 * NSTATE), F32),
                        pltpu.VMEM((SSM_SUB, w, 2 * NSTATE), BF16), pltpu.VMEM((SSM_SUB, w, 2 * NSTATE), BF16),
                        pltpu.VMEM((SSM_SUB, w, w), BF16)],
        compiler_params=_params("arbitrary", "arbitrary"),
        name="ssm_branch",
    )(ssm_in, ssm_in, bc, cc, gc, powers, d_skip, w_glu)


SWA_GROUP = SWA_HEADS // SWA_KV_HEADS
SWA_BLOCKS = SEQ_BLOCK // WINDOW


SWA_VT_ROWS = SWA_HEAD_DIM + 16


def _swa_kernel(l, sink_ref, q_ref, kv_ref, prev_ref, dz_ref, o_ref, kbuf, vtbuf, ytbuf):
    i = pl.program_id(1)
    hd = SWA_HEAD_DIM
    kv_w = SWA_KV_HEADS * hd
    win = WINDOW
    cols = SWA_GROUP * win

    to_log2 = hd ** -0.5 * LOG2E
    kbuf[0:win, :] = prev_ref[:, 0:kv_w]
    kbuf[win:, :] = kv_ref[:, 0:kv_w]
    v_all = jnp.concatenate([prev_ref[:, kv_w:2 * kv_w], kv_ref[:, kv_w:2 * kv_w]], axis=0)
    vt = v_all.astype(F32).T
    ones = jnp.ones((SWA_VT_ROWS - hd, vt.shape[1]), BF16)
    for kvh in range(SWA_KV_HEADS):
        vtbuf[kvh * SWA_VT_ROWS:kvh * SWA_VT_ROWS + hd, :] = vt[kvh * hd:(kvh + 1) * hd, :].astype(BF16)
        vtbuf[kvh * SWA_VT_ROWS + hd:(kvh + 1) * SWA_VT_ROWS, :] = ones

    krow = lax.broadcasted_iota(jnp.int32, (2 * win, cols), 0)
    qpos = lax.broadcasted_iota(jnp.int32, (2 * win, cols), 1) % win + win
    rel = qpos - krow
    band = (rel >= 0) & (rel < win)
    first_head = lax.broadcasted_iota(jnp.int32, (1, cols), 1) < win
    kmin = jnp.where(i == 0, win, 0)

    scores = {}
    for j in range(SWA_BLOCKS):
        for kvh in range(SWA_KV_HEADS):
            q2 = jnp.concatenate(
                [q_ref[j * win:(j + 1) * win, (kvh * SWA_GROUP + g) * hd:(kvh * SWA_GROUP + g + 1) * hd]
                 for g in range(SWA_GROUP)], axis=0)
            k2 = kbuf[j * win:(j + 2) * win, kvh * hd:(kvh + 1) * hd]
            scores[j, kvh] = _dot_nt(k2, q2) * to_log2
    for j in range(SWA_BLOCKS):
        valid = band & (krow >= kmin) if j == 0 else band
        for kvh in range(SWA_KV_HEADS):
            s = jnp.where(valid, scores[j, kvh], NEG_BIG)
            sink = jnp.where(first_head, sink_ref[l, kvh * SWA_GROUP] * LOG2E,
                             sink_ref[l, kvh * SWA_GROUP + 1] * LOG2E)
            m = jnp.maximum(jnp.max(s, axis=0, keepdims=True), sink)
            p = jnp.exp2(s - m).astype(BF16)
            acc = _dot(vtbuf[kvh * SWA_VT_ROWS:(kvh + 1) * SWA_VT_ROWS, j * win:(j + 2) * win], p)
            denom = acc[hd:hd + 1, :] + jnp.exp2(sink - m)
            o = acc[0:hd, :] * (1.0 / denom)
            for g in range(SWA_GROUP):
                h = kvh * SWA_GROUP + g
                ytbuf[h * hd:(h + 1) * hd, j * win:(j + 1) * win] = o[:, g * win:(g + 1) * win]
    o_ref[...] = (ytbuf[...].T * _silu(dz_ref[...])).astype(o_ref.dtype)


def _swa_branch(l, swa_in, dz, sinks, batch, seq):
    bs = SEQ_BLOCK
    w = BRANCH_W
    nsb = seq // bs
    wins = seq // WINDOW
    prev_spec = pl.BlockSpec((WINDOW, w), lambda b, i: (b * wins + jnp.maximum(i * SWA_BLOCKS - 1, 0), 1))
    return pl.pallas_call(
        functools.partial(_swa_kernel, l),
        grid=(batch, nsb),
        in_specs=[pl.BlockSpec(memory_space=pltpu.SMEM),
                  _cols(w, 0, bs, nsb), _cols(w, 1, bs, nsb), prev_spec, _cols(w, 0, bs, nsb)],
        out_specs=pl.BlockSpec((None, bs, w), lambda b, i: (b, i, 0)),
        out_shape=jax.ShapeDtypeStruct((batch, seq, w), BF16),
        scratch_shapes=[pltpu.VMEM((bs + WINDOW, SWA_KV_HEADS * SWA_HEAD_DIM), BF16),
                        pltpu.VMEM((SWA_KV_HEADS * SWA_VT_ROWS, bs + WINDOW), BF16), pltpu.VMEM((w, bs), F32)],
        compiler_params=_params("parallel", "arbitrary"),
        name="swa_branch",
    )(sinks, swa_in, swa_in, swa_in, dz)


MLA_QK = MLA_HEADS * MLA_SLOT


def _rms(x, g):
    ms = jnp.mean(x * x, axis=-1, keepdims=True)
    return x * lax.rsqrt(ms + RMS_EPS) * g


def _mla_prep_kernel(cq_ref, ckv_ref, krot_ref, cos_ref, sin_ref, qg_ref, kvg_ref, wuq_ref, wukv_ref,
                     qt_ref, k_ref, vt_ref):
    scale = (MLA_NOPE + MLA_ROPE) ** -0.5 * LOG2E
    cq = _rms(cq_ref[...], qg_ref[...]).astype(BF16)
    ckv = _rms(ckv_ref[:, 0:MLA_KV_RANK], kvg_ref[...]).astype(BF16)
    cos = cos_ref[...]
    sin = sin_ref[...]
    k_rope = ckv_ref[:, MLA_KV_RANK:MLA_KV_RANK + LANES] * cos + krot_ref[...] * sin
    kv = _dot(ckv, wukv_ref[...])
    for h in range(MLA_HEADS):
        lo = h * MLA_SLOT
        qm = _dot(cq, wuq_ref[:, lo:lo + MLA_SLOT])
        qr = _dot(cq, wuq_ref[:, MLA_QK + lo:MLA_QK + lo + MLA_SLOT])
        qt_ref[lo:lo + MLA_SLOT, :] = ((qm * cos + qr * sin) * scale).T.astype(BF16)
        k_ref[:, lo:lo + MLA_SLOT] = (kv[:, lo:lo + MLA_SLOT] + k_rope).astype(BF16)
    vt = kv[:, MLA_QK:].T.astype(BF16)
    ones = jnp.ones((MLA_VT_ROWS - MLA_V, vt.shape[1]), BF16)
    for h in range(MLA_HEADS):
        vt_ref[h * MLA_VT_ROWS:h * MLA_VT_ROWS + MLA_V, :] = vt[h * MLA_V:(h + 1) * MLA_V, :]
        vt_ref[h * MLA_VT_ROWS + MLA_V:(h + 1) * MLA_VT_ROWS, :] = ones


def _mla_prep(l, mla_in, cos_t, sin_t, q_norm_g, kv_norm_g, w_uq_ext, w_ukv_ext, seq):
    t = mla_in.shape[0]
    tm = ATTN_BLK
    nsb = seq // tm
    vw = MLA_HEADS * MLA_V
    vt_rows = MLA_HEADS * MLA_VT_ROWS
    return pl.pallas_call(
        _mla_prep_kernel,
        grid=(t // tm,),
        in_specs=[pl.BlockSpec((tm, MLA_Q_RANK), lambda i: (i, 0)),
                  pl.BlockSpec((tm, MLA_Q_RANK), lambda i: (i, 1)),
                  pl.BlockSpec((tm, LANES), lambda i: (i, 2 * MLA_Q_RANK // LANES)),
                  pl.BlockSpec((tm, LANES), lambda i: (i % nsb, 0)),
                  pl.BlockSpec((tm, LANES), lambda i: (i % nsb, 0)),
                  _resident(l, (1, MLA_Q_RANK)), _resident(l, (1, MLA_KV_RANK)),
                  _resident(l, (MLA_Q_RANK, 2 * MLA_QK)), _resident(l, (MLA_KV_RANK, MLA_QK + vw))],
        out_specs=[pl.BlockSpec((None, MLA_QK, tm), lambda i: (i, 0, 0)),
                   pl.BlockSpec((tm, MLA_QK), lambda i: (i, 0)),
                   pl.BlockSpec((None, vt_rows, tm), lambda i: (i, 0, 0))],
        out_shape=[jax.ShapeDtypeStruct((t // tm, MLA_QK, tm), BF16), jax.ShapeDtypeStruct((t, MLA_QK), BF16),
                   jax.ShapeDtypeStruct((t // tm, vt_rows, tm), BF16)],
        compiler_params=_params("parallel"),
        name="mla_prep",
    )(mla_in, mla_in, mla_in, cos_t, sin_t, q_norm_g, kv_norm_g, w_uq_ext, w_ukv_ext)


def _attn_kernel(qt_ref, k_ref, vt_ref, bz_ref, o_ref, m_sc, acc_sc, s_sc):
    i = pl.program_id(1)
    blk = ATTN_BLK
    krow = lax.broadcasted_iota(jnp.int32, (blk, blk), 0)
    qcol = lax.broadcasted_iota(jnp.int32, (blk, blk), 1)
    causal = krow <= qcol
    m_sc[...] = jnp.full(m_sc.shape, NEG_BIG, F32)
    acc_sc[...] = jnp.zeros(acc_sc.shape, F32)

    def score(j, slot):
        start = j * blk if isinstance(j, int) else pl.multiple_of(j * blk, blk)
        for h in range(MLA_HEADS):
            kb = k_ref[pl.ds(start, blk), h * MLA_SLOT:(h + 1) * MLA_SLOT]
            s_sc[slot, h] = _dot(kb, qt_ref[h * MLA_SLOT:(h + 1) * MLA_SLOT, :])

    def absorb(j, slot, masked):
        for h in range(MLA_HEADS):
            s = s_sc[slot, h]
            if masked:
                s = jnp.where(causal, s, NEG_BIG)
            m_prev = m_sc[h]
            m_new = jnp.maximum(m_prev, jnp.max(s, axis=0, keepdims=True))
            alpha = jnp.exp2(m_prev - m_new)
            p = jnp.exp2(s - m_new).astype(BF16)
            vb = vt_ref[j, h * MLA_VT_ROWS:(h + 1) * MLA_VT_ROWS, :]
            acc_sc[h] = alpha * acc_sc[h] + _dot(vb, p)
            m_sc[h] = m_new

    score(0, 0)

    def body(t, c):
        j = 2 * t
        score(j + 1, 1)
        absorb(j, 0, False)
        score(j + 2, 0)
        absorb(j + 1, 1, False)
        return c

    lax.fori_loop(0, i // 2, body, 0)

    @pl.when(i % 2 == 0)
    def _():
        absorb(i, 0, True)

    @pl.when(i % 2 == 1)
    def _():
        score(i, 1)
        absorb(i - 1, 0, False)
        absorb(i, 1, True)
    outs = [acc_sc[h, 0:MLA_V, :] * (1.0 / acc_sc[h, MLA_V:MLA_V + 1, :]) for h in range(MLA_HEADS)]
    y = jnp.concatenate(outs, axis=0).T
    o_ref[...] = (y * _silu(bz_ref[...])).astype(o_ref.dtype)


def _mla_attention(qt, k, vt, bz, batch, seq):
    blk = ATTN_BLK
    nq = seq // blk
    vt_rows = MLA_HEADS * MLA_VT_ROWS
    return pl.pallas_call(
        _attn_kernel,
        grid=(batch, nq),
        in_specs=[pl.BlockSpec((None, MLA_QK, blk), lambda b, i: (b * nq + i, 0, 0)),
                  pl.BlockSpec((None, seq, MLA_QK), lambda b, i: (b, 0, 0)),
                  pl.BlockSpec((None, nq, vt_rows, blk), lambda b, i: (b, 0, 0, 0)),
                  _cols(BRANCH_W, 0, blk, nq)],
        out_specs=pl.BlockSpec((None, blk, BRANCH_W), lambda b, i: (b, i, 0)),
        out_shape=jax.ShapeDtypeStruct((batch, seq, BRANCH_W), BF16),
        scratch_shapes=[pltpu.VMEM((MLA_HEADS, 1, blk), F32), pltpu.VMEM((MLA_HEADS, MLA_VT_ROWS, blk), F32),
                        pltpu.VMEM((2, MLA_HEADS, blk, blk), F32)],
        compiler_params=_params("parallel", "arbitrary"),
        name="mla_attention",
    )(qt, k.reshape(batch, seq, MLA_QK), vt.reshape(batch, nq, vt_rows, blk), bz)


def _merge_kernel(x_ref, ya_ref, yb_ref, yc_ref, yd_ref, p_ref, wm_ref, bm_ref, wb_ref, wo_ref,
                  lng_ref, lnb_ref, wp_ref, wpg_ref, pg_ref, o_ref):
    d = D_MODEL
    x = x_ref[...]
    xb = x.astype(BF16)
    merged = None
    for n, y_ref in enumerate((ya_ref, yb_ref, yc_ref, yd_ref)):
        gate = _sigmoid(_dot(xb, wm_ref[:, n * d:(n + 1) * d]) + bm_ref[:, n * d:(n + 1) * d])
        term = gate * _dot(y_ref[...], wb_ref[n])
        merged = term if merged is None else merged + term
    z = DEEPNORM_ALPHA * x + _dot(merged.astype(BF16), wo_ref[...])
    mu = jnp.mean(z, axis=-1, keepdims=True)
    zc = z - mu
    var = jnp.mean(zc * zc, axis=-1, keepdims=True)
    xn = zc * lax.rsqrt(var + LN_EPS) * lng_ref[...] + lnb_ref[...]
    e = _dot(p_ref[...].astype(BF16), wp_ref[...]) * _sigmoid(_dot(xn.astype(BF16), wpg_ref[...]))
    o_ref[...] = xn + _rms(e, pg_ref[...])


def _merge(l, x2d, ys, p3d, w_merge, b_merge, w_branch, w_out, ln_g, ln_b, w_ple, w_ple_gate, ple_norm_g):
    t = x2d.shape[0]
    tm = MERGE_TM
    d = D_MODEL
    row_spec = lambda width: pl.BlockSpec((tm, width), lambda i: (i, 0))
    return pl.pallas_call(
        _merge_kernel,
        grid=(t // tm,),
        in_specs=[row_spec(d)] + [row_spec(BRANCH_W)] * N_BRANCH + [
                  pl.BlockSpec((None, tm, PLE_DIM), lambda i: (l, i, 0)),
                  _resident(l, (d, N_BRANCH * d)), _resident(l, (1, N_BRANCH * d)),
                  _resident(l, (N_BRANCH, BRANCH_W, d)), _resident(l, (d, d)),
                  _resident(l, (1, d)), _resident(l, (1, d)), _resident(l, (PLE_DIM, d)), _resident(l, (d, d)),
                  _resident(l, (1, d))],
        out_specs=row_spec(d),
        out_shape=jax.ShapeDtypeStruct((t, d), F32),
        compiler_params=_params("parallel"),
        name="merge",
    )(x2d, *ys, p3d, w_merge, b_merge, w_branch, w_out, ln_g, ln_b, w_ple, w_ple_gate, ple_norm_g)


def _rot_cols(w):
    half = w.shape[-1] // 2
    return jnp.concatenate([-w[..., half:], w[..., :half]], axis=-1)


def _zero_cols(w, n):
    return jnp.zeros(w.shape[:-1] + (n,), w.dtype)


def _rope_slot(w):
    return jnp.concatenate([_zero_cols(w, MLA_NOPE), w, _zero_cols(w, MLA_SLOT - MLA_NOPE - MLA_ROPE)], axis=-1)


def _in_weights(w_in):
    o = [0]
    for s in (256, 256, 256, 256, 128, 32, 256, 256, 256, 256, 128, 128, 256):
        o.append(o[-1] + s)
    col = lambda k: w_in[..., o[k]:o[k + 1]]
    a_val, a_gate, a_z, c_q, c_kv, k_r, b_z, u, c_z, q, k, v, d_z = (col(k) for k in range(13))
    groups = [a_val, a_gate, a_z, c_q, c_kv, _rope_slot(k_r), _rope_slot(_rot_cols(k_r)), b_z, u, c_z, q, k, v, d_z]
    return jnp.concatenate(groups, axis=-1).astype(BF16)


def _uq_weights(w_uq):
    hd = MLA_NOPE + MLA_ROPE
    main, rot = [], []
    for h in range(MLA_HEADS):
        head = w_uq[..., h * hd:(h + 1) * hd]
        main.append(jnp.concatenate([head, _zero_cols(w_uq, MLA_SLOT - hd)], axis=-1))
        rot.append(_rope_slot(_rot_cols(head[..., MLA_NOPE:])))
    return jnp.concatenate(main + rot, axis=-1).astype(BF16)


def _ukv_weights(w_ukv):
    hd = MLA_NOPE + MLA_V
    ks, vs = [], []
    for h in range(MLA_HEADS):
        ks.append(jnp.concatenate([w_ukv[..., h * hd:h * hd + MLA_NOPE], _zero_cols(w_ukv, MLA_SLOT - MLA_NOPE)],
                                  axis=-1))
        vs.append(w_ukv[..., h * hd + MLA_NOPE:(h + 1) * hd])
    return jnp.concatenate(ks + vs, axis=-1).astype(BF16)


def _rope_tables(seq):
    pos = jnp.arange(seq, dtype=F32)
    inv_freq = ROPE_THETA ** (-jnp.arange(0, MLA_ROPE, 2, dtype=F32) / MLA_ROPE)
    ang = pos[:, None] * inv_freq[None, :]
    cos, sin = jnp.cos(ang), jnp.sin(ang)
    ones = jnp.ones((seq, MLA_NOPE), F32)
    zeros = jnp.zeros((seq, MLA_NOPE), F32)
    pad = jnp.zeros((seq, MLA_SLOT - MLA_NOPE - MLA_ROPE), F32)
    cos_t = jnp.concatenate([ones, cos, cos, pad], axis=1)
    sin_t = jnp.concatenate([zeros, sin, sin, pad], axis=1)
    return cos_t, sin_t


def _ssm_weights(a_re, a_im, log_dt, b_re, b_im, c_re, c_im):
    g, p, h = SSM_GROUPS, SSM_STATE, SSM_GROUP
    nl = a_re.shape[0]
    dt = jnp.exp(log_dt.astype(F32))[..., None]
    lr, li = a_re.astype(F32), a_im.astype(F32)
    mag = jnp.exp(lr * dt)
    lb_re, lb_im = mag * jnp.cos(li * dt), mag * jnp.sin(li * dt)
    den = lr * lr + li * li
    nr, ni = lb_re - 1.0, lb_im
    f_re = ((nr * lr + ni * li) / den)[..., None]
    f_im = ((ni * lr - nr * li) / den)[..., None]
    bb_re = (f_re * b_re - f_im * b_im)[:, None]
    bb_im = (f_re * b_im + f_im * b_re)[:, None]
    c_re, c_im = c_re.astype(F32)[:, None], c_im.astype(F32)[:, None]
    prs, pis = [jnp.ones_like(lb_re)], [jnp.zeros_like(lb_im)]
    for _ in range(SSM_SUB):
        qr, qi = prs[-1], pis[-1]
        prs.append(qr * lb_re - qi * lb_im)
        pis.append(qr * lb_im + qi * lb_re)
    pw_r, pw_i = jnp.stack(prs, axis=1), jnp.stack(pis, axis=1)
    rows_gh = lambda m: m.reshape(nl, SSM_SUB, g * h, m.shape[-1])
    lanes = lambda m: jnp.tile(m, LANES // m.shape[-1])
    qr, qi = pw_r[:, :SSM_SUB, :, :, None], pw_i[:, :SSM_SUB, :, :, None]
    vr = jnp.swapaxes(qr * bb_re - qi * bb_im, -1, -2)
    vi = jnp.swapaxes(qr * bb_im + qi * bb_re, -1, -2)
    bc = lanes(jnp.stack([rows_gh(vr), rows_gh(vi)], axis=2))[:, ::-1]
    qr, qi = pw_r[:, 1:, :, None, :], pw_i[:, 1:, :, None, :]
    cc = lanes(jnp.stack([rows_gh(c_re * qr - c_im * qi), -rows_gh(c_re * qi + c_im * qr)], axis=2))
    lag = jnp.sum(vr[..., None, :] * c_re[:, :, :, None] - vi[..., None, :] * c_im[:, :, :, None], axis=-1)
    gc = lanes(rows_gh(lag))
    mr, mi = pw_r[:, SSM_SUB].reshape(nl, 1, g * p), pw_i[:, SSM_SUB].reshape(nl, 1, g * p)
    mu = [(mr, mi)]
    for _ in range(7):
        qr, qi = mu[-1]
        mu.append((qr * mr - qi * mi, qr * mi + qi * mr))
    cat = lambda q: jnp.concatenate(q, axis=-1)
    rowid = jnp.arange(8)[:, None]
    tables = [jnp.where(rowid >= (1 << lvl), cat(mu[(1 << lvl) - 1]), 0.0) for lvl in range(3)]
    tables.append(jnp.concatenate([cat(q) for q in mu], axis=1))
    return bc, cc, gc, jnp.stack(tables, axis=1)


def kernel(x, p, w_in, w_merge, b_merge, conv_w, conv_b, conv_norm_g, conv_norm_b, w_pw2, mla_q_norm_g,
           mla_kv_norm_g, w_uq, w_ukv, ssm_a_re, ssm_a_im, ssm_log_dt, ssm_b_re, ssm_b_im, ssm_c_re,
           ssm_c_im, ssm_d, w_glu, attn_sinks, w_branch, w_out, ln_g, ln_b, w_ple, w_ple_gate, ple_norm_g):
    batch, seq, d = x.shape
    t = batch * seq
    rows = lambda a: a.reshape(DEPTH, 1, -1).astype(F32)
    bf = lambda a: a.astype(BF16)
    cos_t, sin_t = _rope_tables(seq)
    w_in_ext, w_uq_ext, w_ukv_ext = _in_weights(w_in), _uq_weights(w_uq), _ukv_weights(w_ukv)
    bp, cp, gd, sc = _ssm_weights(ssm_a_re, ssm_a_im, ssm_log_dt, ssm_b_re, ssm_b_im, ssm_c_re, ssm_c_im)
    conv_p = (conv_w.astype(F32), rows(conv_b), rows(conv_norm_g), rows(conv_norm_b), bf(w_pw2))
    mla_p = (rows(mla_q_norm_g), rows(mla_kv_norm_g), w_uq_ext, w_ukv_ext)
    ssm_p = (bp, cp, gd, sc, rows(ssm_d), bf(w_glu))
    merge_p = (bf(w_merge), rows(b_merge), bf(w_branch), bf(w_out), rows(ln_g), rows(ln_b), bf(w_ple),
               bf(w_ple_gate), rows(ple_norm_g))
    sinks = attn_sinks.astype(F32)
    p3d = p.reshape(DEPTH, t, PLE_DIM)
    x2d = x.reshape(t, d)
    for l in range(DEPTH):
        conv_in, mla_in, bz, ssm_in, swa_in, dz = _in_proj(l, x2d, w_in_ext)
        y_a = _conv_branch(l, conv_in, *conv_p, batch, seq)
        qt, k, vt = _mla_prep(l, mla_in, cos_t, sin_t, *mla_p, seq)
        y_b = _mla_attention(qt, k, vt, bz, batch, seq)
        y_c = _ssm_branch(l, ssm_in, *ssm_p, batch, seq)
        y_d = _swa_branch(l, swa_in, dz, sinks, batch, seq)
        ys = [y.reshape(t, BRANCH_W) for y in (y_a, y_b, y_c, y_d)]
        x2d = _merge(l, x2d, ys, p3d, *merge_p)
    return x2d.reshape(batch, seq, d)
```

```python
import functools
import math

import jax
import jax.numpy as jnp
from jax import lax
from jax.experimental import pallas as pl
from jax.experimental.pallas import tpu as pltpu

D_MODEL = 1024
DEPTH = 4
PLE_DIM = 256
N_BRANCH = 4
BRANCH_W = 256
CONV_W = 31
MLA_HEADS = 4
MLA_NOPE = 64
MLA_ROPE = 32
MLA_V = 64
MLA_Q_RANK = 256
MLA_KV_RANK = 128
ROPE_THETA = 10000.0
SSM_GROUP = 16
SSM_GROUPS = 16
SSM_STATE = 64
SWA_HEADS = 4
SWA_KV_HEADS = 2
SWA_HEAD_DIM = 64
WINDOW = 128
DEEPNORM_ALPHA = (2.0 * DEPTH) ** 0.25
LN_EPS = 1e-5
RMS_EPS = 1e-6

LANES = 128
VMEM_LIMIT_BYTES = 56 * 1024 * 1024

IN_TM = 512
SEQ_BLOCK = 512
SSM_CHUNK = 2048
SSM_SUB = 8
SSM_ROWS = 1024
ATTN_BLK = 512
MERGE_TM = 512
CONV_HALO = 32
CONV_ROWS = 64

MLA_SLOT = LANES
MLA_VT_ROWS = MLA_V + 16
LOG2E = math.log2(math.e)
NSTATE = SSM_GROUPS * SSM_STATE
NEG_BIG = -1e30

F32 = jnp.float32
BF16 = jnp.bfloat16


def _params(*sem):
    return pltpu.CompilerParams(dimension_semantics=sem, vmem_limit_bytes=VMEM_LIMIT_BYTES)


def _resident(l, shape):
    nd = len(shape)
    return pl.BlockSpec((None,) + tuple(shape), lambda *_: (l,) + (0,) * nd, pipeline_mode=pl.Buffered(1))


def _dot(a, b):
    return jnp.dot(a, b, preferred_element_type=F32)


def _dot_nt(a, b):
    return lax.dot_general(a, b, (((1,), (1,)), ((), ())), preferred_element_type=F32)


def _sigmoid(x):
    return 1.0 / (1.0 + jnp.exp(-x))


def _silu(x):
    return x * _sigmoid(x)


IN_GROUPS = (
    ("conv", 3 * BRANCH_W, F32),
    ("mla", MLA_Q_RANK + MLA_KV_RANK + 2 * LANES, F32),
    ("bz", BRANCH_W, F32),
    ("ssm", 2 * BRANCH_W, F32),
    ("swa", 2 * BRANCH_W, BF16),
    ("dz", BRANCH_W, F32),
)
IN_EXT_WIDTH = sum(w for _, w, _ in IN_GROUPS)


def _in_proj_kernel(x_ref, w_ref, *out_refs):
    xb = x_ref[...].astype(BF16)
    start = 0
    for (_, width, dtype), o_ref in zip(IN_GROUPS, out_refs):
        o_ref[...] = _dot(xb, w_ref[:, start:start + width]).astype(dtype)
        start += width


def _in_proj(l, x2d, w_in_ext):
    t = x2d.shape[0]
    return pl.pallas_call(
        _in_proj_kernel,
        grid=(t // IN_TM,),
        in_specs=[pl.BlockSpec((IN_TM, D_MODEL), lambda i: (i, 0)),
                  _resident(l, (D_MODEL, IN_EXT_WIDTH))],
        out_specs=[pl.BlockSpec((IN_TM, w), lambda i: (i, 0)) for _, w, _ in IN_GROUPS],
        out_shape=[jax.ShapeDtypeStruct((t, w), d) for _, w, d in IN_GROUPS],
        compiler_params=_params("parallel"),
        name="in_proj",
    )(x2d, w_in_ext)


def _conv_kernel(h_ref, cw_ref, cb_ref, ng_ref, nb_ref, pw2_ref, o_ref, hbuf, ybuf):
    i = pl.program_id(1)
    bs = SEQ_BLOCK
    w = BRANCH_W

    @pl.when(i == 0)
    def _():
        hbuf[0:CONV_HALO, :] = jnp.zeros((CONV_HALO, w), F32)

    @pl.when(i > 0)
    def _():
        hbuf[0:CONV_HALO, :] = hbuf[bs:bs + CONV_HALO, :]

    hbuf[CONV_HALO:, :] = h_ref[:, 0:w] * _sigmoid(h_ref[:, w:2 * w])

    base = CONV_HALO - (CONV_W - 1)
    tile = 8

    for c in range(bs // CONV_ROWS):
        r0 = c * CONV_ROWS
        acc = jnp.broadcast_to(cb_ref[...], (CONV_ROWS, w))
        for r in range(tile):
            span = CONV_ROWS + (tile if r else 0)
            part = None
            for j in range((r - base) % tile, CONV_W, tile):
                lo = r0 + base + j - r
                term = cw_ref[j:j + 1, :] * hbuf[lo:lo + span, :]
                part = term if part is None else part + term
            acc = acc + part[r:r + CONV_ROWS, :]
        mu = jnp.mean(acc, axis=-1, keepdims=True)
        d = acc - mu
        var = jnp.mean(d * d, axis=-1, keepdims=True)
        y = d * lax.rsqrt(var + LN_EPS) * ng_ref[...] + nb_ref[...]
        ybuf[r0:r0 + CONV_ROWS, :] = _silu(y).astype(BF16)
    o_ref[...] = (_dot(ybuf[...], pw2_ref[...]) * _silu(h_ref[:, 2 * w:3 * w])).astype(o_ref.dtype)


def _conv_branch(l, conv_in, conv_w, conv_b, norm_g, norm_b, w_pw2, batch, seq):
    bs = SEQ_BLOCK
    w = BRANCH_W
    return pl.pallas_call(
        _conv_kernel,
        grid=(batch, seq // bs),
        in_specs=[pl.BlockSpec((None, bs, 3 * w), lambda b, i: (b, i, 0)),
                  _resident(l, (CONV_W, w)), _resident(l, (1, w)), _resident(l, (1, w)), _resident(l, (1, w)),
                  _resident(l, (w, w))],
        out_specs=pl.BlockSpec((None, bs, w), lambda b, i: (b, i, 0)),
        out_shape=jax.ShapeDtypeStruct((batch, seq, w), BF16),
        scratch_shapes=[pltpu.VMEM((bs + CONV_HALO, w), F32), pltpu.VMEM((bs, w), BF16)],
        compiler_params=_params("parallel", "arbitrary"),
        name="conv_branch",
    )(conv_in.reshape(batch, seq, 3 * w), conv_w, conv_b, norm_g, norm_b, w_pw2)


def _cols(width, col, rows, per_batch):
    return pl.BlockSpec((rows, width), lambda b, i: (b * per_batch + i, col))


SSM_BOUND = SSM_CHUNK // SSM_SUB
assert SSM_BOUND % 8 == 0


def _ssm_kernel(u_ref, cz_ref, bc_ref, cc_ref, gc_ref, sc_ref, d_ref, wglu_ref, o_ref,
                carry, ubuf, ybuf, hbuf, bp_ref, cpt_ref, g_ref):
    i = pl.program_id(1)
    n = NSTATE
    w = BRANCH_W
    m = SSM_BOUND

    @pl.when((pl.program_id(0) == 0) & (i == 0))
    def _():
        shift = lambda a, k: lax.shift_right_logical(a, jnp.full(a.shape, k, jnp.int32))
        row_g = shift(lax.broadcasted_iota(jnp.int32, (w, n), 0), int(math.log2(SSM_GROUP)))
        col_g = shift(lax.broadcasted_iota(jnp.int32, (w, n), 1), int(math.log2(SSM_STATE)))
        same_group = row_g == col_g
        in_g = shift(lax.broadcasted_iota(jnp.int32, (w, w), 0), int(math.log2(SSM_GROUP)))
        out_g = shift(lax.broadcasted_iota(jnp.int32, (w, w), 1), int(math.log2(SSM_GROUP)))
        same_group_out = in_g == out_g
        for k in range(SSM_SUB):
            for c in range(2):
                for src, dst in ((bc_ref, bp_ref), (cc_ref, cpt_ref)):
                    wide = jnp.concatenate([src[k, c]] * (n // LANES), axis=1)
                    dst[k, :, c * n:(c + 1) * n] = jnp.where(same_group, wide, 0.0).astype(BF16)
            wide = jnp.concatenate([gc_ref[k]] * (w // LANES), axis=1)
            g_ref[k] = jnp.where(same_group_out, wide, 0.0).astype(BF16)

    @pl.when(i == 0)
    def _():
        carry[...] = jnp.zeros_like(carry)

    nt = w // LANES
    for k in range(nt):
        ubuf[k] = u_ref[:, k * LANES:(k + 1) * LANES]

    x = None
    for p in range(SSM_SUB):
        up = jnp.concatenate([ubuf[k, pl.ds(p, m, stride=SSM_SUB), :] for k in range(nt)], axis=1).astype(BF16)
        t = _dot(up, bp_ref[p])
        x = t if x is None else x + t
    tile = 8
    first = lax.broadcasted_iota(jnp.int32, (tile, n), 0) == 0
    cr = carry[0:1, :]
    ci = carry[1:2, :]

    rb = SSM_ROWS
    nrb = SSM_CHUNK // rb
    pos = lax.broadcasted_iota(jnp.int32, (rb, w), 0) % SSM_SUB
    lags = []

    def lag_block(r0):
        u = u_ref[r0:r0 + rb, :]
        y = _dot(u.astype(BF16), g_ref[0])
        for d in range(1, SSM_SUB):
            ud = jnp.where(pos >= d, pltpu.roll(u, d, 0), 0.0)
            y = y + _dot(ud.astype(BF16), g_ref[d])
        return y

    tiles_per_block = (m // tile) // nrb
    for k in range(m // tile):
        if k % tiles_per_block == 0:
            lags.append(lag_block((k // tiles_per_block) * rb))
        tr = x[k * tile:(k + 1) * tile, 0:n]
        ti = x[k * tile:(k + 1) * tile, n:2 * n]
        for lvl in range(3):
            s = 1 << lvl
            ar = sc_ref[lvl, :, 0:n]
            ai = sc_ref[lvl, :, n:2 * n]
            sr = pltpu.roll(tr, s, 0)
            si = pltpu.roll(ti, s, 0)
            tr, ti = tr + (ar * sr - ai * si), ti + (ar * si + ai * sr)
        pr = sc_ref[3, :, 0:n]
        pi = sc_ref[3, :, n:2 * n]
        tr, ti = tr + (pr * cr - pi * ci), ti + (pr * ci + pi * cr)
        hbuf[k * tile:(k + 1) * tile, 0:n] = jnp.where(first, cr, pltpu.roll(tr, 1, 0))
        hbuf[k * tile:(k + 1) * tile, n:2 * n] = jnp.where(first, ci, pltpu.roll(ti, 1, 0))
        cr = tr[tile - 1:tile, :]
        ci = ti[tile - 1:tile, :]
    carry[0:1, :] = cr
    carry[1:2, :] = ci
    hcat = hbuf[...].astype(BF16)
    for p in range(SSM_SUB):
        yp = _dot_nt(hcat, cpt_ref[p])
        for k in range(nt):
            ybuf[k, pl.ds(p, m, stride=SSM_SUB), :] = yp[:, k * LANES:(k + 1) * LANES]

    for r in range(nrb):
        r0 = r * rb
        u = u_ref[r0:r0 + rb, :]
        yb = jnp.concatenate([ybuf[k, r0:r0 + rb, :] for k in range(nt)], axis=1)
        y = jax.nn.gelu(yb + lags[r] + d_ref[...] * u)
        g = _dot(y.astype(BF16), wglu_ref[...])
        out = g[:, 0:w] * _sigmoid(g[:, w:2 * w]) * _silu(cz_ref[r0:r0 + rb, :])
        o_ref[r0:r0 + rb, :] = out.astype(o_ref.dtype)


def _ssm_branch(l, ssm_in, bc, cc, gc, powers, d_skip, w_glu, batch, seq):
    rows = SSM_CHUNK
    w = BRANCH_W
    return pl.pallas_call(
        _ssm_kernel,
        grid=(batch, seq // rows),
        in_specs=[_cols(w, 0, rows, seq // rows), _cols(w, 1, rows, seq // rows),
                  _resident(l, (SSM_SUB, 2, w, LANES)), _resident(l, (SSM_SUB, 2, w, LANES)),
                  _resident(l, (SSM_SUB, w, LANES)),
                  _resident(l, (4, 8, 2 * NSTATE)), _resident(l, (1, w)), _resident(l, (w, 2 * w))],
        out_specs=pl.BlockSpec((None, rows, w), lambda b, i: (b, i, 0)),
        out_shape=jax.ShapeDtypeStruct((batch, seq, w), BF16),
        scratch_shapes=[pltpu.VMEM((8, NSTATE), F32), pltpu.VMEM((w // LANES, rows, LANES), F32),
                        pltpu.VMEM((w // LANES, rows, LANES), F32), pltpu.VMEM((SSM_BOUND, 2 * NSTATE), F32),
                        pltpu.VMEM((SSM_SUB, w, 2 * NSTATE), BF16), pltpu.VMEM((SSM_SUB, w, 2 * NSTATE), BF16),
                        pltpu.VMEM((SSM_SUB, w, w), BF16)],
        compiler_params=_params("arbitrary", "arbitrary"),
        name="ssm_branch",
    )(ssm_in, ssm_in, bc, cc, gc, powers, d_skip, w_glu)


SWA_GROUP = SWA_HEADS // SWA_KV_HEADS
SWA_BLOCKS = SEQ_BLOCK // WINDOW


SWA_VT_ROWS = SWA_HEAD_DIM + 16


def _swa_kernel(l, sink_ref, q_ref, kv_ref, prev_ref, dz_ref, o_ref, kbuf, vtbuf, ytbuf):
    i = pl.program_id(1)
    hd = SWA_HEAD_DIM
    kv_w = SWA_KV_HEADS * hd
    win = WINDOW
    cols = SWA_GROUP * win

    to_log2 = hd ** -0.5 * LOG2E
    kbuf[0:win, :] = prev_ref[:, 0:kv_w]
    kbuf[win:, :] = kv_ref[:, 0:kv_w]
    v_all = jnp.concatenate([prev_ref[:, kv_w:2 * kv_w], kv_ref[:, kv_w:2 * kv_w]], axis=0)
    vt = v_all.astype(F32).T
    ones = jnp.ones((SWA_VT_ROWS - hd, vt.shape[1]), BF16)
    for kvh in range(SWA_KV_HEADS):
        vtbuf[kvh * SWA_VT_ROWS:kvh * SWA_VT_ROWS + hd, :] = vt[kvh * hd:(kvh + 1) * hd, :].astype(BF16)
        vtbuf[kvh * SWA_VT_ROWS + hd:(kvh + 1) * SWA_VT_ROWS, :] = ones

    krow = lax.broadcasted_iota(jnp.int32, (2 * win, cols), 0)
    qpos = lax.broadcasted_iota(jnp.int32, (2 * win, cols), 1) % win + win
    rel = qpos - krow
    band = (rel >= 0) & (rel < win)
    first_head = lax.broadcasted_iota(jnp.int32, (1, cols), 1) < win
    kmin = jnp.where(i == 0, win, 0)

    scores = {}
    for j in range(SWA_BLOCKS):
        for kvh in range(SWA_KV_HEADS):
            q2 = jnp.concatenate(
                [q_ref[j * win:(j + 1) * win, (kvh * SWA_GROUP + g) * hd:(kvh * SWA_GROUP + g + 1) * hd]
                 for g in range(SWA_GROUP)], axis=0)
            k2 = kbuf[j * win:(j + 2) * win, kvh * hd:(kvh + 1) * hd]
            scores[j, kvh] = _dot_nt(k2, q2) * to_log2
    for j in range(SWA_BLOCKS):
        valid = band & (krow >= kmin) if j == 0 else band
        for kvh in range(SWA_KV_HEADS):
            s = jnp.where(valid, scores[j, kvh], NEG_BIG)
            sink = jnp.where(first_head, sink_ref[l, kvh * SWA_GROUP] * LOG2E,
                             sink_ref[l, kvh * SWA_GROUP + 1] * LOG2E)
            m = jnp.maximum(jnp.max(s, axis=0, keepdims=True), sink)
            p = jnp.exp2(s - m).astype(BF16)
            acc = _dot(vtbuf[kvh * SWA_VT_ROWS:(kvh + 1) * SWA_VT_ROWS, j * win:(j + 2) * win], p)
            denom = acc[hd:hd + 1, :] + jnp.exp2(sink - m)
            o = acc[0:hd, :] * (1.0 / denom)
            for g in range(SWA_GROUP):
                h = kvh * SWA_GROUP + g
                ytbuf[h * hd:(h + 1) * hd, j * win:(j + 1) * win] = o[:, g * win:(g + 1) * win]
    o_ref[...] = (ytbuf[...].T * _silu(dz_ref[...])).astype(o_ref.dtype)


def _swa_branch(l, swa_in, dz, sinks, batch, seq):
    bs = SEQ_BLOCK
    w = BRANCH_W
    nsb = seq // bs
    wins = seq // WINDOW
    prev_spec = pl.BlockSpec((WINDOW, w), lambda b, i: (b * wins + jnp.maximum(i * SWA_BLOCKS - 1, 0), 1))
    return pl.pallas_call(
        functools.partial(_swa_kernel, l),
        grid=(batch, nsb),
        in_specs=[pl.BlockSpec(memory_space=pltpu.SMEM),
                  _cols(w, 0, bs, nsb), _cols(w, 1, bs, nsb), prev_spec, _cols(w, 0, bs, nsb)],
        out_specs=pl.BlockSpec((None, bs, w), lambda b, i: (b, i, 0)),
        out_shape=jax.ShapeDtypeStruct((batch, seq, w), BF16),
        scratch_shapes=[pltpu.VMEM((bs + WINDOW, SWA_KV_HEADS * SWA_HEAD_DIM), BF16),
                        pltpu.VMEM((SWA_KV_HEADS * SWA_VT_ROWS, bs + WINDOW), BF16), pltpu.VMEM((w, bs), F32)],
        compiler_params=_params("parallel", "arbitrary"),
        name="swa_branch",
    )(sinks, swa_in, swa_in, swa_in, dz)


MLA_QK = MLA_HEADS * MLA_SLOT


def _rms(x, g):
    ms = jnp.mean(x * x, axis=-1, keepdims=True)
    return x * lax.rsqrt(ms + RMS_EPS) * g


def _mla_prep_kernel(cq_ref, ckv_ref, krot_ref, cos_ref, sin_ref, qg_ref, kvg_ref, wuq_ref, wukv_ref,
                     qt_ref, k_ref, vt_ref):
    scale = (MLA_NOPE + MLA_ROPE) ** -0.5 * LOG2E
    cq = _rms(cq_ref[...], qg_ref[...]).astype(BF16)
    ckv = _rms(ckv_ref[:, 0:MLA_KV_RANK], kvg_ref[...]).astype(BF16)
    cos = cos_ref[...]
    sin = sin_ref[...]
    k_rope = ckv_ref[:, MLA_KV_RANK:MLA_KV_RANK + LANES] * cos + krot_ref[...] * sin
    kv = _dot(ckv, wukv_ref[...])
    for h in range(MLA_HEADS):
        lo = h * MLA_SLOT
        qm = _dot(cq, wuq_ref[:, lo:lo + MLA_SLOT])
        qr = _dot(cq, wuq_ref[:, MLA_QK + lo:MLA_QK + lo + MLA_SLOT])
        qt_ref[lo:lo + MLA_SLOT, :] = ((qm * cos + qr * sin) * scale).T.astype(BF16)
        k_ref[:, lo:lo + MLA_SLOT] = (kv[:, lo:lo + MLA_SLOT] + k_rope).astype(BF16)
    vt = kv[:, MLA_QK:].T.astype(BF16)
    ones = jnp.ones((MLA_VT_ROWS - MLA_V, vt.shape[1]), BF16)
    for h in range(MLA_HEADS):
        vt_ref[h * MLA_VT_ROWS:h * MLA_VT_ROWS + MLA_V, :] = vt[h * MLA_V:(h + 1) * MLA_V, :]
        vt_ref[h * MLA_VT_ROWS + MLA_V:(h + 1) * MLA_VT_ROWS, :] = ones


def _mla_prep(l, mla_in, cos_t, sin_t, q_norm_g, kv_norm_g, w_uq_ext, w_ukv_ext, seq):
    t = mla_in.shape[0]
    tm = ATTN_BLK
    nsb = seq // tm
    vw = MLA_HEADS * MLA_V
    vt_rows = MLA_HEADS * MLA_VT_ROWS
    return pl.pallas_call(
        _mla_prep_kernel,
        grid=(t // tm,),
        in_specs=[pl.BlockSpec((tm, MLA_Q_RANK), lambda i: (i, 0)),
                  pl.BlockSpec((tm, MLA_Q_RANK), lambda i: (i, 1)),
                  pl.BlockSpec((tm, LANES), lambda i: (i, 2 * MLA_Q_RANK // LANES)),
                  pl.BlockSpec((tm, LANES), lambda i: (i % nsb, 0)),
                  pl.BlockSpec((tm, LANES), lambda i: (i % nsb, 0)),
                  _resident(l, (1, MLA_Q_RANK)), _resident(l, (1, MLA_KV_RANK)),
                  _resident(l, (MLA_Q_RANK, 2 * MLA_QK)), _resident(l, (MLA_KV_RANK, MLA_QK + vw))],
        out_specs=[pl.BlockSpec((None, MLA_QK, tm), lambda i: (i, 0, 0)),
                   pl.BlockSpec((tm, MLA_QK), lambda i: (i, 0)),
                   pl.BlockSpec((None, vt_rows, tm), lambda i: (i, 0, 0))],
        out_shape=[jax.ShapeDtypeStruct((t // tm, MLA_QK, tm), BF16), jax.ShapeDtypeStruct((t, MLA_QK), BF16),
                   jax.ShapeDtypeStruct((t // tm, vt_rows, tm), BF16)],
        compiler_params=_params("parallel"),
        name="mla_prep",
    )(mla_in, mla_in, mla_in, cos_t, sin_t, q_norm_g, kv_norm_g, w_uq_ext, w_ukv_ext)


def _attn_kernel(qt_ref, k_ref, vt_ref, bz_ref, o_ref, m_sc, acc_sc, s_sc):
    i = pl.program_id(1)
    blk = ATTN_BLK
    krow = lax.broadcasted_iota(jnp.int32, (blk, blk), 0)
    qcol = lax.broadcasted_iota(jnp.int32, (blk, blk), 1)
    causal = krow <= qcol
    m_sc[...] = jnp.full(m_sc.shape, NEG_BIG, F32)
    acc_sc[...] = jnp.zeros(acc_sc.shape, F32)

    def score(j, slot):
        start = j * blk if isinstance(j, int) else pl.multiple_of(j * blk, blk)
        for h in range(MLA_HEADS):
            kb = k_ref[pl.ds(start, blk), h * MLA_SLOT:(h + 1) * MLA_SLOT]
            s_sc[slot, h] = _dot(kb, qt_ref[h * MLA_SLOT:(h + 1) * MLA_SLOT, :])

    def absorb(j, slot, masked):
        for h in range(MLA_HEADS):
            s = s_sc[slot, h]
            if masked:
                s = jnp.where(causal, s, NEG_BIG)
            m_prev = m_sc[h]
            m_new = jnp.maximum(m_prev, jnp.max(s, axis=0, keepdims=True))
            alpha = jnp.exp2(m_prev - m_new)
            p = jnp.exp2(s - m_new).astype(BF16)
            vb = vt_ref[j, h * MLA_VT_ROWS:(h + 1) * MLA_VT_ROWS, :]
            acc_sc[h] = alpha * acc_sc[h] + _dot(vb, p)
            m_sc[h] = m_new

    score(0, 0)

    def body(t, c):
        j = 2 * t
        score(j + 1, 1)
        absorb(j, 0, False)
        score(j + 2, 0)
        absorb(j + 1, 1, False)
        return c

    lax.fori_loop(0, i // 2, body, 0)

    @pl.when(i % 2 == 0)
    def _():
        absorb(i, 0, True)

    @pl.when(i % 2 == 1)
    def _():
        score(i, 1)
        absorb(i - 1, 0, False)
        absorb(i, 1, True)
    outs = [acc_sc[h, 0:MLA_V, :] * (1.0 / acc_sc[h, MLA_V:MLA_V + 1, :]) for h in range(MLA_HEADS)]
    y = jnp.concatenate(outs, axis=0).T
    o_ref[...] = (y * _silu(bz_ref[...])).astype(o_ref.dtype)


def _mla_attention(qt, k, vt, bz, batch, seq):
    blk = ATTN_BLK
    nq = seq // blk
    vt_rows = MLA_HEADS * MLA_VT_ROWS
    return pl.pallas_call(
        _attn_kernel,
        grid=(batch, nq),
        in_specs=[pl.BlockSpec((None, MLA_QK, blk), lambda b, i: (b * nq + i, 0, 0)),
                  pl.BlockSpec((None, seq, MLA_QK), lambda b, i: (b, 0, 0)),
                  pl.BlockSpec((None, nq, vt_rows, blk), lambda b, i: (b, 0, 0, 0)),
                  _cols(BRANCH_W, 0, blk, nq)],
        out_specs=pl.BlockSpec((None, blk, BRANCH_W), lambda b, i: (b, i, 0)),
        out_shape=jax.ShapeDtypeStruct((batch, seq, BRANCH_W), BF16),
        scratch_shapes=[pltpu.VMEM((MLA_HEADS, 1, blk), F32), pltpu.VMEM((MLA_HEADS, MLA_VT_ROWS, blk), F32),
                        pltpu.VMEM((2, MLA_HEADS, blk, blk), F32)],
        compiler_params=_params("parallel", "arbitrary"),
        name="mla_attention",
    )(qt, k.reshape(batch, seq, MLA_QK), vt.reshape(batch, nq, vt_rows, blk), bz)


def _merge_kernel(x_ref, ya_ref, yb_ref, yc_ref, yd_ref, p_ref, wm_ref, bm_ref, wb_ref, wo_ref,
                  lng_ref, lnb_ref, wp_ref, wpg_ref, pg_ref, o_ref):
    d = D_MODEL
    x = x_ref[...]
    xb = x.astype(BF16)
    merged = None
    for n, y_ref in enumerate((ya_ref, yb_ref, yc_ref, yd_ref)):
        gate = _sigmoid(_dot(xb, wm_ref[:, n * d:(n + 1) * d]) + bm_ref[:, n * d:(n + 1) * d])
        term = gate * _dot(y_ref[...], wb_ref[n])
        merged = term if merged is None else merged + term
    z = DEEPNORM_ALPHA * x + _dot(merged.astype(BF16), wo_ref[...])
    mu = jnp.mean(z, axis=-1, keepdims=True)
    zc = z - mu
    var = jnp.mean(zc * zc, axis=-1, keepdims=True)
    xn = zc * lax.rsqrt(var + LN_EPS) * lng_ref[...] + lnb_ref[...]
    e = _dot(p_ref[...].astype(BF16), wp_ref[...]) * _sigmoid(_dot(xn.astype(BF16), wpg_ref[...]))
    o_ref[...] = xn + _rms(e, pg_ref[...])


def _merge(l, x2d, ys, p3d, w_merge, b_merge, w_branch, w_out, ln_g, ln_b, w_ple, w_ple_gate, ple_norm_g):
    t = x2d.shape[0]
    tm = MERGE_TM
    d = D_MODEL
    row_spec = lambda width: pl.BlockSpec((tm, width), lambda i: (i, 0))
    return pl.pallas_call(
        _merge_kernel,
        grid=(t // tm,),
        in_specs=[row_spec(d)] + [row_spec(BRANCH_W)] * N_BRANCH + [
                  pl.BlockSpec((None, tm, PLE_DIM), lambda i: (l, i, 0)),
                  _resident(l, (d, N_BRANCH * d)), _resident(l, (1, N_BRANCH * d)),
                  _resident(l, (N_BRANCH, BRANCH_W, d)), _resident(l, (d, d)),
                  _resident(l, (1, d)), _resident(l, (1, d)), _resident(l, (PLE_DIM, d)), _resident(l, (d, d)),
                  _resident(l, (1, d))],
        out_specs=row_spec(d),
        out_shape=jax.ShapeDtypeStruct((t, d), F32),
        compiler_params=_params("parallel"),
        name="merge",
    )(x2d, *ys, p3d, w_merge, b_merge, w_branch, w_out, ln_g, ln_b, w_ple, w_ple_gate, ple_norm_g)


def _rot_cols(w):
    half = w.shape[-1] // 2
    return jnp.concatenate([-w[..., half:], w[..., :half]], axis=-1)


def _zero_cols(w, n):
    return jnp.zeros(w.shape[:-1] + (n,), w.dtype)


def _rope_slot(w):
    return jnp.concatenate([_zero_cols(w, MLA_NOPE), w, _zero_cols(w, MLA_SLOT - MLA_NOPE - MLA_ROPE)], axis=-1)


def _in_weights(w_in):
    o = [0]
    for s in (256, 256, 256, 256, 128, 32, 256, 256, 256, 256, 128, 128, 256):
        o.append(o[-1] + s)
    col = lambda k: w_in[..., o[k]:o[k + 1]]
    a_val, a_gate, a_z, c_q, c_kv, k_r, b_z, u, c_z, q, k, v, d_z = (col(k) for k in range(13))
    groups = [a_val, a_gate, a_z, c_q, c_kv, _rope_slot(k_r), _rope_slot(_rot_cols(k_r)), b_z, u, c_z, q, k, v, d_z]
    return jnp.concatenate(groups, axis=-1).astype(BF16)


def _uq_weights(w_uq):
    hd = MLA_NOPE + MLA_ROPE
    main, rot = [], []
    for h in range(MLA_HEADS):
        head = w_uq[..., h * hd:(h + 1) * hd]
        main.append(jnp.concatenate([head, _zero_cols(w_uq, MLA_SLOT - hd)], axis=-1))
        rot.append(_rope_slot(_rot_cols(head[..., MLA_NOPE:])))
    return jnp.concatenate(main + rot, axis=-1).astype(BF16)


def _ukv_weights(w_ukv):
    hd = MLA_NOPE + MLA_V
    ks, vs = [], []
    for h in range(MLA_HEADS):
        ks.append(jnp.concatenate([w_ukv[..., h * hd:h * hd + MLA_NOPE], _zero_cols(w_ukv, MLA_SLOT - MLA_NOPE)],
                                  axis=-1))
        vs.append(w_ukv[..., h * hd + MLA_NOPE:(h + 1) * hd])
    return jnp.concatenate(ks + vs, axis=-1).astype(BF16)


def _rope_tables(seq):
    pos = jnp.arange(seq, dtype=F32)
    inv_freq = ROPE_THETA ** (-jnp.arange(0, MLA_ROPE, 2, dtype=F32) / MLA_ROPE)
    ang = pos[:, None] * inv_freq[None, :]
    cos, sin = jnp.cos(ang), jnp.sin(ang)
    ones = jnp.ones((seq, MLA_NOPE), F32)
    zeros = jnp.zeros((seq, MLA_NOPE), F32)
    pad = jnp.zeros((seq, MLA_SLOT - MLA_NOPE - MLA_ROPE), F32)
    cos_t = jnp.concatenate([ones, cos, cos, pad], axis=1)
    sin_t = jnp.concatenate([zeros, sin, sin, pad], axis=1)
    return cos_t, sin_t


def _ssm_weights(a_re, a_im, log_dt, b_re, b_im, c_re, c_im):
    g, p, h = SSM_GROUPS, SSM_STATE, SSM_GROUP
    nl = a_re.shape[0]
    dt = jnp.exp(log_dt.astype(F32))[..., None]
    lr, li = a_re.astype(F32), a_im.astype(F32)
    mag = jnp.exp(lr * dt)
    lb_re, lb_im = mag * jnp.cos(li * dt), mag * jnp.sin(li * dt)
    den = lr * lr + li * li
    nr, ni = lb_re - 1.0, lb_im
    f_re = ((nr * lr + ni * li) / den)[..., None]
    f_im = ((ni * lr - nr * li) / den)[..., None]
    bb_re = (f_re * b_re - f_im * b_im)[:, None]
    bb_im = (f_re * b_im + f_im * b_re)[:, None]
    c_re, c_im = c_re.astype(F32)[:, None], c_im.astype(F32)[:, None]
    prs, pis = [jnp.ones_like(lb_re)], [jnp.zeros_like(lb_im)]
    for _ in range(SSM_SUB):
        qr, qi = prs[-1], pis[-1]
        prs.append(qr * lb_re - qi * lb_im)
        pis.append(qr * lb_im + qi * lb_re)
    pw_r, pw_i = jnp.stack(prs, axis=1), jnp.stack(pis, axis=1)
    rows_gh = lambda m: m.reshape(nl, SSM_SUB, g * h, m.shape[-1])
    lanes = lambda m: jnp.tile(m, LANES // m.shape[-1])
    qr, qi = pw_r[:, :SSM_SUB, :, :, None], pw_i[:, :SSM_SUB, :, :, None]
    vr = jnp.swapaxes(qr * bb_re - qi * bb_im, -1, -2)
    vi = jnp.swapaxes(qr * bb_im + qi * bb_re, -1, -2)
    bc = lanes(jnp.stack([rows_gh(vr), rows_gh(vi)], axis=2))[:, ::-1]
    qr, qi = pw_r[:, 1:, :, None, :], pw_i[:, 1:, :, None, :]
    cc = lanes(jnp.stack([rows_gh(c_re * qr - c_im * qi), -rows_gh(c_re * qi + c_im * qr)], axis=2))
    lag = jnp.sum(vr[..., None, :] * c_re[:, :, :, None] - vi[..., None, :] * c_im[:, :, :, None], axis=-1)
    gc = lanes(rows_gh(lag))
    mr, mi = pw_r[:, SSM_SUB].reshape(nl, 1, g * p), pw_i[:, SSM_SUB].reshape(nl, 1, g * p)
    mu = [(mr, mi)]
    for _ in range(7):
        qr, qi = mu[-1]
        mu.append((qr * mr - qi * mi, qr * mi + qi * mr))
    cat = lambda q: jnp.concatenate(q, axis=-1)
    rowid = jnp.arange(8)[:, None]
    tables = [jnp.where(rowid >= (1 << lvl), cat(mu[(1 << lvl) - 1]), 0.0) for lvl in range(3)]
    tables.append(jnp.concatenate([cat(q) for q in mu], axis=1))
    return bc, cc, gc, jnp.stack(tables, axis=1)


def kernel(x, p, w_in, w_merge, b_merge, conv_w, conv_b, conv_norm_g, conv_norm_b, w_pw2, mla_q_norm_g,
           mla_kv_norm_g, w_uq, w_ukv, ssm_a_re, ssm_a_im, ssm_log_dt, ssm_b_re, ssm_b_im, ssm_c_re,
           ssm_c_im, ssm_d, w_glu, attn_sinks, w_branch, w_out, ln_g, ln_b, w_ple, w_ple_gate, ple_norm_g):
    batch, seq, d = x.shape
    t = batch * seq
    rows = lambda a: a.reshape(DEPTH, 1, -1).astype(F32)
    bf = lambda a: a.astype(BF16)
    cos_t, sin_t = _rope_tables(seq)
    w_in_ext, w_uq_ext, w_ukv_ext = _in_weights(w_in), _uq_weights(w_uq), _ukv_weights(w_ukv)
    bp, cp, gd, sc = _ssm_weights(ssm_a_re, ssm_a_im, ssm_log_dt, ssm_b_re, ssm_b_im, ssm_c_re, ssm_c_im)
    conv_p = (conv_w.astype(F32), rows(conv_b), rows(conv_norm_g), rows(conv_norm_b), bf(w_pw2))
    mla_p = (rows(mla_q_norm_g), rows(mla_kv_norm_g), w_uq_ext, w_ukv_ext)
    ssm_p = (bp, cp, gd, sc, rows(ssm_d), bf(w_glu))
    merge_p = (bf(w_merge), rows(b_merge), bf(w_branch), bf(w_out), rows(ln_g), rows(ln_b), bf(w_ple),
               bf(w_ple_gate), rows(ple_norm_g))
    sinks = attn_sinks.astype(F32)
    p3d = p.reshape(DEPTH, t, PLE_DIM)
    x2d = x.reshape(t, d)
    for l in range(DEPTH):
        conv_in, mla_in, bz, ssm_in, swa_in, dz = _in_proj(l, x2d, w_in_ext)
        y_a = _conv_branch(l, conv_in, *conv_p, batch, seq)
        qt, k, vt = _mla_prep(l, mla_in, cos_t, sin_t, *mla_p, seq)
        y_b = _mla_attention(qt, k, vt, bz, batch, seq)
        y_c = _ssm_branch(l, ssm_in, *ssm_p, batch, seq)
        y_d = _swa_branch(l, swa_in, dz, sinks, batch, seq)
        ys = [y.reshape(t, BRANCH_W) for y in (y_a, y_b, y_c, y_d)]
        x2d = _merge(l, x2d, ys, p3d, *merge_p)
    return x2d.reshape(batch, seq, d)
```

```python
import functools
import math

import jax
import jax.numpy as jnp
from jax import lax
from jax.experimental import pallas as pl
from jax.experimental.pallas import tpu as pltpu

D_MODEL = 1024
DEPTH = 4
PLE_DIM = 256
N_BRANCH = 4
BRANCH_W = 256
CONV_W = 31
MLA_HEADS = 4
MLA_NOPE = 64
MLA_ROPE = 32
MLA_V = 64
MLA_Q_RANK = 256
MLA_KV_RANK = 128
ROPE_THETA = 10000.0
SSM_GROUP = 16
SSM_GROUPS = 16
SSM_STATE = 64
SWA_HEADS = 4
SWA_KV_HEADS = 2
SWA_HEAD_DIM = 64
WINDOW = 128
DEEPNORM_ALPHA = (2.0 * DEPTH) ** 0.25
LN_EPS = 1e-5
RMS_EPS = 1e-6

LANES = 128
VMEM_LIMIT_BYTES = 56 * 1024 * 1024

IN_TM = 512
SEQ_BLOCK = 512
SSM_CHUNK = 2048
SSM_SUB = 8
SSM_ROWS = 1024
ATTN_BLK = 512
MERGE_TM = 512
CONV_HALO = 32
CONV_ROWS = 64

MLA_SLOT = LANES
MLA_VT_ROWS = MLA_V + 16
LOG2E = math.log2(math.e)
NSTATE = SSM_GROUPS * SSM_STATE
NEG_BIG = -1e30

F32 = jnp.float32
BF16 = jnp.bfloat16


def _params(*sem):
    return pltpu.CompilerParams(dimension_semantics=sem, vmem_limit_bytes=VMEM_LIMIT_BYTES)


def _resident(l, shape):
    nd = len(shape)
    return pl.BlockSpec((None,) + tuple(shape), lambda *_: (l,) + (0,) * nd, pipeline_mode=pl.Buffered(1))


def _dot(a, b):
    return jnp.dot(a, b, preferred_element_type=F32)


def _dot_nt(a, b):
    return lax.dot_general(a, b, (((1,), (1,)), ((), ())), preferred_element_type=F32)


def _sigmoid(x):
    return 1.0 / (1.0 + jnp.exp(-x))


def _silu(x):
    return x * _sigmoid(x)


IN_GROUPS = (
    ("conv", 3 * BRANCH_W, F32),
    ("mla", MLA_Q_RANK + MLA_KV_RANK + 2 * LANES, F32),
    ("bz", BRANCH_W, F32),
    ("ssm", 2 * BRANCH_W, F32),
    ("swa", 2 * BRANCH_W, BF16),
    ("dz", BRANCH_W, F32),
)
IN_EXT_WIDTH = sum(w for _, w, _ in IN_GROUPS)


def _in_proj_kernel(x_ref, w_ref, *out_refs):
    xb = x_ref[...].astype(BF16)
    start = 0
    for (_, width, dtype), o_ref in zip(IN_GROUPS, out_refs):
        o_ref[...] = _dot(xb, w_ref[:, start:start + width]).astype(dtype)
        start += width


def _in_proj(l, x2d, w_in_ext):
    t = x2d.shape[0]
    return pl.pallas_call(
        _in_proj_kernel,
        grid=(t // IN_TM,),
        in_specs=[pl.BlockSpec((IN_TM, D_MODEL), lambda i: (i, 0)),
                  _resident(l, (D_MODEL, IN_EXT_WIDTH))],
        out_specs=[pl.BlockSpec((IN_TM, w), lambda i: (i, 0)) for _, w, _ in IN_GROUPS],
        out_shape=[jax.ShapeDtypeStruct((t, w), d) for _, w, d in IN_GROUPS],
        compiler_params=_params("parallel"),
        name="in_proj",
    )(x2d, w_in_ext)


def _conv_kernel(h_ref, cw_ref, cb_ref, ng_ref, nb_ref, pw2_ref, o_ref, hbuf, ybuf):
    i = pl.program_id(1)
    bs = SEQ_BLOCK
    w = BRANCH_W

    @pl.when(i == 0)
    def _():
        hbuf[0:CONV_HALO, :] = jnp.zeros((CONV_HALO, w), F32)

    @pl.when(i > 0)
    def _():
        hbuf[0:CONV_HALO, :] = hbuf[bs:bs + CONV_HALO, :]

    hbuf[CONV_HALO:, :] = h_ref[:, 0:w] * _sigmoid(h_ref[:, w:2 * w])

    base = CONV_HALO - (CONV_W - 1)
    tile = 8

    for c in range(bs // CONV_ROWS):
        r0 = c * CONV_ROWS
        acc = jnp.broadcast_to(cb_ref[...], (CONV_ROWS, w))
        for r in range(tile):
            span = CONV_ROWS + (tile if r else 0)
            part = None
            for j in range((r - base) % tile, CONV_W, tile):
                lo = r0 + base + j - r
                term = cw_ref[j:j + 1, :] * hbuf[lo:lo + span, :]
                part = term if part is None else part + term
            acc = acc + part[r:r + CONV_ROWS, :]
        mu = jnp.mean(acc, axis=-1, keepdims=True)
        d = acc - mu
        var = jnp.mean(d * d, axis=-1, keepdims=True)
        y = d * lax.rsqrt(var + LN_EPS) * ng_ref[...] + nb_ref[...]
        ybuf[r0:r0 + CONV_ROWS, :] = _silu(y).astype(BF16)
    o_ref[...] = (_dot(ybuf[...], pw2_ref[...]) * _silu(h_ref[:, 2 * w:3 * w])).astype(o_ref.dtype)


def _conv_branch(l, conv_in, conv_w, conv_b, norm_g, norm_b, w_pw2, batch, seq):
    bs = SEQ_BLOCK
    w = BRANCH_W
    return pl.pallas_call(
        _conv_kernel,
        grid=(batch, seq // bs),
        in_specs=[pl.BlockSpec((None, bs, 3 * w), lambda b, i: (b, i, 0)),
                  _resident(l, (CONV_W, w)), _resident(l, (1, w)), _resident(l, (1, w)), _resident(l, (1, w)),
                  _resident(l, (w, w))],
        out_specs=pl.BlockSpec((None, bs, w), lambda b, i: (b, i, 0)),
        out_shape=jax.ShapeDtypeStruct((batch, seq, w), BF16),
        scratch_shapes=[pltpu.VMEM((bs + CONV_HALO, w), F32), pltpu.VMEM((bs, w), BF16)],
        compiler_params=_params("parallel", "arbitrary"),
        name="conv_branch",
    )(conv_in.reshape(batch, seq, 3 * w), conv_w, conv_b, norm_g, norm_b, w_pw2)


def _cols(width, col, rows, per_batch):
    return pl.BlockSpec((rows, width), lambda b, i: (b * per_batch + i, col))


SSM_BOUND = SSM_CHUNK // SSM_SUB
assert SSM_BOUND % 8 == 0


def _ssm_kernel(u_ref, cz_ref, bc_ref, cc_ref, gc_ref, sc_ref, d_ref, wglu_ref, o_ref,
                carry, ubuf, zbuf, obuf, hbuf, bp_ref, cpt_ref, g_ref):
    i = pl.program_id(1)
    n = NSTATE
    w = BRANCH_W
    m = SSM_BOUND

    @pl.when((pl.program_id(0) == 0) & (i == 0))
    def _():
        shift = lambda a, k: lax.shift_right_logical(a, jnp.full(a.shape, k, jnp.int32))
        row_g = shift(lax.broadcasted_iota(jnp.int32, (w, n), 0), int(math.log2(SSM_GROUP)))
        col_g = shift(lax.broadcasted_iota(jnp.int32, (w, n), 1), int(math.log2(SSM_STATE)))
        same_group = row_g == col_g
        in_g = shift(lax.broadcasted_iota(jnp.int32, (w, w), 0), int(math.log2(SSM_GROUP)))
        out_g = shift(lax.broadcasted_iota(jnp.int32, (w, w), 1), int(math.log2(SSM_GROUP)))
        same_group_out = in_g == out_g
        for k in range(SSM_SUB):
            for c in range(2):
                for src, dst in ((bc_ref, bp_ref), (cc_ref, cpt_ref)):
                    wide = jnp.concatenate([src[k, c]] * (n // LANES), axis=1)
                    dst[k, :, c * n:(c + 1) * n] = jnp.where(same_group, wide, 0.0).astype(BF16)
            wide = jnp.concatenate([gc_ref[k]] * (w // LANES), axis=1)
            g_ref[k] = jnp.where(same_group_out, wide, 0.0).astype(BF16)

    @pl.when(i == 0)
    def _():
        carry[...] = jnp.zeros_like(carry)

    nt = w // LANES
    for k in range(nt):
        ubuf[k] = u_ref[:, k * LANES:(k + 1) * LANES]
        zbuf[k] = cz_ref[:, k * LANES:(k + 1) * LANES]

    def rows_of(buf, p):
        return jnp.concatenate([buf[k, pl.ds(p, m, stride=SSM_SUB), :] for k in range(nt)], axis=1)

    us = [rows_of(ubuf, p) for p in range(SSM_SUB)]
    ub = [u.astype(BF16) for u in us]

    x = None
    for p in range(SSM_SUB):
        t = _dot(ub[p], bp_ref[p])
        x = t if x is None else x + t

    def lag(p):
        y = _dot(ub[p], g_ref[0])
        for d in range(1, p + 1):
            y = y + _dot(ub[p - d], g_ref[d])
        return y

    tile = 8
    first = lax.broadcasted_iota(jnp.int32, (tile, n), 0) == 0
    cr = carry[0:1, :]
    ci = carry[1:2, :]
    lags = []
    tiles_per_lag = (m // tile) // SSM_SUB
    for k in range(m // tile):
        if k % tiles_per_lag == 0:
            lags.append(lag(k // tiles_per_lag))
        tr = x[k * tile:(k + 1) * tile, 0:n]
        ti = x[k * tile:(k + 1) * tile, n:2 * n]
        for lvl in range(3):
            s = 1 << lvl
            ar = sc_ref[lvl, :, 0:n]
            ai = sc_ref[lvl, :, n:2 * n]
            sr = pltpu.roll(tr, s, 0)
            si = pltpu.roll(ti, s, 0)
            tr, ti = tr + (ar * sr - ai * si), ti + (ar * si + ai * sr)
        pr = sc_ref[3, :, 0:n]
        pi = sc_ref[3, :, n:2 * n]
        tr, ti = tr + (pr * cr - pi * ci), ti + (pr * ci + pi * cr)
        hbuf[k * tile:(k + 1) * tile, 0:n] = jnp.where(first, cr, pltpu.roll(tr, 1, 0))
        hbuf[k * tile:(k + 1) * tile, n:2 * n] = jnp.where(first, ci, pltpu.roll(ti, 1, 0))
        cr = tr[tile - 1:tile, :]
        ci = ti[tile - 1:tile, :]
    carry[0:1, :] = cr
    carry[1:2, :] = ci
    hcat = hbuf[...].astype(BF16)

    ys, gs = {}, {}
    for t in range(SSM_SUB + 2):
        if t < SSM_SUB:
            ys[t] = _dot_nt(hcat, cpt_ref[t]) + lags[t] + d_ref[...] * us[t]
        if 0 <= t - 1 < SSM_SUB:
            gs[t - 1] = _dot(jax.nn.gelu(ys.pop(t - 1)).astype(BF16), wglu_ref[...])
        if 0 <= t - 2 < SSM_SUB:
            p = t - 2
            g = gs.pop(p)
            out = g[:, 0:w] * _sigmoid(g[:, w:2 * w]) * _silu(rows_of(zbuf, p))
            for k in range(nt):
                obuf[k, pl.ds(p, m, stride=SSM_SUB), :] = out[:, k * LANES:(k + 1) * LANES]
    o_ref[...] = jnp.concatenate([obuf[k] for k in range(nt)], axis=1).astype(o_ref.dtype)


def _ssm_branch(l, ssm_in, bc, cc, gc, powers, d_skip, w_glu, batch, seq):
    rows = SSM_CHUNK
    w = BRANCH_W
    return pl.pallas_call(
        _ssm_kernel,
        grid=(batch, seq // rows),
        in_specs=[_cols(w, 0, rows, seq // rows), _cols(w, 1, rows, seq // rows),
                  _resident(l, (SSM_SUB, 2, w, LANES)), _resident(l, (SSM_SUB, 2, w, LANES)),
                  _resident(l, (SSM_SUB, w, LANES)),
                  _resident(l, (4, 8, 2 * NSTATE)), _resident(l, (1, w)), _resident(l, (w, 2 * w))],
        out_specs=pl.BlockSpec((None, rows, w), lambda b, i: (b, i, 0)),
        out_shape=jax.ShapeDtypeStruct((batch, seq, w), BF16),
        scratch_shapes=[pltpu.VMEM((8, NSTATE), F32)] + [pltpu.VMEM((w // LANES, rows, LANES), F32)] * 3 + [
                        pltpu.VMEM((SSM_BOUND, 2 * NSTATE), F32),
                        pltpu.VMEM((SSM_SUB, w, 2 * NSTATE), BF16), pltpu.VMEM((SSM_SUB, w, 2 * NSTATE), BF16),
                        pltpu.VMEM((SSM_SUB, w, w), BF16)],
        compiler_params=_params("arbitrary", "arbitrary"),
        name="ssm_branch",
    )(ssm_in, ssm_in, bc, cc, gc, powers, d_skip, w_glu)


SWA_GROUP = SWA_HEADS // SWA_KV_HEADS
SWA_BLOCKS = SEQ_BLOCK // WINDOW


SWA_VT_ROWS = SWA_HEAD_DIM + 16


def _swa_kernel(l, sink_ref, q_ref, kv_ref, prev_ref, dz_ref, o_ref, kbuf, vtbuf, ytbuf):
    i = pl.program_id(1)
    hd = SWA_HEAD_DIM
    kv_w = SWA_KV_HEADS * hd
    win = WINDOW
    cols = SWA_GROUP * win

    to_log2 = hd ** -0.5 * LOG2E
    kbuf[0:win, :] = prev_ref[:, 0:kv_w]
    kbuf[win:, :] = kv_ref[:, 0:kv_w]
    v_all = jnp.concatenate([prev_ref[:, kv_w:2 * kv_w], kv_ref[:, kv_w:2 * kv_w]], axis=0)
    vt = v_all.astype(F32).T
    ones = jnp.ones((SWA_VT_ROWS - hd, vt.shape[1]), BF16)
    for kvh in range(SWA_KV_HEADS):
        vtbuf[kvh * SWA_VT_ROWS:kvh * SWA_VT_ROWS + hd, :] = vt[kvh * hd:(kvh + 1) * hd, :].astype(BF16)
        vtbuf[kvh * SWA_VT_ROWS + hd:(kvh + 1) * SWA_VT_ROWS, :] = ones

    krow = lax.broadcasted_iota(jnp.int32, (2 * win, cols), 0)
    qpos = lax.broadcasted_iota(jnp.int32, (2 * win, cols), 1) % win + win
    rel = qpos - krow
    band = (rel >= 0) & (rel < win)
    first_head = lax.broadcasted_iota(jnp.int32, (1, cols), 1) < win
    kmin = jnp.where(i == 0, win, 0)

    scores = {}
    for j in range(SWA_BLOCKS):
        for kvh in range(SWA_KV_HEADS):
            q2 = jnp.concatenate(
                [q_ref[j * win:(j + 1) * win, (kvh * SWA_GROUP + g) * hd:(kvh * SWA_GROUP + g + 1) * hd]
                 for g in range(SWA_GROUP)], axis=0)
            k2 = kbuf[j * win:(j + 2) * win, kvh * hd:(kvh + 1) * hd]
            scores[j, kvh] = _dot_nt(k2, q2) * to_log2
    for j in range(SWA_BLOCKS):
        valid = band & (krow >= kmin) if j == 0 else band
        for kvh in range(SWA_KV_HEADS):
            s = jnp.where(valid, scores[j, kvh], NEG_BIG)
            sink = jnp.where(first_head, sink_ref[l, kvh * SWA_GROUP] * LOG2E,
                             sink_ref[l, kvh * SWA_GROUP + 1] * LOG2E)
            m = jnp.maximum(jnp.max(s, axis=0, keepdims=True), sink)
            p = jnp.exp2(s - m).astype(BF16)
            acc = _dot(vtbuf[kvh * SWA_VT_ROWS:(kvh + 1) * SWA_VT_ROWS, j * win:(j + 2) * win], p)
            denom = acc[hd:hd + 1, :] + jnp.exp2(sink - m)
            o = acc[0:hd, :] * (1.0 / denom)
            for g in range(SWA_GROUP):
                h = kvh * SWA_GROUP + g
                ytbuf[h * hd:(h + 1) * hd, j * win:(j + 1) * win] = o[:, g * win:(g + 1) * win]
    o_ref[...] = (ytbuf[...].T * _silu(dz_ref[...])).astype(o_ref.dtype)


def _swa_branch(l, swa_in, dz, sinks, batch, seq):
    bs = SEQ_BLOCK
    w = BRANCH_W
    nsb = seq // bs
    wins = seq // WINDOW
    prev_spec = pl.BlockSpec((WINDOW, w), lambda b, i: (b * wins + jnp.maximum(i * SWA_BLOCKS - 1, 0), 1))
    return pl.pallas_call(
        functools.partial(_swa_kernel, l),
        grid=(batch, nsb),
        in_specs=[pl.BlockSpec(memory_space=pltpu.SMEM),
                  _cols(w, 0, bs, nsb), _cols(w, 1, bs, nsb), prev_spec, _cols(w, 0, bs, nsb)],
        out_specs=pl.BlockSpec((None, bs, w), lambda b, i: (b, i, 0)),
        out_shape=jax.ShapeDtypeStruct((batch, seq, w), BF16),
        scratch_shapes=[pltpu.VMEM((bs + WINDOW, SWA_KV_HEADS * SWA_HEAD_DIM), BF16),
                        pltpu.VMEM((SWA_KV_HEADS * SWA_VT_ROWS, bs + WINDOW), BF16), pltpu.VMEM((w, bs), F32)],
        compiler_params=_params("parallel", "arbitrary"),
        name="swa_branch",
    )(sinks, swa_in, swa_in, swa_in, dz)


MLA_QK = MLA_HEADS * MLA_SLOT


def _rms(x, g):
    ms = jnp.mean(x * x, axis=-1, keepdims=True)
    return x * lax.rsqrt(ms + RMS_EPS) * g


def _mla_prep_kernel(cq_ref, ckv_ref, krot_ref, cos_ref, sin_ref, qg_ref, kvg_ref, wuq_ref, wukv_ref,
                     qt_ref, k_ref, vt_ref):
    scale = (MLA_NOPE + MLA_ROPE) ** -0.5 * LOG2E
    cq = _rms(cq_ref[...], qg_ref[...]).astype(BF16)
    ckv = _rms(ckv_ref[:, 0:MLA_KV_RANK], kvg_ref[...]).astype(BF16)
    cos = cos_ref[...]
    sin = sin_ref[...]
    k_rope = ckv_ref[:, MLA_KV_RANK:MLA_KV_RANK + LANES] * cos + krot_ref[...] * sin
    kv = _dot(ckv, wukv_ref[...])
    for h in range(MLA_HEADS):
        lo = h * MLA_SLOT
        qm = _dot(cq, wuq_ref[:, lo:lo + MLA_SLOT])
        qr = _dot(cq, wuq_ref[:, MLA_QK + lo:MLA_QK + lo + MLA_SLOT])
        qt_ref[lo:lo + MLA_SLOT, :] = ((qm * cos + qr * sin) * scale).T.astype(BF16)
        k_ref[:, lo:lo + MLA_SLOT] = (kv[:, lo:lo + MLA_SLOT] + k_rope).astype(BF16)
    vt = kv[:, MLA_QK:].T.astype(BF16)
    ones = jnp.ones((MLA_VT_ROWS - MLA_V, vt.shape[1]), BF16)
    for h in range(MLA_HEADS):
        vt_ref[h * MLA_VT_ROWS:h * MLA_VT_ROWS + MLA_V, :] = vt[h * MLA_V:(h + 1) * MLA_V, :]
        vt_ref[h * MLA_VT_ROWS + MLA_V:(h + 1) * MLA_VT_ROWS, :] = ones


def _mla_prep(l, mla_in, cos_t, sin_t, q_norm_g, kv_norm_g, w_uq_ext, w_ukv_ext, seq):
    t = mla_in.shape[0]
    tm = ATTN_BLK
    nsb = seq // tm
    vw = MLA_HEADS * MLA_V
    vt_rows = MLA_HEADS * MLA_VT_ROWS
    return pl.pallas_call(
        _mla_prep_kernel,
        grid=(t // tm,),
        in_specs=[pl.BlockSpec((tm, MLA_Q_RANK), lambda i: (i, 0)),
                  pl.BlockSpec((tm, MLA_Q_RANK), lambda i: (i, 1)),
                  pl.BlockSpec((tm, LANES), lambda i: (i, 2 * MLA_Q_RANK // LANES)),
                  pl.BlockSpec((tm, LANES), lambda i: (i % nsb, 0)),
                  pl.BlockSpec((tm, LANES), lambda i: (i % nsb, 0)),
                  _resident(l, (1, MLA_Q_RANK)), _resident(l, (1, MLA_KV_RANK)),
                  _resident(l, (MLA_Q_RANK, 2 * MLA_QK)), _resident(l, (MLA_KV_RANK, MLA_QK + vw))],
        out_specs=[pl.BlockSpec((None, MLA_QK, tm), lambda i: (i, 0, 0)),
                   pl.BlockSpec((tm, MLA_QK), lambda i: (i, 0)),
                   pl.BlockSpec((None, vt_rows, tm), lambda i: (i, 0, 0))],
        out_shape=[jax.ShapeDtypeStruct((t // tm, MLA_QK, tm), BF16), jax.ShapeDtypeStruct((t, MLA_QK), BF16),
                   jax.ShapeDtypeStruct((t // tm, vt_rows, tm), BF16)],
        compiler_params=_params("parallel"),
        name="mla_prep",
    )(mla_in, mla_in, mla_in, cos_t, sin_t, q_norm_g, kv_norm_g, w_uq_ext, w_ukv_ext)


def _attn_kernel(qt_ref, k_ref, vt_ref, bz_ref, o_ref, m_sc, acc_sc, s_sc):
    i = pl.program_id(1)
    blk = ATTN_BLK
    krow = lax.broadcasted_iota(jnp.int32, (blk, blk), 0)
    qcol = lax.broadcasted_iota(jnp.int32, (blk, blk), 1)
    causal = krow <= qcol
    m_sc[...] = jnp.full(m_sc.shape, NEG_BIG, F32)
    acc_sc[...] = jnp.zeros(acc_sc.shape, F32)

    def score(j, slot):
        start = j * blk if isinstance(j, int) else pl.multiple_of(j * blk, blk)
        for h in range(MLA_HEADS):
            kb = k_ref[pl.ds(start, blk), h * MLA_SLOT:(h + 1) * MLA_SLOT]
            s_sc[slot, h] = _dot(kb, qt_ref[h * MLA_SLOT:(h + 1) * MLA_SLOT, :])

    def absorb(j, slot, masked):
        for h in range(MLA_HEADS):
            s = s_sc[slot, h]
            if masked:
                s = jnp.where(causal, s, NEG_BIG)
            m_prev = m_sc[h]
            m_new = jnp.maximum(m_prev, jnp.max(s, axis=0, keepdims=True))
            alpha = jnp.exp2(m_prev - m_new)
            p = jnp.exp2(s - m_new).astype(BF16)
            vb = vt_ref[j, h * MLA_VT_ROWS:(h + 1) * MLA_VT_ROWS, :]
            acc_sc[h] = alpha * acc_sc[h] + _dot(vb, p)
            m_sc[h] = m_new

    score(0, 0)

    def body(t, c):
        j = 2 * t
        score(j + 1, 1)
        absorb(j, 0, False)
        score(j + 2, 0)
        absorb(j + 1, 1, False)
        return c

    lax.fori_loop(0, i // 2, body, 0)

    @pl.when(i % 2 == 0)
    def _():
        absorb(i, 0, True)

    @pl.when(i % 2 == 1)
    def _():
        score(i, 1)
        absorb(i - 1, 0, False)
        absorb(i, 1, True)
    outs = [acc_sc[h, 0:MLA_V, :] * (1.0 / acc_sc[h, MLA_V:MLA_V + 1, :]) for h in range(MLA_HEADS)]
    y = jnp.concatenate(outs, axis=0).T
    o_ref[...] = (y * _silu(bz_ref[...])).astype(o_ref.dtype)


def _mla_attention(qt, k, vt, bz, batch, seq):
    blk = ATTN_BLK
    nq = seq // blk
    vt_rows = MLA_HEADS * MLA_VT_ROWS
    return pl.pallas_call(
        _attn_kernel,
        grid=(batch, nq),
        in_specs=[pl.BlockSpec((None, MLA_QK, blk), lambda b, i: (b * nq + i, 0, 0)),
                  pl.BlockSpec((None, seq, MLA_QK), lambda b, i: (b, 0, 0)),
                  pl.BlockSpec((None, nq, vt_rows, blk), lambda b, i: (b, 0, 0, 0)),
                  _cols(BRANCH_W, 0, blk, nq)],
        out_specs=pl.BlockSpec((None, blk, BRANCH_W), lambda b, i: (b, i, 0)),
        out_shape=jax.ShapeDtypeStruct((batch, seq, BRANCH_W), BF16),
        scratch_shapes=[pltpu.VMEM((MLA_HEADS, 1, blk), F32), pltpu.VMEM((MLA_HEADS, MLA_VT_ROWS, blk), F32),
                        pltpu.VMEM((2, MLA_HEADS, blk, blk), F32)],
        compiler_params=_params("parallel", "arbitrary"),
        name="mla_attention",
    )(qt, k.reshape(batch, seq, MLA_QK), vt.reshape(batch, nq, vt_rows, blk), bz)


def _merge_kernel(x_ref, ya_ref, yb_ref, yc_ref, yd_ref, p_ref, wm_ref, bm_ref, wb_ref, wo_ref,
                  lng_ref, lnb_ref, wp_ref, wpg_ref, pg_ref, o_ref):
    d = D_MODEL
    x = x_ref[...]
    xb = x.astype(BF16)
    merged = None
    for n, y_ref in enumerate((ya_ref, yb_ref, yc_ref, yd_ref)):
        gate = _sigmoid(_dot(xb, wm_ref[:, n * d:(n + 1) * d]) + bm_ref[:, n * d:(n + 1) * d])
        term = gate * _dot(y_ref[...], wb_ref[n])
        merged = term if merged is None else merged + term
    z = DEEPNORM_ALPHA * x + _dot(merged.astype(BF16), wo_ref[...])
    mu = jnp.mean(z, axis=-1, keepdims=True)
    zc = z - mu
    var = jnp.mean(zc * zc, axis=-1, keepdims=True)
    xn = zc * lax.rsqrt(var + LN_EPS) * lng_ref[...] + lnb_ref[...]
    e = _dot(p_ref[...].astype(BF16), wp_ref[...]) * _sigmoid(_dot(xn.astype(BF16), wpg_ref[...]))
    o_ref[...] = xn + _rms(e, pg_ref[...])


def _merge(l, x2d, ys, p3d, w_merge, b_merge, w_branch, w_out, ln_g, ln_b, w_ple, w_ple_gate, ple_norm_g):
    t = x2d.shape[0]
    tm = MERGE_TM
    d = D_MODEL
    row_spec = lambda width: pl.BlockSpec((tm, width), lambda i: (i, 0))
    return pl.pallas_call(
        _merge_kernel,
        grid=(t // tm,),
        in_specs=[row_spec(d)] + [row_spec(BRANCH_W)] * N_BRANCH + [
                  pl.BlockSpec((None, tm, PLE_DIM), lambda i: (l, i, 0)),
                  _resident(l, (d, N_BRANCH * d)), _resident(l, (1, N_BRANCH * d)),
                  _resident(l, (N_BRANCH, BRANCH_W, d)), _resident(l, (d, d)),
                  _resident(l, (1, d)), _resident(l, (1, d)), _resident(l, (PLE_DIM, d)), _resident(l, (d, d)),
                  _resident(l, (1, d))],
        out_specs=row_spec(d),
        out_shape=jax.ShapeDtypeStruct((t, d), F32),
        compiler_params=_params("parallel"),
        name="merge",
    )(x2d, *ys, p3d, w_merge, b_merge, w_branch, w_out, ln_g, ln_b, w_ple, w_ple_gate, ple_norm_g)


def _rot_cols(w):
    half = w.shape[-1] // 2
    return jnp.concatenate([-w[..., half:], w[..., :half]], axis=-1)


def _zero_cols(w, n):
    return jnp.zeros(w.shape[:-1] + (n,), w.dtype)


def _rope_slot(w):
    return jnp.concatenate([_zero_cols(w, MLA_NOPE), w, _zero_cols(w, MLA_SLOT - MLA_NOPE - MLA_ROPE)], axis=-1)


def _in_weights(w_in):
    o = [0]
    for s in (256, 256, 256, 256, 128, 32, 256, 256, 256, 256, 128, 128, 256):
        o.append(o[-1] + s)
    col = lambda k: w_in[..., o[k]:o[k + 1]]
    a_val, a_gate, a_z, c_q, c_kv, k_r, b_z, u, c_z, q, k, v, d_z = (col(k) for k in range(13))
    groups = [a_val, a_gate, a_z, c_q, c_kv, _rope_slot(k_r), _rope_slot(_rot_cols(k_r)), b_z, u, c_z, q, k, v, d_z]
    return jnp.concatenate(groups, axis=-1).astype(BF16)


def _uq_weights(w_uq):
    hd = MLA_NOPE + MLA_ROPE
    main, rot = [], []
    for h in range(MLA_HEADS):
        head = w_uq[..., h * hd:(h + 1) * hd]
        main.append(jnp.concatenate([head, _zero_cols(w_uq, MLA_SLOT - hd)], axis=-1))
        rot.append(_rope_slot(_rot_cols(head[..., MLA_NOPE:])))
    return jnp.concatenate(main + rot, axis=-1).astype(BF16)


def _ukv_weights(w_ukv):
    hd = MLA_NOPE + MLA_V
    ks, vs = [], []
    for h in range(MLA_HEADS):
        ks.append(jnp.concatenate([w_ukv[..., h * hd:h * hd + MLA_NOPE], _zero_cols(w_ukv, MLA_SLOT - MLA_NOPE)],
                                  axis=-1))
        vs.append(w_ukv[..., h * hd + MLA_NOPE:(h + 1) * hd])
    return jnp.concatenate(ks + vs, axis=-1).astype(BF16)


def _rope_tables(seq):
    pos = jnp.arange(seq, dtype=F32)
    inv_freq = ROPE_THETA ** (-jnp.arange(0, MLA_ROPE, 2, dtype=F32) / MLA_ROPE)
    ang = pos[:, None] * inv_freq[None, :]
    cos, sin = jnp.cos(ang), jnp.sin(ang)
    ones = jnp.ones((seq, MLA_NOPE), F32)
    zeros = jnp.zeros((seq, MLA_NOPE), F32)
    pad = jnp.zeros((seq, MLA_SLOT - MLA_NOPE - MLA_ROPE), F32)
    cos_t = jnp.concatenate([ones, cos, cos, pad], axis=1)
    sin_t = jnp.concatenate([zeros, sin, sin, pad], axis=1)
    return cos_t, sin_t


def _ssm_weights(a_re, a_im, log_dt, b_re, b_im, c_re, c_im):
    g, p, h = SSM_GROUPS, SSM_STATE, SSM_GROUP
    nl = a_re.shape[0]
    dt = jnp.exp(log_dt.astype(F32))[..., None]
    lr, li = a_re.astype(F32), a_im.astype(F32)
    mag = jnp.exp(lr * dt)
    lb_re, lb_im = mag * jnp.cos(li * dt), mag * jnp.sin(li * dt)
    den = lr * lr + li * li
    nr, ni = lb_re - 1.0, lb_im
    f_re = ((nr * lr + ni * li) / den)[..., None]
    f_im = ((ni * lr - nr * li) / den)[..., None]
    bb_re = (f_re * b_re - f_im * b_im)[:, None]
    bb_im = (f_re * b_im + f_im * b_re)[:, None]
    c_re, c_im = c_re.astype(F32)[:, None], c_im.astype(F32)[:, None]
    prs, pis = [jnp.ones_like(lb_re)], [jnp.zeros_like(lb_im)]
    for _ in range(SSM_SUB):
        qr, qi = prs[-1], pis[-1]
        prs.append(qr * lb_re - qi * lb_im)
        pis.append(qr * lb_im + qi * lb_re)
    pw_r, pw_i = jnp.stack(prs, axis=1), jnp.stack(pis, axis=1)
    rows_gh = lambda m: m.reshape(nl, SSM_SUB, g * h, m.shape[-1])
    lanes = lambda m: jnp.tile(m, LANES // m.shape[-1])
    qr, qi = pw_r[:, :SSM_SUB, :, :, None], pw_i[:, :SSM_SUB, :, :, None]
    vr = jnp.swapaxes(qr * bb_re - qi * bb_im, -1, -2)
    vi = jnp.swapaxes(qr * bb_im + qi * bb_re, -1, -2)
    bc = lanes(jnp.stack([rows_gh(vr), rows_gh(vi)], axis=2))[:, ::-1]
    qr, qi = pw_r[:, 1:, :, None, :], pw_i[:, 1:, :, None, :]
    cc = lanes(jnp.stack([rows_gh(c_re * qr - c_im * qi), -rows_gh(c_re * qi + c_im * qr)], axis=2))
    lag = jnp.sum(vr[..., None, :] * c_re[:, :, :, None] - vi[..., None, :] * c_im[:, :, :, None], axis=-1)
    gc = lanes(rows_gh(lag))
    mr, mi = pw_r[:, SSM_SUB].reshape(nl, 1, g * p), pw_i[:, SSM_SUB].reshape(nl, 1, g * p)
    mu = [(mr, mi)]
    for _ in range(7):
        qr, qi = mu[-1]
        mu.append((qr * mr - qi * mi, qr * mi + qi * mr))
    cat = lambda q: jnp.concatenate(q, axis=-1)
    rowid = jnp.arange(8)[:, None]
    tables = [jnp.where(rowid >= (1 << lvl), cat(mu[(1 << lvl) - 1]), 0.0) for lvl in range(3)]
    tables.append(jnp.concatenate([cat(q) for q in mu], axis=1))
    return bc, cc, gc, jnp.stack(tables, axis=1)


def kernel(x, p, w_in, w_merge, b_merge, conv_w, conv_b, conv_norm_g, conv_norm_b, w_pw2, mla_q_norm_g,
           mla_kv_norm_g, w_uq, w_ukv, ssm_a_re, ssm_a_im, ssm_log_dt, ssm_b_re, ssm_b_im, ssm_c_re,
           ssm_c_im, ssm_d, w_glu, attn_sinks, w_branch, w_out, ln_g, ln_b, w_ple, w_ple_gate, ple_norm_g):
    batch, seq, d = x.shape
    t = batch * seq
    rows = lambda a: a.reshape(DEPTH, 1, -1).astype(F32)
    bf = lambda a: a.astype(BF16)
    cos_t, sin_t = _rope_tables(seq)
    w_in_ext, w_uq_ext, w_ukv_ext = _in_weights(w_in), _uq_weights(w_uq), _ukv_weights(w_ukv)
    bp, cp, gd, sc = _ssm_weights(ssm_a_re, ssm_a_im, ssm_log_dt, ssm_b_re, ssm_b_im, ssm_c_re, ssm_c_im)
    conv_p = (conv_w.astype(F32), rows(conv_b), rows(conv_norm_g), rows(conv_norm_b), bf(w_pw2))
    mla_p = (rows(mla_q_norm_g), rows(mla_kv_norm_g), w_uq_ext, w_ukv_ext)
    ssm_p = (bp, cp, gd, sc, rows(ssm_d), bf(w_glu))
    merge_p = (bf(w_merge), rows(b_merge), bf(w_branch), bf(w_out), rows(ln_g), rows(ln_b), bf(w_ple),
               bf(w_ple_gate), rows(ple_norm_g))
    sinks = attn_sinks.astype(F32)
    p3d = p.reshape(DEPTH, t, PLE_DIM)
    x2d = x.reshape(t, d)
    for l in range(DEPTH):
        conv_in, mla_in, bz, ssm_in, swa_in, dz = _in_proj(l, x2d, w_in_ext)
        y_a = _conv_branch(l, conv_in, *conv_p, batch, seq)
        qt, k, vt = _mla_prep(l, mla_in, cos_t, sin_t, *mla_p, seq)
        y_b = _mla_attention(qt, k, vt, bz, batch, seq)
        y_c = _ssm_branch(l, ssm_in, *ssm_p, batch, seq)
        y_d = _swa_branch(l, swa_in, dz, sinks, batch, seq)
        ys = [y.reshape(t, BRANCH_W) for y in (y_a, y_b, y_c, y_d)]
        x2d = _merge(l, x2d, ys, p3d, *merge_p)
    return x2d.reshape(batch, seq, d)
```

```python
import functools
import math

import jax
import jax.numpy as jnp
from jax import lax
from jax.experimental import pallas as pl
from jax.experimental.pallas import tpu as pltpu

D_MODEL = 1024
DEPTH = 4
PLE_DIM = 256
N_BRANCH = 4
BRANCH_W = 256
CONV_W = 31
MLA_HEADS = 4
MLA_NOPE = 64
MLA_ROPE = 32
MLA_V = 64
MLA_Q_RANK = 256
MLA_KV_RANK = 128
ROPE_THETA = 10000.0
SSM_GROUP = 16
SSM_GROUPS = 16
SSM_STATE = 64
SWA_HEADS = 4
SWA_KV_HEADS = 2
SWA_HEAD_DIM = 64
WINDOW = 128
DEEPNORM_ALPHA = (2.0 * DEPTH) ** 0.25
LN_EPS = 1e-5
RMS_EPS = 1e-6

LANES = 128
VMEM_LIMIT_BYTES = 56 * 1024 * 1024

IN_TM = 512
SEQ_BLOCK = 512
SSM_CHUNK = 2048
SSM_SUB = 8
SSM_ROWS = 1024
ATTN_BLK = 512
MERGE_TM = 1024
MERGE_ROWS = 256
CONV_HALO = 32
CONV_ROWS = 64

MLA_SLOT = LANES
MLA_VT_ROWS = MLA_V + 16
LOG2E = math.log2(math.e)
NSTATE = SSM_GROUPS * SSM_STATE
NEG_BIG = -1e30

F32 = jnp.float32
BF16 = jnp.bfloat16


def _params(*sem):
    return pltpu.CompilerParams(dimension_semantics=sem, vmem_limit_bytes=VMEM_LIMIT_BYTES)


def _resident(l, shape):
    nd = len(shape)
    return pl.BlockSpec((None,) + tuple(shape), lambda *_: (l,) + (0,) * nd, pipeline_mode=pl.Buffered(1))


def _dot(a, b):
    return jnp.dot(a, b, preferred_element_type=F32)


def _dot_nt(a, b):
    return lax.dot_general(a, b, (((1,), (1,)), ((), ())), preferred_element_type=F32)


def _sigmoid(x):
    return 1.0 / (1.0 + jnp.exp(-x))


def _silu(x):
    return x * _sigmoid(x)


IN_GROUPS = (
    ("conv", 3 * BRANCH_W, F32),
    ("mla", MLA_Q_RANK + MLA_KV_RANK + 2 * LANES, F32),
    ("bz", BRANCH_W, F32),
    ("ssm", 2 * BRANCH_W, F32),
    ("swa", 2 * BRANCH_W, BF16),
    ("dz", BRANCH_W, F32),
)
IN_EXT_WIDTH = sum(w for _, w, _ in IN_GROUPS)


def _in_proj_kernel(x_ref, w_ref, *out_refs):
    xb = x_ref[...].astype(BF16)
    start = 0
    for (_, width, dtype), o_ref in zip(IN_GROUPS, out_refs):
        o_ref[...] = _dot(xb, w_ref[:, start:start + width]).astype(dtype)
        start += width


def _in_proj(l, x2d, w_in_ext):
    t = x2d.shape[0]
    return pl.pallas_call(
        _in_proj_kernel,
        grid=(t // IN_TM,),
        in_specs=[pl.BlockSpec((IN_TM, D_MODEL), lambda i: (i, 0)),
                  _resident(l, (D_MODEL, IN_EXT_WIDTH))],
        out_specs=[pl.BlockSpec((IN_TM, w), lambda i: (i, 0)) for _, w, _ in IN_GROUPS],
        out_shape=[jax.ShapeDtypeStruct((t, w), d) for _, w, d in IN_GROUPS],
        compiler_params=_params("parallel"),
        name="in_proj",
    )(x2d, w_in_ext)


def _conv_kernel(h_ref, cw_ref, cb_ref, ng_ref, nb_ref, pw2_ref, o_ref, hbuf, ybuf):
    i = pl.program_id(1)
    bs = SEQ_BLOCK
    w = BRANCH_W

    @pl.when(i == 0)
    def _():
        hbuf[0:CONV_HALO, :] = jnp.zeros((CONV_HALO, w), F32)

    @pl.when(i > 0)
    def _():
        hbuf[0:CONV_HALO, :] = hbuf[bs:bs + CONV_HALO, :]

    hbuf[CONV_HALO:, :] = h_ref[:, 0:w] * _sigmoid(h_ref[:, w:2 * w])

    base = CONV_HALO - (CONV_W - 1)
    tile = 8

    for c in range(bs // CONV_ROWS):
        r0 = c * CONV_ROWS
        acc = jnp.broadcast_to(cb_ref[...], (CONV_ROWS, w))
        for r in range(tile):
            span = CONV_ROWS + (tile if r else 0)
            part = None
            for j in range((r - base) % tile, CONV_W, tile):
                lo = r0 + base + j - r
                term = cw_ref[j:j + 1, :] * hbuf[lo:lo + span, :]
                part = term if part is None else part + term
            acc = acc + part[r:r + CONV_ROWS, :]
        mu = jnp.mean(acc, axis=-1, keepdims=True)
        d = acc - mu
        var = jnp.mean(d * d, axis=-1, keepdims=True)
        y = d * lax.rsqrt(var + LN_EPS) * ng_ref[...] + nb_ref[...]
        ybuf[r0:r0 + CONV_ROWS, :] = _silu(y).astype(BF16)
    o_ref[...] = (_dot(ybuf[...], pw2_ref[...]) * _silu(h_ref[:, 2 * w:3 * w])).astype(o_ref.dtype)


def _conv_branch(l, conv_in, conv_w, conv_b, norm_g, norm_b, w_pw2, batch, seq):
    bs = SEQ_BLOCK
    w = BRANCH_W
    return pl.pallas_call(
        _conv_kernel,
        grid=(batch, seq // bs),
        in_specs=[pl.BlockSpec((None, bs, 3 * w), lambda b, i: (b, i, 0)),
                  _resident(l, (CONV_W, w)), _resident(l, (1, w)), _resident(l, (1, w)), _resident(l, (1, w)),
                  _resident(l, (w, w))],
        out_specs=pl.BlockSpec((None, bs, w), lambda b, i: (b, i, 0)),
        out_shape=jax.ShapeDtypeStruct((batch, seq, w), BF16),
        scratch_shapes=[pltpu.VMEM((bs + CONV_HALO, w), F32), pltpu.VMEM((bs, w), BF16)],
        compiler_params=_params("parallel", "arbitrary"),
        name="conv_branch",
    )(conv_in.reshape(batch, seq, 3 * w), conv_w, conv_b, norm_g, norm_b, w_pw2)


def _cols(width, col, rows, per_batch):
    return pl.BlockSpec((rows, width), lambda b, i: (b * per_batch + i, col))


SSM_BOUND = SSM_CHUNK // SSM_SUB
assert SSM_BOUND % 8 == 0


def _ssm_kernel(u_ref, cz_ref, bc_ref, cc_ref, gc_ref, sc_ref, d_ref, wglu_ref, o_ref,
                carry, ubuf, zbuf, obuf, hbuf, bp_ref, cpt_ref, g_ref):
    i = pl.program_id(1)
    n = NSTATE
    w = BRANCH_W
    m = SSM_BOUND

    @pl.when((pl.program_id(0) == 0) & (i == 0))
    def _():
        shift = lambda a, k: lax.shift_right_logical(a, jnp.full(a.shape, k, jnp.int32))
        row_g = shift(lax.broadcasted_iota(jnp.int32, (w, n), 0), int(math.log2(SSM_GROUP)))
        col_g = shift(lax.broadcasted_iota(jnp.int32, (w, n), 1), int(math.log2(SSM_STATE)))
        same_group = row_g == col_g
        in_g = shift(lax.broadcasted_iota(jnp.int32, (w, w), 0), int(math.log2(SSM_GROUP)))
        out_g = shift(lax.broadcasted_iota(jnp.int32, (w, w), 1), int(math.log2(SSM_GROUP)))
        same_group_out = in_g == out_g
        for k in range(SSM_SUB):
            for c in range(2):
                for src, dst in ((bc_ref, bp_ref), (cc_ref, cpt_ref)):
                    wide = jnp.concatenate([src[k, c]] * (n // LANES), axis=1)
                    dst[k, :, c * n:(c + 1) * n] = jnp.where(same_group, wide, 0.0).astype(BF16)
            wide = jnp.concatenate([gc_ref[k]] * (w // LANES), axis=1)
            g_ref[k] = jnp.where(same_group_out, wide, 0.0).astype(BF16)

    @pl.when(i == 0)
    def _():
        carry[...] = jnp.zeros_like(carry)

    nt = w // LANES
    for k in range(nt):
        ubuf[k] = u_ref[:, k * LANES:(k + 1) * LANES]
        zbuf[k] = cz_ref[:, k * LANES:(k + 1) * LANES]

    def rows_of(buf, p):
        return jnp.concatenate([buf[k, pl.ds(p, m, stride=SSM_SUB), :] for k in range(nt)], axis=1)

    us = [rows_of(ubuf, p) for p in range(SSM_SUB)]
    ub = [u.astype(BF16) for u in us]

    x = None
    for p in range(SSM_SUB):
        t = _dot(ub[p], bp_ref[p])
        x = t if x is None else x + t

    def lag(p):
        y = _dot(ub[p], g_ref[0])
        for d in range(1, p + 1):
            y = y + _dot(ub[p - d], g_ref[d])
        return y

    tile = 8
    first = lax.broadcasted_iota(jnp.int32, (tile, n), 0) == 0
    cr = carry[0:1, :]
    ci = carry[1:2, :]
    lags = []
    tiles_per_lag = (m // tile) // SSM_SUB
    for k in range(m // tile):
        if k % tiles_per_lag == 0:
            lags.append(lag(k // tiles_per_lag))
        tr = x[k * tile:(k + 1) * tile, 0:n]
        ti = x[k * tile:(k + 1) * tile, n:2 * n]
        for lvl in range(3):
            s = 1 << lvl
            ar = sc_ref[lvl, :, 0:n]
            ai = sc_ref[lvl, :, n:2 * n]
            sr = pltpu.roll(tr, s, 0)
            si = pltpu.roll(ti, s, 0)
            tr, ti = tr + (ar * sr - ai * si), ti + (ar * si + ai * sr)
        pr = sc_ref[3, :, 0:n]
        pi = sc_ref[3, :, n:2 * n]
        tr, ti = tr + (pr * cr - pi * ci), ti + (pr * ci + pi * cr)
        hbuf[k * tile:(k + 1) * tile, 0:n] = jnp.where(first, cr, pltpu.roll(tr, 1, 0))
        hbuf[k * tile:(k + 1) * tile, n:2 * n] = jnp.where(first, ci, pltpu.roll(ti, 1, 0))
        cr = tr[tile - 1:tile, :]
        ci = ti[tile - 1:tile, :]
    carry[0:1, :] = cr
    carry[1:2, :] = ci
    hcat = hbuf[...].astype(BF16)

    ys, gs = {}, {}
    for t in range(SSM_SUB + 2):
        if t < SSM_SUB:
            ys[t] = _dot_nt(hcat, cpt_ref[t]) + lags[t] + d_ref[...] * us[t]
        if 0 <= t - 1 < SSM_SUB:
            gs[t - 1] = _dot(jax.nn.gelu(ys.pop(t - 1)).astype(BF16), wglu_ref[...])
        if 0 <= t - 2 < SSM_SUB:
            p = t - 2
            g = gs.pop(p)
            out = g[:, 0:w] * _sigmoid(g[:, w:2 * w]) * _silu(rows_of(zbuf, p))
            for k in range(nt):
                obuf[k, pl.ds(p, m, stride=SSM_SUB), :] = out[:, k * LANES:(k + 1) * LANES]
    o_ref[...] = jnp.concatenate([obuf[k] for k in range(nt)], axis=1).astype(o_ref.dtype)


def _ssm_branch(l, ssm_in, bc, cc, gc, powers, d_skip, w_glu, batch, seq):
    rows = SSM_CHUNK
    w = BRANCH_W
    return pl.pallas_call(
        _ssm_kernel,
        grid=(batch, seq // rows),
        in_specs=[_cols(w, 0, rows, seq // rows), _cols(w, 1, rows, seq // rows),
                  _resident(l, (SSM_SUB, 2, w, LANES)), _resident(l, (SSM_SUB, 2, w, LANES)),
                  _resident(l, (SSM_SUB, w, LANES)),
                  _resident(l, (4, 8, 2 * NSTATE)), _resident(l, (1, w)), _resident(l, (w, 2 * w))],
        out_specs=pl.BlockSpec((None, rows, w), lambda b, i: (b, i, 0)),
        out_shape=jax.ShapeDtypeStruct((batch, seq, w), BF16),
        scratch_shapes=[pltpu.VMEM((8, NSTATE), F32)] + [pltpu.VMEM((w // LANES, rows, LANES), F32)] * 3 + [
                        pltpu.VMEM((SSM_BOUND, 2 * NSTATE), F32),
                        pltpu.VMEM((SSM_SUB, w, 2 * NSTATE), BF16), pltpu.VMEM((SSM_SUB, w, 2 * NSTATE), BF16),
                        pltpu.VMEM((SSM_SUB, w, w), BF16)],
        compiler_params=_params("arbitrary", "arbitrary"),
        name="ssm_branch",
    )(ssm_in, ssm_in, bc, cc, gc, powers, d_skip, w_glu)


SWA_GROUP = SWA_HEADS // SWA_KV_HEADS
SWA_BLOCKS = SEQ_BLOCK // WINDOW


SWA_VT_ROWS = SWA_HEAD_DIM + 16


def _swa_kernel(l, sink_ref, q_ref, kv_ref, prev_ref, dz_ref, o_ref, kbuf, vtbuf, ytbuf):
    i = pl.program_id(1)
    hd = SWA_HEAD_DIM
    kv_w = SWA_KV_HEADS * hd
    win = WINDOW
    cols = SWA_GROUP * win

    to_log2 = hd ** -0.5 * LOG2E
    kbuf[0:win, :] = prev_ref[:, 0:kv_w]
    kbuf[win:, :] = kv_ref[:, 0:kv_w]
    v_all = jnp.concatenate([prev_ref[:, kv_w:2 * kv_w], kv_ref[:, kv_w:2 * kv_w]], axis=0)
    vt = v_all.astype(F32).T
    ones = jnp.ones((SWA_VT_ROWS - hd, vt.shape[1]), BF16)
    for kvh in range(SWA_KV_HEADS):
        vtbuf[kvh * SWA_VT_ROWS:kvh * SWA_VT_ROWS + hd, :] = vt[kvh * hd:(kvh + 1) * hd, :].astype(BF16)
        vtbuf[kvh * SWA_VT_ROWS + hd:(kvh + 1) * SWA_VT_ROWS, :] = ones

    krow = lax.broadcasted_iota(jnp.int32, (2 * win, cols), 0)
    qpos = lax.broadcasted_iota(jnp.int32, (2 * win, cols), 1) % win + win
    rel = qpos - krow
    band = (rel >= 0) & (rel < win)
    first_head = lax.broadcasted_iota(jnp.int32, (1, cols), 1) < win
    kmin = jnp.where(i == 0, win, 0)

    scores = {}
    for j in range(SWA_BLOCKS):
        for kvh in range(SWA_KV_HEADS):
            q2 = jnp.concatenate(
                [q_ref[j * win:(j + 1) * win, (kvh * SWA_GROUP + g) * hd:(kvh * SWA_GROUP + g + 1) * hd]
                 for g in range(SWA_GROUP)], axis=0)
            k2 = kbuf[j * win:(j + 2) * win, kvh * hd:(kvh + 1) * hd]
            scores[j, kvh] = _dot_nt(k2, q2) * to_log2
    for j in range(SWA_BLOCKS):
        valid = band & (krow >= kmin) if j == 0 else band
        for kvh in range(SWA_KV_HEADS):
            s = jnp.where(valid, scores[j, kvh], NEG_BIG)
            sink = jnp.where(first_head, sink_ref[l, kvh * SWA_GROUP] * LOG2E,
                             sink_ref[l, kvh * SWA_GROUP + 1] * LOG2E)
            m = jnp.maximum(jnp.max(s, axis=0, keepdims=True), sink)
            p = jnp.exp2(s - m).astype(BF16)
            acc = _dot(vtbuf[kvh * SWA_VT_ROWS:(kvh + 1) * SWA_VT_ROWS, j * win:(j + 2) * win], p)
            denom = acc[hd:hd + 1, :] + jnp.exp2(sink - m)
            o = acc[0:hd, :] * (1.0 / denom)
            for g in range(SWA_GROUP):
                h = kvh * SWA_GROUP + g
                ytbuf[h * hd:(h + 1) * hd, j * win:(j + 1) * win] = o[:, g * win:(g + 1) * win]
    o_ref[...] = (ytbuf[...].T * _silu(dz_ref[...])).astype(o_ref.dtype)


def _swa_branch(l, swa_in, dz, sinks, batch, seq):
    bs = SEQ_BLOCK
    w = BRANCH_W
    nsb = seq // bs
    wins = seq // WINDOW
    prev_spec = pl.BlockSpec((WINDOW, w), lambda b, i: (b * wins + jnp.maximum(i * SWA_BLOCKS - 1, 0), 1))
    return pl.pallas_call(
        functools.partial(_swa_kernel, l),
        grid=(batch, nsb),
        in_specs=[pl.BlockSpec(memory_space=pltpu.SMEM),
                  _cols(w, 0, bs, nsb), _cols(w, 1, bs, nsb), prev_spec, _cols(w, 0, bs, nsb)],
        out_specs=pl.BlockSpec((None, bs, w), lambda b, i: (b, i, 0)),
        out_shape=jax.ShapeDtypeStruct((batch, seq, w), BF16),
        scratch_shapes=[pltpu.VMEM((bs + WINDOW, SWA_KV_HEADS * SWA_HEAD_DIM), BF16),
                        pltpu.VMEM((SWA_KV_HEADS * SWA_VT_ROWS, bs + WINDOW), BF16), pltpu.VMEM((w, bs), F32)],
        compiler_params=_params("parallel", "arbitrary"),
        name="swa_branch",
    )(sinks, swa_in, swa_in, swa_in, dz)


MLA_QK = MLA_HEADS * MLA_SLOT


def _rms(x, g):
    ms = jnp.mean(x * x, axis=-1, keepdims=True)
    return x * lax.rsqrt(ms + RMS_EPS) * g


def _mla_prep_kernel(cq_ref, ckv_ref, krot_ref, cos_ref, sin_ref, qg_ref, kvg_ref, wuq_ref, wukv_ref,
                     qt_ref, k_ref, vt_ref):
    scale = (MLA_NOPE + MLA_ROPE) ** -0.5 * LOG2E
    cq = _rms(cq_ref[...], qg_ref[...]).astype(BF16)
    ckv = _rms(ckv_ref[:, 0:MLA_KV_RANK], kvg_ref[...]).astype(BF16)
    cos = cos_ref[...]
    sin = sin_ref[...]
    k_rope = ckv_ref[:, MLA_KV_RANK:MLA_KV_RANK + LANES] * cos + krot_ref[...] * sin
    kv = _dot(ckv, wukv_ref[...])
    for h in range(MLA_HEADS):
        lo = h * MLA_SLOT
        qm = _dot(cq, wuq_ref[:, lo:lo + MLA_SLOT])
        qr = _dot(cq, wuq_ref[:, MLA_QK + lo:MLA_QK + lo + MLA_SLOT])
        qt_ref[lo:lo + MLA_SLOT, :] = ((qm * cos + qr * sin) * scale).T.astype(BF16)
        k_ref[:, lo:lo + MLA_SLOT] = (kv[:, lo:lo + MLA_SLOT] + k_rope).astype(BF16)
    vt = kv[:, MLA_QK:].T.astype(BF16)
    ones = jnp.ones((MLA_VT_ROWS - MLA_V, vt.shape[1]), BF16)
    for h in range(MLA_HEADS):
        vt_ref[h * MLA_VT_ROWS:h * MLA_VT_ROWS + MLA_V, :] = vt[h * MLA_V:(h + 1) * MLA_V, :]
        vt_ref[h * MLA_VT_ROWS + MLA_V:(h + 1) * MLA_VT_ROWS, :] = ones


def _mla_prep(l, mla_in, cos_t, sin_t, q_norm_g, kv_norm_g, w_uq_ext, w_ukv_ext, seq):
    t = mla_in.shape[0]
    tm = ATTN_BLK
    nsb = seq // tm
    vw = MLA_HEADS * MLA_V
    vt_rows = MLA_HEADS * MLA_VT_ROWS
    return pl.pallas_call(
        _mla_prep_kernel,
        grid=(t // tm,),
        in_specs=[pl.BlockSpec((tm, MLA_Q_RANK), lambda i: (i, 0)),
                  pl.BlockSpec((tm, MLA_Q_RANK), lambda i: (i, 1)),
                  pl.BlockSpec((tm, LANES), lambda i: (i, 2 * MLA_Q_RANK // LANES)),
                  pl.BlockSpec((tm, LANES), lambda i: (i % nsb, 0)),
                  pl.BlockSpec((tm, LANES), lambda i: (i % nsb, 0)),
                  _resident(l, (1, MLA_Q_RANK)), _resident(l, (1, MLA_KV_RANK)),
                  _resident(l, (MLA_Q_RANK, 2 * MLA_QK)), _resident(l, (MLA_KV_RANK, MLA_QK + vw))],
        out_specs=[pl.BlockSpec((None, MLA_QK, tm), lambda i: (i, 0, 0)),
                   pl.BlockSpec((tm, MLA_QK), lambda i: (i, 0)),
                   pl.BlockSpec((None, vt_rows, tm), lambda i: (i, 0, 0))],
        out_shape=[jax.ShapeDtypeStruct((t // tm, MLA_QK, tm), BF16), jax.ShapeDtypeStruct((t, MLA_QK), BF16),
                   jax.ShapeDtypeStruct((t // tm, vt_rows, tm), BF16)],
        compiler_params=_params("parallel"),
        name="mla_prep",
    )(mla_in, mla_in, mla_in, cos_t, sin_t, q_norm_g, kv_norm_g, w_uq_ext, w_ukv_ext)


def _attn_kernel(qt_ref, k_ref, vt_ref, bz_ref, o_ref, m_sc, acc_sc, s_sc):
    i = pl.program_id(1)
    blk = ATTN_BLK
    krow = lax.broadcasted_iota(jnp.int32, (blk, blk), 0)
    qcol = lax.broadcasted_iota(jnp.int32, (blk, blk), 1)
    causal = krow <= qcol
    m_sc[...] = jnp.full(m_sc.shape, NEG_BIG, F32)
    acc_sc[...] = jnp.zeros(acc_sc.shape, F32)

    all_heads = tuple(range(MLA_HEADS))

    def score(j, slot, heads=all_heads):
        start = j * blk if isinstance(j, int) else pl.multiple_of(j * blk, blk)
        for h in heads:
            kb = k_ref[pl.ds(start, blk), h * MLA_SLOT:(h + 1) * MLA_SLOT]
            s_sc[slot, h] = _dot(kb, qt_ref[h * MLA_SLOT:(h + 1) * MLA_SLOT, :])

    def absorb(j, slot, masked, heads=all_heads):
        for h in heads:
            s = s_sc[slot, h]
            if masked:
                s = jnp.where(causal, s, NEG_BIG)
            m_prev = m_sc[h]
            m_new = jnp.maximum(m_prev, jnp.max(s, axis=0, keepdims=True))
            alpha = jnp.exp2(m_prev - m_new)
            p = jnp.exp2(s - m_new).astype(BF16)
            vb = vt_ref[j, h * MLA_VT_ROWS:(h + 1) * MLA_VT_ROWS, :]
            acc_sc[h] = alpha * acc_sc[h] + _dot(vb, p)
            m_sc[h] = m_new

    score(0, 0)

    def body(t, c):
        j = 2 * t
        for h in all_heads:
            score(j + 1, 1, (h,))
            absorb(j, 0, False, (h,))
        for h in all_heads:
            score(j + 2, 0, (h,))
            absorb(j + 1, 1, False, (h,))
        return c

    lax.fori_loop(0, i // 2, body, 0)

    @pl.when(i % 2 == 0)
    def _():
        absorb(i, 0, True)

    @pl.when(i % 2 == 1)
    def _():
        for h in all_heads:
            score(i, 1, (h,))
            absorb(i - 1, 0, False, (h,))
        absorb(i, 1, True)
    outs = [acc_sc[h, 0:MLA_V, :] * (1.0 / acc_sc[h, MLA_V:MLA_V + 1, :]) for h in range(MLA_HEADS)]
    y = jnp.concatenate(outs, axis=0).T
    o_ref[...] = (y * _silu(bz_ref[...])).astype(o_ref.dtype)


def _mla_attention(qt, k, vt, bz, batch, seq):
    blk = ATTN_BLK
    nq = seq // blk
    vt_rows = MLA_HEADS * MLA_VT_ROWS
    return pl.pallas_call(
        _attn_kernel,
        grid=(batch, nq),
        in_specs=[pl.BlockSpec((None, MLA_QK, blk), lambda b, i: (b * nq + i, 0, 0)),
                  pl.BlockSpec((None, seq, MLA_QK), lambda b, i: (b, 0, 0)),
                  pl.BlockSpec((None, nq, vt_rows, blk), lambda b, i: (b, 0, 0, 0)),
                  _cols(BRANCH_W, 0, blk, nq)],
        out_specs=pl.BlockSpec((None, blk, BRANCH_W), lambda b, i: (b, i, 0)),
        out_shape=jax.ShapeDtypeStruct((batch, seq, BRANCH_W), BF16),
        scratch_shapes=[pltpu.VMEM((MLA_HEADS, 1, blk), F32), pltpu.VMEM((MLA_HEADS, MLA_VT_ROWS, blk), F32),
                        pltpu.VMEM((2, MLA_HEADS, blk, blk), F32)],
        compiler_params=_params("parallel", "arbitrary"),
        name="mla_attention",
    )(qt, k.reshape(batch, seq, MLA_QK), vt.reshape(batch, nq, vt_rows, blk), bz)


def _merge_kernel(x_ref, ya_ref, yb_ref, yc_ref, yd_ref, p_ref, wm_ref, bm_ref, wb_ref, wo_ref,
                  lng_ref, lnb_ref, wp_ref, wpg_ref, pg_ref, o_ref):
    d = D_MODEL
    y_refs = (ya_ref, yb_ref, yc_ref, yd_ref)
    parts = MERGE_TM // MERGE_ROWS

    def gated_merge(r):
        rows = slice(r * MERGE_ROWS, (r + 1) * MERGE_ROWS)
        xb = x_ref[rows, :].astype(BF16)
        merged = None
        for n, y_ref in enumerate(y_refs):
            gate = _sigmoid(_dot(xb, wm_ref[:, n * d:(n + 1) * d]) + bm_ref[:, n * d:(n + 1) * d])
            term = gate * _dot(y_ref[rows, :], wb_ref[n])
            merged = term if merged is None else merged + term
        return merged

    def post_norm(r, merged):
        rows = slice(r * MERGE_ROWS, (r + 1) * MERGE_ROWS)
        z = DEEPNORM_ALPHA * x_ref[rows, :] + _dot(merged.astype(BF16), wo_ref[...])
        mu = jnp.mean(z, axis=-1, keepdims=True)
        zc = z - mu
        var = jnp.mean(zc * zc, axis=-1, keepdims=True)
        return zc * lax.rsqrt(var + LN_EPS) * lng_ref[...] + lnb_ref[...]

    def embed(r, xn):
        rows = slice(r * MERGE_ROWS, (r + 1) * MERGE_ROWS)
        e = _dot(p_ref[rows, :].astype(BF16), wp_ref[...]) * _sigmoid(_dot(xn.astype(BF16), wpg_ref[...]))
        o_ref[rows, :] = xn + _rms(e, pg_ref[...])

    merged, xn = {}, {}
    for t in range(parts + 2):
        if t < parts:
            merged[t] = gated_merge(t)
        if 0 <= t - 1 < parts:
            xn[t - 1] = post_norm(t - 1, merged.pop(t - 1))
        if 0 <= t - 2 < parts:
            embed(t - 2, xn.pop(t - 2))


def _merge(l, x2d, ys, p3d, w_merge, b_merge, w_branch, w_out, ln_g, ln_b, w_ple, w_ple_gate, ple_norm_g):
    t = x2d.shape[0]
    tm = MERGE_TM
    d = D_MODEL
    row_spec = lambda width: pl.BlockSpec((tm, width), lambda i: (i, 0))
    return pl.pallas_call(
        _merge_kernel,
        grid=(t // tm,),
        in_specs=[row_spec(d)] + [row_spec(BRANCH_W)] * N_BRANCH + [
                  pl.BlockSpec((None, tm, PLE_DIM), lambda i: (l, i, 0)),
                  _resident(l, (d, N_BRANCH * d)), _resident(l, (1, N_BRANCH * d)),
                  _resident(l, (N_BRANCH, BRANCH_W, d)), _resident(l, (d, d)),
                  _resident(l, (1, d)), _resident(l, (1, d)), _resident(l, (PLE_DIM, d)), _resident(l, (d, d)),
                  _resident(l, (1, d))],
        out_specs=row_spec(d),
        out_shape=jax.ShapeDtypeStruct((t, d), F32),
        compiler_params=_params("parallel"),
        name="merge",
    )(x2d, *ys, p3d, w_merge, b_merge, w_branch, w_out, ln_g, ln_b, w_ple, w_ple_gate, ple_norm_g)


def _rot_cols(w):
    half = w.shape[-1] // 2
    return jnp.concatenate([-w[..., half:], w[..., :half]], axis=-1)


def _zero_cols(w, n):
    return jnp.zeros(w.shape[:-1] + (n,), w.dtype)


def _rope_slot(w):
    return jnp.concatenate([_zero_cols(w, MLA_NOPE), w, _zero_cols(w, MLA_SLOT - MLA_NOPE - MLA_ROPE)], axis=-1)


def _in_weights(w_in):
    o = [0]
    for s in (256, 256, 256, 256, 128, 32, 256, 256, 256, 256, 128, 128, 256):
        o.append(o[-1] + s)
    col = lambda k: w_in[..., o[k]:o[k + 1]]
    a_val, a_gate, a_z, c_q, c_kv, k_r, b_z, u, c_z, q, k, v, d_z = (col(k) for k in range(13))
    groups = [a_val, a_gate, a_z, c_q, c_kv, _rope_slot(k_r), _rope_slot(_rot_cols(k_r)), b_z, u, c_z, q, k, v, d_z]
    return jnp.concatenate(groups, axis=-1).astype(BF16)


def _uq_weights(w_uq):
    hd = MLA_NOPE + MLA_ROPE
    main, rot = [], []
    for h in range(MLA_HEADS):
        head = w_uq[..., h * hd:(h + 1) * hd]
        main.append(jnp.concatenate([head, _zero_cols(w_uq, MLA_SLOT - hd)], axis=-1))
        rot.append(_rope_slot(_rot_cols(head[..., MLA_NOPE:])))
    return jnp.concatenate(main + rot, axis=-1).astype(BF16)


def _ukv_weights(w_ukv):
    hd = MLA_NOPE + MLA_V
    ks, vs = [], []
    for h in range(MLA_HEADS):
        ks.append(jnp.concatenate([w_ukv[..., h * hd:h * hd + MLA_NOPE], _zero_cols(w_ukv, MLA_SLOT - MLA_NOPE)],
                                  axis=-1))
        vs.append(w_ukv[..., h * hd + MLA_NOPE:(h + 1) * hd])
    return jnp.concatenate(ks + vs, axis=-1).astype(BF16)


def _rope_tables(seq):
    pos = jnp.arange(seq, dtype=F32)
    inv_freq = ROPE_THETA ** (-jnp.arange(0, MLA_ROPE, 2, dtype=F32) / MLA_ROPE)
    ang = pos[:, None] * inv_freq[None, :]
    cos, sin = jnp.cos(ang), jnp.sin(ang)
    ones = jnp.ones((seq, MLA_NOPE), F32)
    zeros = jnp.zeros((seq, MLA_NOPE), F32)
    pad = jnp.zeros((seq, MLA_SLOT - MLA_NOPE - MLA_ROPE), F32)
    cos_t = jnp.concatenate([ones, cos, cos, pad], axis=1)
    sin_t = jnp.concatenate([zeros, sin, sin, pad], axis=1)
    return cos_t, sin_t


def _ssm_weights(a_re, a_im, log_dt, b_re, b_im, c_re, c_im):
    g, p, h = SSM_GROUPS, SSM_STATE, SSM_GROUP
    nl = a_re.shape[0]
    dt = jnp.exp(log_dt.astype(F32))[..., None]
    lr, li = a_re.astype(F32), a_im.astype(F32)
    mag = jnp.exp(lr * dt)
    lb_re, lb_im = mag * jnp.cos(li * dt), mag * jnp.sin(li * dt)
    den = lr * lr + li * li
    nr, ni = lb_re - 1.0, lb_im
    f_re = ((nr * lr + ni * li) / den)[..., None]
    f_im = ((ni * lr - nr * li) / den)[..., None]
    bb_re = (f_re * b_re - f_im * b_im)[:, None]
    bb_im = (f_re * b_im + f_im * b_re)[:, None]
    c_re, c_im = c_re.astype(F32)[:, None], c_im.astype(F32)[:, None]
    prs, pis = [jnp.ones_like(lb_re)], [jnp.zeros_like(lb_im)]
    for _ in range(SSM_SUB):
        qr, qi = prs[-1], pis[-1]
        prs.append(qr * lb_re - qi * lb_im)
        pis.append(qr * lb_im + qi * lb_re)
    pw_r, pw_i = jnp.stack(prs, axis=1), jnp.stack(pis, axis=1)
    rows_gh = lambda m: m.reshape(nl, SSM_SUB, g * h, m.shape[-1])
    lanes = lambda m: jnp.tile(m, LANES // m.shape[-1])
    qr, qi = pw_r[:, :SSM_SUB, :, :, None], pw_i[:, :SSM_SUB, :, :, None]
    vr = jnp.swapaxes(qr * bb_re - qi * bb_im, -1, -2)
    vi = jnp.swapaxes(qr * bb_im + qi * bb_re, -1, -2)
    bc = lanes(jnp.stack([rows_gh(vr), rows_gh(vi)], axis=2))[:, ::-1]
    qr, qi = pw_r[:, 1:, :, None, :], pw_i[:, 1:, :, None, :]
    cc = lanes(jnp.stack([rows_gh(c_re * qr - c_im * qi), -rows_gh(c_re * qi + c_im * qr)], axis=2))
    lag = jnp.sum(vr[..., None, :] * c_re[:, :, :, None] - vi[..., None, :] * c_im[:, :, :, None], axis=-1)
    gc = lanes(rows_gh(lag))
    mr, mi = pw_r[:, SSM_SUB].reshape(nl, 1, g * p), pw_i[:, SSM_SUB].reshape(nl, 1, g * p)
    mu = [(mr, mi)]
    for _ in range(7):
        qr, qi = mu[-1]
        mu.append((qr * mr - qi * mi, qr * mi + qi * mr))
    cat = lambda q: jnp.concatenate(q, axis=-1)
    rowid = jnp.arange(8)[:, None]
    tables = [jnp.where(rowid >= (1 << lvl), cat(mu[(1 << lvl) - 1]), 0.0) for lvl in range(3)]
    tables.append(jnp.concatenate([cat(q) for q in mu], axis=1))
    return bc, cc, gc, jnp.stack(tables, axis=1)


def kernel(x, p, w_in, w_merge, b_merge, conv_w, conv_b, conv_norm_g, conv_norm_b, w_pw2, mla_q_norm_g,
           mla_kv_norm_g, w_uq, w_ukv, ssm_a_re, ssm_a_im, ssm_log_dt, ssm_b_re, ssm_b_im, ssm_c_re,
           ssm_c_im, ssm_d, w_glu, attn_sinks, w_branch, w_out, ln_g, ln_b, w_ple, w_ple_gate, ple_norm_g):
    batch, seq, d = x.shape
    t = batch * seq
    rows = lambda a: a.reshape(DEPTH, 1, -1).astype(F32)
    bf = lambda a: a.astype(BF16)
    cos_t, sin_t = _rope_tables(seq)
    w_in_ext, w_uq_ext, w_ukv_ext = _in_weights(w_in), _uq_weights(w_uq), _ukv_weights(w_ukv)
    bp, cp, gd, sc = _ssm_weights(ssm_a_re, ssm_a_im, ssm_log_dt, ssm_b_re, ssm_b_im, ssm_c_re, ssm_c_im)
    conv_p = (conv_w.astype(F32), rows(conv_b), rows(conv_norm_g), rows(conv_norm_b), bf(w_pw2))
    mla_p = (rows(mla_q_norm_g), rows(mla_kv_norm_g), w_uq_ext, w_ukv_ext)
    ssm_p = (bp, cp, gd, sc, rows(ssm_d), bf(w_glu))
    merge_p = (bf(w_merge), rows(b_merge), bf(w_branch), bf(w_out), rows(ln_g), rows(ln_b), bf(w_ple),
               bf(w_ple_gate), rows(ple_norm_g))
    sinks = attn_sinks.astype(F32)
    p3d = p.reshape(DEPTH, t, PLE_DIM)
    x2d = x.reshape(t, d)
    for l in range(DEPTH):
        conv_in, mla_in, bz, ssm_in, swa_in, dz = _in_proj(l, x2d, w_in_ext)
        y_a = _conv_branch(l, conv_in, *conv_p, batch, seq)
        qt, k, vt = _mla_prep(l, mla_in, cos_t, sin_t, *mla_p, seq)
        y_b = _mla_attention(qt, k, vt, bz, batch, seq)
        y_c = _ssm_branch(l, ssm_in, *ssm_p, batch, seq)
        y_d = _swa_branch(l, swa_in, dz, sinks, batch, seq)
        ys = [y.reshape(t, BRANCH_W) for y in (y_a, y_b, y_c, y_d)]
        x2d = _merge(l, x2d, ys, p3d, *merge_p)
    return x2d.reshape(batch, seq, d)
```

```python
import functools
import math

import jax
import jax.numpy as jnp
from jax import lax
from jax.experimental import pallas as pl
from jax.experimental.pallas import tpu as pltpu

D_MODEL = 1024
DEPTH = 4
PLE_DIM = 256
N_BRANCH = 4
BRANCH_W = 256
CONV_W = 31
MLA_HEADS = 4
MLA_NOPE = 64
MLA_ROPE = 32
MLA_V = 64
MLA_Q_RANK = 256
MLA_KV_RANK = 128
ROPE_THETA = 10000.0
SSM_GROUP = 16
SSM_GROUPS = 16
SSM_STATE = 64
SWA_HEADS = 4
SWA_KV_HEADS = 2
SWA_HEAD_DIM = 64
WINDOW = 128
DEEPNORM_ALPHA = (2.0 * DEPTH) ** 0.25
LN_EPS = 1e-5
RMS_EPS = 1e-6

LANES = 128
VMEM_LIMIT_BYTES = 56 * 1024 * 1024

IN_TM = 512
SEQ_BLOCK = 512
SSM_CHUNK = 2048
SSM_SUB = 8
ATTN_BLK = 512
MERGE_TM = 1024
MERGE_ROWS = 256
CONV_HALO = 32
CONV_ROWS = 512

MLA_SLOT = LANES
MLA_VT_ROWS = MLA_V + 16
LOG2E = math.log2(math.e)
NSTATE = SSM_GROUPS * SSM_STATE
NEG_BIG = -1e30

F32 = jnp.float32
BF16 = jnp.bfloat16


def _params(*sem):
    return pltpu.CompilerParams(dimension_semantics=sem, vmem_limit_bytes=VMEM_LIMIT_BYTES)


def _resident(l, shape):
    nd = len(shape)
    return pl.BlockSpec((None,) + tuple(shape), lambda *_: (l,) + (0,) * nd, pipeline_mode=pl.Buffered(1))


def _dot(a, b):
    return jnp.dot(a, b, preferred_element_type=F32)


def _dot_nt(a, b):
    return lax.dot_general(a, b, (((1,), (1,)), ((), ())), preferred_element_type=F32)


def _sigmoid(x):
    return 1.0 / (1.0 + jnp.exp(-x))


def _silu(x):
    return x * _sigmoid(x)


IN_GROUPS = (
    ("conv", 3 * BRANCH_W, F32),
    ("mla", MLA_Q_RANK + MLA_KV_RANK + 2 * LANES, F32),
    ("bz", BRANCH_W, F32),
    ("ssm", 2 * BRANCH_W, F32),
    ("swa", 2 * BRANCH_W, BF16),
    ("dz", BRANCH_W, F32),
)
IN_EXT_WIDTH = sum(w for _, w, _ in IN_GROUPS)


def _in_proj_kernel(x_ref, w_ref, *out_refs):
    xb = x_ref[...].astype(BF16)
    start = 0
    for (_, width, dtype), o_ref in zip(IN_GROUPS, out_refs):
        o_ref[...] = _dot(xb, w_ref[:, start:start + width]).astype(dtype)
        start += width


def _in_proj(l, x2d, w_in_ext):
    t = x2d.shape[0]
    return pl.pallas_call(
        _in_proj_kernel,
        grid=(t // IN_TM,),
        in_specs=[pl.BlockSpec((IN_TM, D_MODEL), lambda i: (i, 0)),
                  _resident(l, (D_MODEL, IN_EXT_WIDTH))],
        out_specs=[pl.BlockSpec((IN_TM, w), lambda i: (i, 0)) for _, w, _ in IN_GROUPS],
        out_shape=[jax.ShapeDtypeStruct((t, w), d) for _, w, d in IN_GROUPS],
        compiler_params=_params("parallel"),
        name="in_proj",
    )(x2d, w_in_ext)


def _conv_kernel(h_ref, cw_ref, cb_ref, ng_ref, nb_ref, pw2_ref, o_ref, hbuf, ybuf):
    i = pl.program_id(1)
    bs = SEQ_BLOCK
    w = BRANCH_W

    @pl.when(i == 0)
    def _():
        hbuf[0:CONV_HALO, :] = jnp.zeros((CONV_HALO, w), F32)

    @pl.when(i > 0)
    def _():
        hbuf[0:CONV_HALO, :] = hbuf[bs:bs + CONV_HALO, :]

    hbuf[CONV_HALO:, :] = h_ref[:, 0:w] * _sigmoid(h_ref[:, w:2 * w])

    base = CONV_HALO - (CONV_W - 1)
    tile = 8

    for c in range(bs // CONV_ROWS):
        r0 = c * CONV_ROWS
        acc = jnp.broadcast_to(cb_ref[...], (CONV_ROWS, w))
        for r in range(tile):
            span = CONV_ROWS + (tile if r else 0)
            part = None
            for j in range((r - base) % tile, CONV_W, tile):
                lo = r0 + base + j - r
                term = cw_ref[j:j + 1, :] * hbuf[lo:lo + span, :]
                part = term if part is None else part + term
            acc = acc + part[r:r + CONV_ROWS, :]
        mu = jnp.mean(acc, axis=-1, keepdims=True)
        d = acc - mu
        var = jnp.mean(d * d, axis=-1, keepdims=True)
        y = d * lax.rsqrt(var + LN_EPS) * ng_ref[...] + nb_ref[...]
        ybuf[r0:r0 + CONV_ROWS, :] = _silu(y).astype(BF16)
    o_ref[...] = (_dot(ybuf[...], pw2_ref[...]) * _silu(h_ref[:, 2 * w:3 * w])).astype(o_ref.dtype)


def _conv_branch(l, conv_in, conv_w, conv_b, norm_g, norm_b, w_pw2, batch, seq):
    bs = SEQ_BLOCK
    w = BRANCH_W
    return pl.pallas_call(
        _conv_kernel,
        grid=(batch, seq // bs),
        in_specs=[pl.BlockSpec((None, bs, 3 * w), lambda b, i: (b, i, 0)),
                  _resident(l, (CONV_W, w)), _resident(l, (1, w)), _resident(l, (1, w)), _resident(l, (1, w)),
                  _resident(l, (w, w))],
        out_specs=pl.BlockSpec((None, bs, w), lambda b, i: (b, i, 0)),
        out_shape=jax.ShapeDtypeStruct((batch, seq, w), BF16),
        scratch_shapes=[pltpu.VMEM((bs + CONV_HALO, w), F32), pltpu.VMEM((bs, w), BF16)],
        compiler_params=_params("parallel", "arbitrary"),
        name="conv_branch",
    )(conv_in.reshape(batch, seq, 3 * w), conv_w, conv_b, norm_g, norm_b, w_pw2)


def _cols(width, col, rows, per_batch):
    return pl.BlockSpec((rows, width), lambda b, i: (b * per_batch + i, col))


SSM_BOUND = SSM_CHUNK // SSM_SUB
assert SSM_BOUND % 8 == 0


def _ssm_kernel(u_ref, cz_ref, bc_ref, cc_ref, gc_ref, sc_ref, d_ref, wglu_ref, o_ref,
                carry, ubuf, zbuf, obuf, hbuf, bp_ref, cpt_ref, g_ref):
    i = pl.program_id(1)
    n = NSTATE
    w = BRANCH_W
    m = SSM_BOUND

    @pl.when((pl.program_id(0) == 0) & (i == 0))
    def _():
        shift = lambda a, k: lax.shift_right_logical(a, jnp.full(a.shape, k, jnp.int32))
        row_g = shift(lax.broadcasted_iota(jnp.int32, (w, n), 0), int(math.log2(SSM_GROUP)))
        col_g = shift(lax.broadcasted_iota(jnp.int32, (w, n), 1), int(math.log2(SSM_STATE)))
        same_group = row_g == col_g
        in_g = shift(lax.broadcasted_iota(jnp.int32, (w, w), 0), int(math.log2(SSM_GROUP)))
        out_g = shift(lax.broadcasted_iota(jnp.int32, (w, w), 1), int(math.log2(SSM_GROUP)))
        same_group_out = in_g == out_g
        for k in range(SSM_SUB):
            for c in range(2):
                for src, dst in ((bc_ref, bp_ref), (cc_ref, cpt_ref)):
                    wide = jnp.concatenate([src[k, c]] * (n // LANES), axis=1)
                    dst[k, :, c * n:(c + 1) * n] = jnp.where(same_group, wide, 0.0).astype(BF16)
            wide = jnp.concatenate([gc_ref[k]] * (w // LANES), axis=1)
            g_ref[k] = jnp.where(same_group_out, wide, 0.0).astype(BF16)

    @pl.when(i == 0)
    def _():
        carry[...] = jnp.zeros_like(carry)

    nt = w // LANES
    for k in range(nt):
        ubuf[k] = u_ref[:, k * LANES:(k + 1) * LANES]
        zbuf[k] = cz_ref[:, k * LANES:(k + 1) * LANES]

    def rows_of(buf, p):
        return jnp.concatenate([buf[k, pl.ds(p, m, stride=SSM_SUB), :] for k in range(nt)], axis=1)

    us = [rows_of(ubuf, p) for p in range(SSM_SUB)]
    ub = [u.astype(BF16) for u in us]

    x = None
    for p in range(SSM_SUB):
        t = _dot(ub[p], bp_ref[p])
        x = t if x is None else x + t

    def lag(p):
        y = _dot(ub[p], g_ref[0])
        for d in range(1, p + 1):
            y = y + _dot(ub[p - d], g_ref[d])
        return y

    tile = 8
    first = lax.broadcasted_iota(jnp.int32, (tile, n), 0) == 0
    cr = carry[0:1, :]
    ci = carry[1:2, :]
    lags = []
    tiles_per_lag = (m // tile) // SSM_SUB
    for k in range(m // tile):
        if k % tiles_per_lag == 0:
            lags.append(lag(k // tiles_per_lag))
        tr = x[k * tile:(k + 1) * tile, 0:n]
        ti = x[k * tile:(k + 1) * tile, n:2 * n]
        for lvl in range(3):
            s = 1 << lvl
            ar = sc_ref[lvl, :, 0:n]
            ai = sc_ref[lvl, :, n:2 * n]
            sr = pltpu.roll(tr, s, 0)
            si = pltpu.roll(ti, s, 0)
            tr, ti = tr + (ar * sr - ai * si), ti + (ar * si + ai * sr)
        pr = sc_ref[3, :, 0:n]
        pi = sc_ref[3, :, n:2 * n]
        tr, ti = tr + (pr * cr - pi * ci), ti + (pr * ci + pi * cr)
        hbuf[k * tile:(k + 1) * tile, 0:n] = jnp.where(first, cr, pltpu.roll(tr, 1, 0))
        hbuf[k * tile:(k + 1) * tile, n:2 * n] = jnp.where(first, ci, pltpu.roll(ti, 1, 0))
        cr = tr[tile - 1:tile, :]
        ci = ti[tile - 1:tile, :]
    carry[0:1, :] = cr
    carry[1:2, :] = ci
    hcat = hbuf[...].astype(BF16)

    ys, gs = {}, {}
    for t in range(SSM_SUB + 2):
        if t < SSM_SUB:
            ys[t] = _dot_nt(hcat, cpt_ref[t]) + lags[t] + d_ref[...] * us[t]
        if 0 <= t - 1 < SSM_SUB:
            gs[t - 1] = _dot(jax.nn.gelu(ys.pop(t - 1)).astype(BF16), wglu_ref[...])
        if 0 <= t - 2 < SSM_SUB:
            p = t - 2
            g = gs.pop(p)
            out = g[:, 0:w] * _sigmoid(g[:, w:2 * w]) * _silu(rows_of(zbuf, p))
            for k in range(nt):
                obuf[k, pl.ds(p, m, stride=SSM_SUB), :] = out[:, k * LANES:(k + 1) * LANES]
    o_ref[...] = jnp.concatenate([obuf[k] for k in range(nt)], axis=1).astype(o_ref.dtype)


def _ssm_branch(l, ssm_in, bc, cc, gc, powers, d_skip, w_glu, batch, seq):
    rows = SSM_CHUNK
    w = BRANCH_W
    return pl.pallas_call(
        _ssm_kernel,
        grid=(batch, seq // rows),
        in_specs=[_cols(w, 0, rows, seq // rows), _cols(w, 1, rows, seq // rows),
                  _resident(l, (SSM_SUB, 2, w, LANES)), _resident(l, (SSM_SUB, 2, w, LANES)),
                  _resident(l, (SSM_SUB, w, LANES)),
                  _resident(l, (4, 8, 2 * NSTATE)), _resident(l, (1, w)), _resident(l, (w, 2 * w))],
        out_specs=pl.BlockSpec((None, rows, w), lambda b, i: (b, i, 0)),
        out_shape=jax.ShapeDtypeStruct((batch, seq, w), BF16),
        scratch_shapes=[pltpu.VMEM((8, NSTATE), F32)] + [pltpu.VMEM((w // LANES, rows, LANES), F32)] * 3 + [
                        pltpu.VMEM((SSM_BOUND, 2 * NSTATE), F32),
                        pltpu.VMEM((SSM_SUB, w, 2 * NSTATE), BF16), pltpu.VMEM((SSM_SUB, w, 2 * NSTATE), BF16),
                        pltpu.VMEM((SSM_SUB, w, w), BF16)],
        compiler_params=_params("arbitrary", "arbitrary"),
        name="ssm_branch",
    )(ssm_in, ssm_in, bc, cc, gc, powers, d_skip, w_glu)


SWA_GROUP = SWA_HEADS // SWA_KV_HEADS
SWA_BLOCKS = SEQ_BLOCK // WINDOW


SWA_VT_ROWS = SWA_HEAD_DIM + 16


def _swa_kernel(l, sink_ref, q_ref, kv_ref, prev_ref, dz_ref, o_ref, kbuf, vtbuf, ytbuf):
    i = pl.program_id(1)
    hd = SWA_HEAD_DIM
    kv_w = SWA_KV_HEADS * hd
    win = WINDOW
    cols = SWA_GROUP * win

    to_log2 = hd ** -0.5 * LOG2E
    kbuf[0:win, :] = prev_ref[:, 0:kv_w]
    kbuf[win:, :] = kv_ref[:, 0:kv_w]
    v_all = jnp.concatenate([prev_ref[:, kv_w:2 * kv_w], kv_ref[:, kv_w:2 * kv_w]], axis=0)
    vt = v_all.astype(F32).T
    ones = jnp.ones((SWA_VT_ROWS - hd, vt.shape[1]), BF16)
    for kvh in range(SWA_KV_HEADS):
        vtbuf[kvh * SWA_VT_ROWS:kvh * SWA_VT_ROWS + hd, :] = vt[kvh * hd:(kvh + 1) * hd, :].astype(BF16)
        vtbuf[kvh * SWA_VT_ROWS + hd:(kvh + 1) * SWA_VT_ROWS, :] = ones

    krow = lax.broadcasted_iota(jnp.int32, (2 * win, cols), 0)
    qpos = lax.broadcasted_iota(jnp.int32, (2 * win, cols), 1) % win + win
    rel = qpos - krow
    band = (rel >= 0) & (rel < win)
    first_head = lax.broadcasted_iota(jnp.int32, (1, cols), 1) < win
    kmin = jnp.where(i == 0, win, 0)

    scores = {}
    for j in range(SWA_BLOCKS):
        for kvh in range(SWA_KV_HEADS):
            q2 = jnp.concatenate(
                [q_ref[j * win:(j + 1) * win, (kvh * SWA_GROUP + g) * hd:(kvh * SWA_GROUP + g + 1) * hd]
                 for g in range(SWA_GROUP)], axis=0)
            k2 = kbuf[j * win:(j + 2) * win, kvh * hd:(kvh + 1) * hd]
            scores[j, kvh] = _dot_nt(k2, q2) * to_log2
    for j in range(SWA_BLOCKS):
        valid = band & (krow >= kmin) if j == 0 else band
        for kvh in range(SWA_KV_HEADS):
            s = jnp.where(valid, scores[j, kvh], NEG_BIG)
            sink = jnp.where(first_head, sink_ref[l, kvh * SWA_GROUP] * LOG2E,
                             sink_ref[l, kvh * SWA_GROUP + 1] * LOG2E)
            m = jnp.maximum(jnp.max(s, axis=0, keepdims=True), sink)
            p = jnp.exp2(s - m).astype(BF16)
            acc = _dot(vtbuf[kvh * SWA_VT_ROWS:(kvh + 1) * SWA_VT_ROWS, j * win:(j + 2) * win], p)
            denom = acc[hd:hd + 1, :] + jnp.exp2(sink - m)
            o = acc[0:hd, :] * (1.0 / denom)
            for g in range(SWA_GROUP):
                h = kvh * SWA_GROUP + g
                ytbuf[h * hd:(h + 1) * hd, j * win:(j + 1) * win] = o[:, g * win:(g + 1) * win]
    o_ref[...] = (ytbuf[...].T * _silu(dz_ref[...])).astype(o_ref.dtype)


def _swa_branch(l, swa_in, dz, sinks, batch, seq):
    bs = SEQ_BLOCK
    w = BRANCH_W
    nsb = seq // bs
    wins = seq // WINDOW
    prev_spec = pl.BlockSpec((WINDOW, w), lambda b, i: (b * wins + jnp.maximum(i * SWA_BLOCKS - 1, 0), 1))
    return pl.pallas_call(
        functools.partial(_swa_kernel, l),
        grid=(batch, nsb),
        in_specs=[pl.BlockSpec(memory_space=pltpu.SMEM),
                  _cols(w, 0, bs, nsb), _cols(w, 1, bs, nsb), prev_spec, _cols(w, 0, bs, nsb)],
        out_specs=pl.BlockSpec((None, bs, w), lambda b, i: (b, i, 0)),
        out_shape=jax.ShapeDtypeStruct((batch, seq, w), BF16),
        scratch_shapes=[pltpu.VMEM((bs + WINDOW, SWA_KV_HEADS * SWA_HEAD_DIM), BF16),
                        pltpu.VMEM((SWA_KV_HEADS * SWA_VT_ROWS, bs + WINDOW), BF16), pltpu.VMEM((w, bs), F32)],
        compiler_params=_params("parallel", "arbitrary"),
        name="swa_branch",
    )(sinks, swa_in, swa_in, swa_in, dz)


MLA_QK = MLA_HEADS * MLA_SLOT


def _rms(x, g):
    ms = jnp.mean(x * x, axis=-1, keepdims=True)
    return x * lax.rsqrt(ms + RMS_EPS) * g


def _mla_prep_kernel(cq_ref, ckv_ref, krot_ref, cos_ref, sin_ref, qg_ref, kvg_ref, wuq_ref, wukv_ref,
                     qt_ref, k_ref, vt_ref):
    scale = (MLA_NOPE + MLA_ROPE) ** -0.5 * LOG2E
    cq = _rms(cq_ref[...], qg_ref[...]).astype(BF16)
    ckv = _rms(ckv_ref[:, 0:MLA_KV_RANK], kvg_ref[...]).astype(BF16)
    cos = cos_ref[...]
    sin = sin_ref[...]
    k_rope = ckv_ref[:, MLA_KV_RANK:MLA_KV_RANK + LANES] * cos + krot_ref[...] * sin
    kv = _dot(ckv, wukv_ref[...])
    for h in range(MLA_HEADS):
        lo = h * MLA_SLOT
        qm = _dot(cq, wuq_ref[:, lo:lo + MLA_SLOT])
        qr = _dot(cq, wuq_ref[:, MLA_QK + lo:MLA_QK + lo + MLA_SLOT])
        qt_ref[lo:lo + MLA_SLOT, :] = ((qm * cos + qr * sin) * scale).T.astype(BF16)
        k_ref[:, lo:lo + MLA_SLOT] = (kv[:, lo:lo + MLA_SLOT] + k_rope).astype(BF16)
    vt = kv[:, MLA_QK:].T.astype(BF16)
    ones = jnp.ones((MLA_VT_ROWS - MLA_V, vt.shape[1]), BF16)
    for h in range(MLA_HEADS):
        vt_ref[h * MLA_VT_ROWS:h * MLA_VT_ROWS + MLA_V, :] = vt[h * MLA_V:(h + 1) * MLA_V, :]
        vt_ref[h * MLA_VT_ROWS + MLA_V:(h + 1) * MLA_VT_ROWS, :] = ones


def _mla_prep(l, mla_in, cos_t, sin_t, q_norm_g, kv_norm_g, w_uq_ext, w_ukv_ext, seq):
    t = mla_in.shape[0]
    tm = ATTN_BLK
    nsb = seq // tm
    vw = MLA_HEADS * MLA_V
    vt_rows = MLA_HEADS * MLA_VT_ROWS
    return pl.pallas_call(
        _mla_prep_kernel,
        grid=(t // tm,),
        in_specs=[pl.BlockSpec((tm, MLA_Q_RANK), lambda i: (i, 0)),
                  pl.BlockSpec((tm, MLA_Q_RANK), lambda i: (i, 1)),
                  pl.BlockSpec((tm, LANES), lambda i: (i, 2 * MLA_Q_RANK // LANES)),
                  pl.BlockSpec((tm, LANES), lambda i: (i % nsb, 0)),
                  pl.BlockSpec((tm, LANES), lambda i: (i % nsb, 0)),
                  _resident(l, (1, MLA_Q_RANK)), _resident(l, (1, MLA_KV_RANK)),
                  _resident(l, (MLA_Q_RANK, 2 * MLA_QK)), _resident(l, (MLA_KV_RANK, MLA_QK + vw))],
        out_specs=[pl.BlockSpec((None, MLA_QK, tm), lambda i: (i, 0, 0)),
                   pl.BlockSpec((tm, MLA_QK), lambda i: (i, 0)),
                   pl.BlockSpec((None, vt_rows, tm), lambda i: (i, 0, 0))],
        out_shape=[jax.ShapeDtypeStruct((t // tm, MLA_QK, tm), BF16), jax.ShapeDtypeStruct((t, MLA_QK), BF16),
                   jax.ShapeDtypeStruct((t // tm, vt_rows, tm), BF16)],
        compiler_params=_params("parallel"),
        name="mla_prep",
    )(mla_in, mla_in, mla_in, cos_t, sin_t, q_norm_g, kv_norm_g, w_uq_ext, w_ukv_ext)


def _attn_kernel(qt_ref, k_ref, vt_ref, bz_ref, o_ref, m_sc, acc_sc, s_sc):
    i = pl.program_id(1)
    blk = ATTN_BLK
    krow = lax.broadcasted_iota(jnp.int32, (blk, blk), 0)
    qcol = lax.broadcasted_iota(jnp.int32, (blk, blk), 1)
    causal = krow <= qcol
    m_sc[...] = jnp.full(m_sc.shape, NEG_BIG, F32)
    acc_sc[...] = jnp.zeros(acc_sc.shape, F32)

    all_heads = tuple(range(MLA_HEADS))

    def score(j, slot, heads=all_heads):
        start = j * blk if isinstance(j, int) else pl.multiple_of(j * blk, blk)
        for h in heads:
            kb = k_ref[pl.ds(start, blk), h * MLA_SLOT:(h + 1) * MLA_SLOT]
            s_sc[slot, h] = _dot(kb, qt_ref[h * MLA_SLOT:(h + 1) * MLA_SLOT, :])

    def absorb(j, slot, masked, heads=all_heads):
        for h in heads:
            s = s_sc[slot, h]
            if masked:
                s = jnp.where(causal, s, NEG_BIG)
            m_prev = m_sc[h]
            m_new = jnp.maximum(m_prev, jnp.max(s, axis=0, keepdims=True))
            alpha = jnp.exp2(m_prev - m_new)
            p = jnp.exp2(s - m_new).astype(BF16)
            vb = vt_ref[j, h * MLA_VT_ROWS:(h + 1) * MLA_VT_ROWS, :]
            acc_sc[h] = alpha * acc_sc[h] + _dot(vb, p)
            m_sc[h] = m_new

    score(0, 0)

    def body(t, c):
        j = 2 * t
        for h in all_heads:
            score(j + 1, 1, (h,))
            absorb(j, 0, False, (h,))
        for h in all_heads:
            score(j + 2, 0, (h,))
            absorb(j + 1, 1, False, (h,))
        return c

    lax.fori_loop(0, i // 2, body, 0)

    @pl.when(i % 2 == 0)
    def _():
        absorb(i, 0, True)

    @pl.when(i % 2 == 1)
    def _():
        for h in all_heads:
            score(i, 1, (h,))
            absorb(i - 1, 0, False, (h,))
        absorb(i, 1, True)
    outs = [acc_sc[h, 0:MLA_V, :] * (1.0 / acc_sc[h, MLA_V:MLA_V + 1, :]) for h in range(MLA_HEADS)]
    y = jnp.concatenate(outs, axis=0).T
    o_ref[...] = (y * _silu(bz_ref[...])).astype(o_ref.dtype)


def _mla_attention(qt, k, vt, bz, batch, seq):
    blk = ATTN_BLK
    nq = seq // blk
    vt_rows = MLA_HEADS * MLA_VT_ROWS
    return pl.pallas_call(
        _attn_kernel,
        grid=(batch, nq),
        in_specs=[pl.BlockSpec((None, MLA_QK, blk), lambda b, i: (b * nq + i, 0, 0)),
                  pl.BlockSpec((None, seq, MLA_QK), lambda b, i: (b, 0, 0)),
                  pl.BlockSpec((None, nq, vt_rows, blk), lambda b, i: (b, 0, 0, 0)),
                  _cols(BRANCH_W, 0, blk, nq)],
        out_specs=pl.BlockSpec((None, blk, BRANCH_W), lambda b, i: (b, i, 0)),
        out_shape=jax.ShapeDtypeStruct((batch, seq, BRANCH_W), BF16),
        scratch_shapes=[pltpu.VMEM((MLA_HEADS, 1, blk), F32), pltpu.VMEM((MLA_HEADS, MLA_VT_ROWS, blk), F32),
                        pltpu.VMEM((2, MLA_HEADS, blk, blk), F32)],
        compiler_params=_params("parallel", "arbitrary"),
        name="mla_attention",
    )(qt, k.reshape(batch, seq, MLA_QK), vt.reshape(batch, nq, vt_rows, blk), bz)


def _merge_kernel(x_ref, ya_ref, yb_ref, yc_ref, yd_ref, p_ref, wm_ref, bm_ref, wb_ref, wo_ref,
                  lng_ref, lnb_ref, wp_ref, wpg_ref, pg_ref, o_ref):
    d = D_MODEL
    y_refs = (ya_ref, yb_ref, yc_ref, yd_ref)
    parts = MERGE_TM // MERGE_ROWS

    def gated_merge(r):
        rows = slice(r * MERGE_ROWS, (r + 1) * MERGE_ROWS)
        xb = x_ref[rows, :].astype(BF16)
        merged = None
        for n, y_ref in enumerate(y_refs):
            gate = _sigmoid(_dot(xb, wm_ref[:, n * d:(n + 1) * d]) + bm_ref[:, n * d:(n + 1) * d])
            term = gate * _dot(y_ref[rows, :], wb_ref[n])
            merged = term if merged is None else merged + term
        return merged

    def post_norm(r, merged):
        rows = slice(r * MERGE_ROWS, (r + 1) * MERGE_ROWS)
        z = DEEPNORM_ALPHA * x_ref[rows, :] + _dot(merged.astype(BF16), wo_ref[...])
        mu = jnp.mean(z, axis=-1, keepdims=True)
        zc = z - mu
        var = jnp.mean(zc * zc, axis=-1, keepdims=True)
        return zc * lax.rsqrt(var + LN_EPS) * lng_ref[...] + lnb_ref[...]

    def embed(r, xn):
        rows = slice(r * MERGE_ROWS, (r + 1) * MERGE_ROWS)
        e = _dot(p_ref[rows, :].astype(BF16), wp_ref[...]) * _sigmoid(_dot(xn.astype(BF16), wpg_ref[...]))
        o_ref[rows, :] = xn + _rms(e, pg_ref[...])

    merged, xn = {}, {}
    for t in range(parts + 2):
        if t < parts:
            merged[t] = gated_merge(t)
        if 0 <= t - 1 < parts:
            xn[t - 1] = post_norm(t - 1, merged.pop(t - 1))
        if 0 <= t - 2 < parts:
            embed(t - 2, xn.pop(t - 2))


def _merge(l, x2d, ys, p3d, w_merge, b_merge, w_branch, w_out, ln_g, ln_b, w_ple, w_ple_gate, ple_norm_g):
    t = x2d.shape[0]
    tm = MERGE_TM
    d = D_MODEL
    row_spec = lambda width: pl.BlockSpec((tm, width), lambda i: (i, 0))
    return pl.pallas_call(
        _merge_kernel,
        grid=(t // tm,),
        in_specs=[row_spec(d)] + [row_spec(BRANCH_W)] * N_BRANCH + [
                  pl.BlockSpec((None, tm, PLE_DIM), lambda i: (l, i, 0)),
                  _resident(l, (d, N_BRANCH * d)), _resident(l, (1, N_BRANCH * d)),
                  _resident(l, (N_BRANCH, BRANCH_W, d)), _resident(l, (d, d)),
                  _resident(l, (1, d)), _resident(l, (1, d)), _resident(l, (PLE_DIM, d)), _resident(l, (d, d)),
                  _resident(l, (1, d))],
        out_specs=row_spec(d),
        out_shape=jax.ShapeDtypeStruct((t, d), F32),
        compiler_params=_params("parallel"),
        name="merge",
    )(x2d, *ys, p3d, w_merge, b_merge, w_branch, w_out, ln_g, ln_b, w_ple, w_ple_gate, ple_norm_g)


def _rot_cols(w):
    half = w.shape[-1] // 2
    return jnp.concatenate([-w[..., half:], w[..., :half]], axis=-1)


def _zero_cols(w, n):
    return jnp.zeros(w.shape[:-1] + (n,), w.dtype)


def _rope_slot(w):
    return jnp.concatenate([_zero_cols(w, MLA_NOPE), w, _zero_cols(w, MLA_SLOT - MLA_NOPE - MLA_ROPE)], axis=-1)


def _in_weights(w_in):
    o = [0]
    for s in (256, 256, 256, 256, 128, 32, 256, 256, 256, 256, 128, 128, 256):
        o.append(o[-1] + s)
    col = lambda k: w_in[..., o[k]:o[k + 1]]
    a_val, a_gate, a_z, c_q, c_kv, k_r, b_z, u, c_z, q, k, v, d_z = (col(k) for k in range(13))
    groups = [a_val, a_gate, a_z, c_q, c_kv, _rope_slot(k_r), _rope_slot(_rot_cols(k_r)), b_z, u, c_z, q, k, v, d_z]
    return jnp.concatenate(groups, axis=-1).astype(BF16)


def _uq_weights(w_uq):
    hd = MLA_NOPE + MLA_ROPE
    main, rot = [], []
    for h in range(MLA_HEADS):
        head = w_uq[..., h * hd:(h + 1) * hd]
        main.append(jnp.concatenate([head, _zero_cols(w_uq, MLA_SLOT - hd)], axis=-1))
        rot.append(_rope_slot(_rot_cols(head[..., MLA_NOPE:])))
    return jnp.concatenate(main + rot, axis=-1).astype(BF16)


def _ukv_weights(w_ukv):
    hd = MLA_NOPE + MLA_V
    ks, vs = [], []
    for h in range(MLA_HEADS):
        ks.append(jnp.concatenate([w_ukv[..., h * hd:h * hd + MLA_NOPE], _zero_cols(w_ukv, MLA_SLOT - MLA_NOPE)],
                                  axis=-1))
        vs.append(w_ukv[..., h * hd + MLA_NOPE:(h + 1) * hd])
    return jnp.concatenate(ks + vs, axis=-1).astype(BF16)


def _rope_tables(seq):
    pos = jnp.arange(seq, dtype=F32)
    inv_freq = ROPE_THETA ** (-jnp.arange(0, MLA_ROPE, 2, dtype=F32) / MLA_ROPE)
    ang = pos[:, None] * inv_freq[None, :]
    cos, sin = jnp.cos(ang), jnp.sin(ang)
    ones = jnp.ones((seq, MLA_NOPE), F32)
    zeros = jnp.zeros((seq, MLA_NOPE), F32)
    pad = jnp.zeros((seq, MLA_SLOT - MLA_NOPE - MLA_ROPE), F32)
    cos_t = jnp.concatenate([ones, cos, cos, pad], axis=1)
    sin_t = jnp.concatenate([zeros, sin, sin, pad], axis=1)
    return cos_t, sin_t


def _ssm_weights(a_re, a_im, log_dt, b_re, b_im, c_re, c_im):
    g, p, h = SSM_GROUPS, SSM_STATE, SSM_GROUP
    nl = a_re.shape[0]
    dt = jnp.exp(log_dt.astype(F32))[..., None]
    lr, li = a_re.astype(F32), a_im.astype(F32)
    mag = jnp.exp(lr * dt)
    lb_re, lb_im = mag * jnp.cos(li * dt), mag * jnp.sin(li * dt)
    den = lr * lr + li * li
    nr, ni = lb_re - 1.0, lb_im
    f_re = ((nr * lr + ni * li) / den)[..., None]
    f_im = ((ni * lr - nr * li) / den)[..., None]
    bb_re = (f_re * b_re - f_im * b_im)[:, None]
    bb_im = (f_re * b_im + f_im * b_re)[:, None]
    c_re, c_im = c_re.astype(F32)[:, None], c_im.astype(F32)[:, None]
    prs, pis = [jnp.ones_like(lb_re)], [jnp.zeros_like(lb_im)]
    for _ in range(SSM_SUB):
        qr, qi = prs[-1], pis[-1]
        prs.append(qr * lb_re - qi * lb_im)
        pis.append(qr * lb_im + qi * lb_re)
    pw_r, pw_i = jnp.stack(prs, axis=1), jnp.stack(pis, axis=1)
    rows_gh = lambda m: m.reshape(nl, SSM_SUB, g * h, m.shape[-1])
    lanes = lambda m: jnp.tile(m, LANES // m.shape[-1])
    qr, qi = pw_r[:, :SSM_SUB, :, :, None], pw_i[:, :SSM_SUB, :, :, None]
    vr = jnp.swapaxes(qr * bb_re - qi * bb_im, -1, -2)
    vi = jnp.swapaxes(qr * bb_im + qi * bb_re, -1, -2)
    bc = lanes(jnp.stack([rows_gh(vr), rows_gh(vi)], axis=2))[:, ::-1]
    qr, qi = pw_r[:, 1:, :, None, :], pw_i[:, 1:, :, None, :]
    cc = lanes(jnp.stack([rows_gh(c_re * qr - c_im * qi), -rows_gh(c_re * qi + c_im * qr)], axis=2))
    lag = jnp.sum(vr[..., None, :] * c_re[:, :, :, None] - vi[..., None, :] * c_im[:, :, :, None], axis=-1)
    gc = lanes(rows_gh(lag))
    mr, mi = pw_r[:, SSM_SUB].reshape(nl, 1, g * p), pw_i[:, SSM_SUB].reshape(nl, 1, g * p)
    mu = [(mr, mi)]
    for _ in range(7):
        qr, qi = mu[-1]
        mu.append((qr * mr - qi * mi, qr * mi + qi * mr))
    cat = lambda q: jnp.concatenate(q, axis=-1)
    rowid = jnp.arange(8)[:, None]
    tables = [jnp.where(rowid >= (1 << lvl), cat(mu[(1 << lvl) - 1]), 0.0) for lvl in range(3)]
    tables.append(jnp.concatenate([cat(q) for q in mu], axis=1))
    return bc, cc, gc, jnp.stack(tables, axis=1)


def kernel(x, p, w_in, w_merge, b_merge, conv_w, conv_b, conv_norm_g, conv_norm_b, w_pw2, mla_q_norm_g,
           mla_kv_norm_g, w_uq, w_ukv, ssm_a_re, ssm_a_im, ssm_log_dt, ssm_b_re, ssm_b_im, ssm_c_re,
           ssm_c_im, ssm_d, w_glu, attn_sinks, w_branch, w_out, ln_g, ln_b, w_ple, w_ple_gate, ple_norm_g):
    batch, seq, d = x.shape
    t = batch * seq
    rows = lambda a: a.reshape(DEPTH, 1, -1).astype(F32)
    bf = lambda a: a.astype(BF16)
    cos_t, sin_t = _rope_tables(seq)
    w_in_ext, w_uq_ext, w_ukv_ext = _in_weights(w_in), _uq_weights(w_uq), _ukv_weights(w_ukv)
    bp, cp, gd, sc = _ssm_weights(ssm_a_re, ssm_a_im, ssm_log_dt, ssm_b_re, ssm_b_im, ssm_c_re, ssm_c_im)
    conv_p = (conv_w.astype(F32), rows(conv_b), rows(conv_norm_g), rows(conv_norm_b), bf(w_pw2))
    mla_p = (rows(mla_q_norm_g), rows(mla_kv_norm_g), w_uq_ext, w_ukv_ext)
    ssm_p = (bp, cp, gd, sc, rows(ssm_d), bf(w_glu))
    merge_p = (bf(w_merge), rows(b_merge), bf(w_branch), bf(w_out), rows(ln_g), rows(ln_b), bf(w_ple),
               bf(w_ple_gate), rows(ple_norm_g))
    sinks = attn_sinks.astype(F32)
    p3d = p.reshape(DEPTH, t, PLE_DIM)
    x2d = x.reshape(t, d)
    for l in range(DEPTH):
        conv_in, mla_in, bz, ssm_in, swa_in, dz = _in_proj(l, x2d, w_in_ext)
        y_a = _conv_branch(l, conv_in, *conv_p, batch, seq)
        qt, k, vt = _mla_prep(l, mla_in, cos_t, sin_t, *mla_p, seq)
        y_b = _mla_attention(qt, k, vt, bz, batch, seq)
        y_c = _ssm_branch(l, ssm_in, *ssm_p, batch, seq)
        y_d = _swa_branch(l, swa_in, dz, sinks, batch, seq)
        ys = [y.reshape(t, BRANCH_W) for y in (y_a, y_b, y_c, y_d)]
        x2d = _merge(l, x2d, ys, p3d, *merge_p)
    return x2d.reshape(batch, seq, d)
```

```python
import functools
import math

import jax
import jax.numpy as jnp
from jax import lax
from jax.experimental import pallas as pl
from jax.experimental.pallas import tpu as pltpu

D_MODEL = 1024
DEPTH = 4
PLE_DIM = 256
N_BRANCH = 4
BRANCH_W = 256
CONV_W = 31
MLA_HEADS = 4
MLA_NOPE = 64
MLA_ROPE = 32
MLA_V = 64
MLA_Q_RANK = 256
MLA_KV_RANK = 128
ROPE_THETA = 10000.0
SSM_GROUP = 16
SSM_GROUPS = 16
SSM_STATE = 64
SWA_HEADS = 4
SWA_KV_HEADS = 2
SWA_HEAD_DIM = 64
WINDOW = 128
DEEPNORM_ALPHA = (2.0 * DEPTH) ** 0.25
LN_EPS = 1e-5
RMS_EPS = 1e-6

LANES = 128
VMEM_LIMIT_BYTES = 56 * 1024 * 1024

IN_TM = 512
SEQ_BLOCK = 1024
SSM_CHUNK = 2048
SSM_SUB = 8
ATTN_BLK = 512
MERGE_TM = 1024
MERGE_ROWS = 256
CONV_HALO = 32
CONV_ROWS = 512

MLA_SLOT = LANES
MLA_VT_ROWS = MLA_V + 16
LOG2E = math.log2(math.e)
NSTATE = SSM_GROUPS * SSM_STATE
NEG_BIG = -1e30

F32 = jnp.float32
BF16 = jnp.bfloat16


def _params(*sem):
    return pltpu.CompilerParams(dimension_semantics=sem, vmem_limit_bytes=VMEM_LIMIT_BYTES)


def _resident(l, shape):
    nd = len(shape)
    return pl.BlockSpec((None,) + tuple(shape), lambda *_: (l,) + (0,) * nd, pipeline_mode=pl.Buffered(1))


def _dot(a, b):
    return jnp.dot(a, b, preferred_element_type=F32)


def _dot_nt(a, b):
    return lax.dot_general(a, b, (((1,), (1,)), ((), ())), preferred_element_type=F32)


def _sigmoid(x):
    return 1.0 / (1.0 + jnp.exp(-x))


def _silu(x):
    return x * _sigmoid(x)


IN_GROUPS = (
    ("conv", 3 * BRANCH_W, F32),
    ("mla", MLA_Q_RANK + MLA_KV_RANK + 2 * LANES, F32),
    ("bz", BRANCH_W, F32),
    ("ssm", 2 * BRANCH_W, F32),
    ("swa", 2 * BRANCH_W, BF16),
    ("dz", BRANCH_W, F32),
)
IN_EXT_WIDTH = sum(w for _, w, _ in IN_GROUPS)


def _in_proj_kernel(x_ref, w_ref, *out_refs):
    xb = x_ref[...].astype(BF16)
    start = 0
    for (_, width, dtype), o_ref in zip(IN_GROUPS, out_refs):
        o_ref[...] = _dot(xb, w_ref[:, start:start + width]).astype(dtype)
        start += width


def _in_proj(l, x2d, w_in_ext):
    t = x2d.shape[0]
    return pl.pallas_call(
        _in_proj_kernel,
        grid=(t // IN_TM,),
        in_specs=[pl.BlockSpec((IN_TM, D_MODEL), lambda i: (i, 0)),
                  _resident(l, (D_MODEL, IN_EXT_WIDTH))],
        out_specs=[pl.BlockSpec((IN_TM, w), lambda i: (i, 0)) for _, w, _ in IN_GROUPS],
        out_shape=[jax.ShapeDtypeStruct((t, w), d) for _, w, d in IN_GROUPS],
        compiler_params=_params("parallel"),
        name="in_proj",
    )(x2d, w_in_ext)


def _conv_kernel(h_ref, cw_ref, cb_ref, ng_ref, nb_ref, pw2_ref, o_ref, hbuf, ybuf):
    i = pl.program_id(1)
    bs = SEQ_BLOCK
    w = BRANCH_W

    @pl.when(i == 0)
    def _():
        hbuf[0:CONV_HALO, :] = jnp.zeros((CONV_HALO, w), F32)

    @pl.when(i > 0)
    def _():
        hbuf[0:CONV_HALO, :] = hbuf[bs:bs + CONV_HALO, :]

    hbuf[CONV_HALO:, :] = h_ref[:, 0:w] * _sigmoid(h_ref[:, w:2 * w])

    base = CONV_HALO - (CONV_W - 1)
    tile = 8

    for c in range(bs // CONV_ROWS):
        r0 = c * CONV_ROWS
        acc = jnp.broadcast_to(cb_ref[...], (CONV_ROWS, w))
        for r in range(tile):
            span = CONV_ROWS + (tile if r else 0)
            part = None
            for j in range((r - base) % tile, CONV_W, tile):
                lo = r0 + base + j - r
                term = cw_ref[j:j + 1, :] * hbuf[lo:lo + span, :]
                part = term if part is None else part + term
            acc = acc + part[r:r + CONV_ROWS, :]
        mu = jnp.mean(acc, axis=-1, keepdims=True)
        d = acc - mu
        var = jnp.mean(d * d, axis=-1, keepdims=True)
        y = d * lax.rsqrt(var + LN_EPS) * ng_ref[...] + nb_ref[...]
        ybuf[r0:r0 + CONV_ROWS, :] = _silu(y).astype(BF16)
    o_ref[...] = (_dot(ybuf[...], pw2_ref[...]) * _silu(h_ref[:, 2 * w:3 * w])).astype(o_ref.dtype)


def _conv_branch(l, conv_in, conv_w, conv_b, norm_g, norm_b, w_pw2, batch, seq):
    bs = SEQ_BLOCK
    w = BRANCH_W
    return pl.pallas_call(
        _conv_kernel,
        grid=(batch, seq // bs),
        in_specs=[pl.BlockSpec((None, bs, 3 * w), lambda b, i: (b, i, 0)),
                  _resident(l, (CONV_W, w)), _resident(l, (1, w)), _resident(l, (1, w)), _resident(l, (1, w)),
                  _resident(l, (w, w))],
        out_specs=pl.BlockSpec((None, bs, w), lambda b, i: (b, i, 0)),
        out_shape=jax.ShapeDtypeStruct((batch, seq, w), BF16),
        scratch_shapes=[pltpu.VMEM((bs + CONV_HALO, w), F32), pltpu.VMEM((bs, w), BF16)],
        compiler_params=_params("parallel", "arbitrary"),
        name="conv_branch",
    )(conv_in.reshape(batch, seq, 3 * w), conv_w, conv_b, norm_g, norm_b, w_pw2)


def _cols(width, col, rows, per_batch):
    return pl.BlockSpec((rows, width), lambda b, i: (b * per_batch + i, col))


SSM_BOUND = SSM_CHUNK // SSM_SUB
assert SSM_BOUND % 8 == 0


def _ssm_kernel(u_ref, cz_ref, bc_ref, cc_ref, gc_ref, sc_ref, d_ref, wglu_ref, o_ref,
                carry, ubuf, zbuf, obuf, hbuf, bp_ref, cpt_ref, g_ref):
    i = pl.program_id(1)
    n = NSTATE
    w = BRANCH_W
    m = SSM_BOUND

    @pl.when((pl.program_id(0) == 0) & (i == 0))
    def _():
        shift = lambda a, k: lax.shift_right_logical(a, jnp.full(a.shape, k, jnp.int32))
        row_g = shift(lax.broadcasted_iota(jnp.int32, (w, n), 0), int(math.log2(SSM_GROUP)))
        col_g = shift(lax.broadcasted_iota(jnp.int32, (w, n), 1), int(math.log2(SSM_STATE)))
        same_group = row_g == col_g
        in_g = shift(lax.broadcasted_iota(jnp.int32, (w, w), 0), int(math.log2(SSM_GROUP)))
        out_g = shift(lax.broadcasted_iota(jnp.int32, (w, w), 1), int(math.log2(SSM_GROUP)))
        same_group_out = in_g == out_g
        for k in range(SSM_SUB):
            for c in range(2):
                for src, dst in ((bc_ref, bp_ref), (cc_ref, cpt_ref)):
                    wide = jnp.concatenate([src[k, c]] * (n // LANES), axis=1)
                    dst[k, :, c * n:(c + 1) * n] = jnp.where(same_group, wide, 0.0).astype(BF16)
            wide = jnp.concatenate([gc_ref[k]] * (w // LANES), axis=1)
            g_ref[k] = jnp.where(same_group_out, wide, 0.0).astype(BF16)

    @pl.when(i == 0)
    def _():
        carry[...] = jnp.zeros_like(carry)

    nt = w // LANES
    for k in range(nt):
        ubuf[k] = u_ref[:, k * LANES:(k + 1) * LANES]
        zbuf[k] = cz_ref[:, k * LANES:(k + 1) * LANES]

    def rows_of(buf, p):
        return jnp.concatenate([buf[k, pl.ds(p, m, stride=SSM_SUB), :] for k in range(nt)], axis=1)

    us = [rows_of(ubuf, p) for p in range(SSM_SUB)]
    ub = [u.astype(BF16) for u in us]

    x = None
    for p in range(SSM_SUB):
        t = _dot(ub[p], bp_ref[p])
        x = t if x is None else x + t

    def lag(p):
        y = _dot(ub[p], g_ref[0])
        for d in range(1, p + 1):
            y = y + _dot(ub[p - d], g_ref[d])
        return y

    tile = 8
    first = lax.broadcasted_iota(jnp.int32, (tile, n), 0) == 0
    cr = carry[0:1, :]
    ci = carry[1:2, :]
    lags = []
    tiles_per_lag = (m // tile) // SSM_SUB
    for k in range(m // tile):
        if k % tiles_per_lag == 0:
            lags.append(lag(k // tiles_per_lag))
        tr = x[k * tile:(k + 1) * tile, 0:n]
        ti = x[k * tile:(k + 1) * tile, n:2 * n]
        for lvl in range(3):
            s = 1 << lvl
            ar = sc_ref[lvl, :, 0:n]
            ai = sc_ref[lvl, :, n:2 * n]
            sr = pltpu.roll(tr, s, 0)
            si = pltpu.roll(ti, s, 0)
            tr, ti = tr + (ar * sr - ai * si), ti + (ar * si + ai * sr)
        pr = sc_ref[3, :, 0:n]
        pi = sc_ref[3, :, n:2 * n]
        tr, ti = tr + (pr * cr - pi * ci), ti + (pr * ci + pi * cr)
        hbuf[k * tile:(k + 1) * tile, 0:n] = jnp.where(first, cr, pltpu.roll(tr, 1, 0))
        hbuf[k * tile:(k + 1) * tile, n:2 * n] = jnp.where(first, ci, pltpu.roll(ti, 1, 0))
        cr = tr[tile - 1:tile, :]
        ci = ti[tile - 1:tile, :]
    carry[0:1, :] = cr
    carry[1:2, :] = ci
    hcat = hbuf[...].astype(BF16)

    ys, gs = {}, {}
    for t in range(SSM_SUB + 2):
        if t < SSM_SUB:
            ys[t] = _dot_nt(hcat, cpt_ref[t]) + lags[t] + d_ref[...] * us[t]
        if 0 <= t - 1 < SSM_SUB:
            gs[t - 1] = _dot(jax.nn.gelu(ys.pop(t - 1)).astype(BF16), wglu_ref[...])
        if 0 <= t - 2 < SSM_SUB:
            p = t - 2
            g = gs.pop(p)
            out = g[:, 0:w] * _sigmoid(g[:, w:2 * w]) * _silu(rows_of(zbuf, p))
            for k in range(nt):
                obuf[k, pl.ds(p, m, stride=SSM_SUB), :] = out[:, k * LANES:(k + 1) * LANES]
    o_ref[...] = jnp.concatenate([obuf[k] for k in range(nt)], axis=1).astype(o_ref.dtype)


def _ssm_branch(l, ssm_in, bc, cc, gc, powers, d_skip, w_glu, batch, seq):
    rows = SSM_CHUNK
    w = BRANCH_W
    return pl.pallas_call(
        _ssm_kernel,
        grid=(batch, seq // rows),
        in_specs=[_cols(w, 0, rows, seq // rows), _cols(w, 1, rows, seq // rows),
                  _resident(l, (SSM_SUB, 2, w, LANES)), _resident(l, (SSM_SUB, 2, w, LANES)),
                  _resident(l, (SSM_SUB, w, LANES)),
                  _resident(l, (4, 8, 2 * NSTATE)), _resident(l, (1, w)), _resident(l, (w, 2 * w))],
        out_specs=pl.BlockSpec((None, rows, w), lambda b, i: (b, i, 0)),
        out_shape=jax.ShapeDtypeStruct((batch, seq, w), BF16),
        scratch_shapes=[pltpu.VMEM((8, NSTATE), F32)] + [pltpu.VMEM((w // LANES, rows, LANES), F32)] * 3 + [
                        pltpu.VMEM((SSM_BOUND, 2 * NSTATE), F32),
                        pltpu.VMEM((SSM_SUB, w, 2 * NSTATE), BF16), pltpu.VMEM((SSM_SUB, w, 2 * NSTATE), BF16),
                        pltpu.VMEM((SSM_SUB, w, w), BF16)],
        compiler_params=_params("arbitrary", "arbitrary"),
        name="ssm_branch",
    )(ssm_in, ssm_in, bc, cc, gc, powers, d_skip, w_glu)


SWA_GROUP = SWA_HEADS // SWA_KV_HEADS
SWA_BLOCKS = SEQ_BLOCK // WINDOW


SWA_VT_ROWS = SWA_HEAD_DIM + 16


def _swa_kernel(l, sink_ref, q_ref, kv_ref, prev_ref, dz_ref, o_ref, kbuf, vtbuf, ytbuf):
    i = pl.program_id(1)
    hd = SWA_HEAD_DIM
    kv_w = SWA_KV_HEADS * hd
    win = WINDOW
    cols = SWA_GROUP * win

    to_log2 = hd ** -0.5 * LOG2E
    kbuf[0:win, :] = prev_ref[:, 0:kv_w]
    kbuf[win:, :] = kv_ref[:, 0:kv_w]
    v_all = jnp.concatenate([prev_ref[:, kv_w:2 * kv_w], kv_ref[:, kv_w:2 * kv_w]], axis=0)
    vt = v_all.astype(F32).T
    ones = jnp.ones((SWA_VT_ROWS - hd, vt.shape[1]), BF16)
    for kvh in range(SWA_KV_HEADS):
        vtbuf[kvh * SWA_VT_ROWS:kvh * SWA_VT_ROWS + hd, :] = vt[kvh * hd:(kvh + 1) * hd, :].astype(BF16)
        vtbuf[kvh * SWA_VT_ROWS + hd:(kvh + 1) * SWA_VT_ROWS, :] = ones

    krow = lax.broadcasted_iota(jnp.int32, (2 * win, cols), 0)
    qpos = lax.broadcasted_iota(jnp.int32, (2 * win, cols), 1) % win + win
    rel = qpos - krow
    band = (rel >= 0) & (rel < win)
    first_head = lax.broadcasted_iota(jnp.int32, (1, cols), 1) < win
    kmin = jnp.where(i == 0, win, 0)

    scores = {}
    for j in range(SWA_BLOCKS):
        for kvh in range(SWA_KV_HEADS):
            q2 = jnp.concatenate(
                [q_ref[j * win:(j + 1) * win, (kvh * SWA_GROUP + g) * hd:(kvh * SWA_GROUP + g + 1) * hd]
                 for g in range(SWA_GROUP)], axis=0)
            k2 = kbuf[j * win:(j + 2) * win, kvh * hd:(kvh + 1) * hd]
            scores[j, kvh] = _dot_nt(k2, q2) * to_log2
    for j in range(SWA_BLOCKS):
        valid = band & (krow >= kmin) if j == 0 else band
        for kvh in range(SWA_KV_HEADS):
            s = jnp.where(valid, scores[j, kvh], NEG_BIG)
            sink = jnp.where(first_head, sink_ref[l, kvh * SWA_GROUP] * LOG2E,
                             sink_ref[l, kvh * SWA_GROUP + 1] * LOG2E)
            m = jnp.maximum(jnp.max(s, axis=0, keepdims=True), sink)
            p = jnp.exp2(s - m).astype(BF16)
            acc = _dot(vtbuf[kvh * SWA_VT_ROWS:(kvh + 1) * SWA_VT_ROWS, j * win:(j + 2) * win], p)
            denom = acc[hd:hd + 1, :] + jnp.exp2(sink - m)
            o = acc[0:hd, :] * (1.0 / denom)
            for g in range(SWA_GROUP):
                h = kvh * SWA_GROUP + g
                ytbuf[h * hd:(h + 1) * hd, j * win:(j + 1) * win] = o[:, g * win:(g + 1) * win]
    o_ref[...] = (ytbuf[...].T * _silu(dz_ref[...])).astype(o_ref.dtype)


def _swa_branch(l, swa_in, dz, sinks, batch, seq):
    bs = SEQ_BLOCK
    w = BRANCH_W
    nsb = seq // bs
    wins = seq // WINDOW
    prev_spec = pl.BlockSpec((WINDOW, w), lambda b, i: (b * wins + jnp.maximum(i * SWA_BLOCKS - 1, 0), 1))
    return pl.pallas_call(
        functools.partial(_swa_kernel, l),
        grid=(batch, nsb),
        in_specs=[pl.BlockSpec(memory_space=pltpu.SMEM),
                  _cols(w, 0, bs, nsb), _cols(w, 1, bs, nsb), prev_spec, _cols(w, 0, bs, nsb)],
        out_specs=pl.BlockSpec((None, bs, w), lambda b, i: (b, i, 0)),
        out_shape=jax.ShapeDtypeStruct((batch, seq, w), BF16),
        scratch_shapes=[pltpu.VMEM((bs + WINDOW, SWA_KV_HEADS * SWA_HEAD_DIM), BF16),
                        pltpu.VMEM((SWA_KV_HEADS * SWA_VT_ROWS, bs + WINDOW), BF16), pltpu.VMEM((w, bs), F32)],
        compiler_params=_params("parallel", "arbitrary"),
        name="swa_branch",
    )(sinks, swa_in, swa_in, swa_in, dz)


MLA_QK = MLA_HEADS * MLA_SLOT


def _rms(x, g):
    ms = jnp.mean(x * x, axis=-1, keepdims=True)
    return x * lax.rsqrt(ms + RMS_EPS) * g


def _mla_prep_kernel(cq_ref, ckv_ref, krot_ref, cos_ref, sin_ref, qg_ref, kvg_ref, wuq_ref, wukv_ref,
                     qt_ref, k_ref, vt_ref):
    scale = (MLA_NOPE + MLA_ROPE) ** -0.5 * LOG2E
    cq = _rms(cq_ref[...], qg_ref[...]).astype(BF16)
    ckv = _rms(ckv_ref[:, 0:MLA_KV_RANK], kvg_ref[...]).astype(BF16)
    cos = cos_ref[...]
    sin = sin_ref[...]
    k_rope = ckv_ref[:, MLA_KV_RANK:MLA_KV_RANK + LANES] * cos + krot_ref[...] * sin
    kv = _dot(ckv, wukv_ref[...])
    for h in range(MLA_HEADS):
        lo = h * MLA_SLOT
        qm = _dot(cq, wuq_ref[:, lo:lo + MLA_SLOT])
        qr = _dot(cq, wuq_ref[:, MLA_QK + lo:MLA_QK + lo + MLA_SLOT])
        qt_ref[lo:lo + MLA_SLOT, :] = ((qm * cos + qr * sin) * scale).T.astype(BF16)
        k_ref[:, lo:lo + MLA_SLOT] = (kv[:, lo:lo + MLA_SLOT] + k_rope).astype(BF16)
    vt = kv[:, MLA_QK:].T.astype(BF16)
    ones = jnp.ones((MLA_VT_ROWS - MLA_V, vt.shape[1]), BF16)
    for h in range(MLA_HEADS):
        vt_ref[h * MLA_VT_ROWS:h * MLA_VT_ROWS + MLA_V, :] = vt[h * MLA_V:(h + 1) * MLA_V, :]
        vt_ref[h * MLA_VT_ROWS + MLA_V:(h + 1) * MLA_VT_ROWS, :] = ones


def _mla_prep(l, mla_in, cos_t, sin_t, q_norm_g, kv_norm_g, w_uq_ext, w_ukv_ext, seq):
    t = mla_in.shape[0]
    tm = ATTN_BLK
    nsb = seq // tm
    vw = MLA_HEADS * MLA_V
    vt_rows = MLA_HEADS * MLA_VT_ROWS
    return pl.pallas_call(
        _mla_prep_kernel,
        grid=(t // tm,),
        in_specs=[pl.BlockSpec((tm, MLA_Q_RANK), lambda i: (i, 0)),
                  pl.BlockSpec((tm, MLA_Q_RANK), lambda i: (i, 1)),
                  pl.BlockSpec((tm, LANES), lambda i: (i, 2 * MLA_Q_RANK // LANES)),
                  pl.BlockSpec((tm, LANES), lambda i: (i % nsb, 0)),
                  pl.BlockSpec((tm, LANES), lambda i: (i % nsb, 0)),
                  _resident(l, (1, MLA_Q_RANK)), _resident(l, (1, MLA_KV_RANK)),
                  _resident(l, (MLA_Q_RANK, 2 * MLA_QK)), _resident(l, (MLA_KV_RANK, MLA_QK + vw))],
        out_specs=[pl.BlockSpec((None, MLA_QK, tm), lambda i: (i, 0, 0)),
                   pl.BlockSpec((tm, MLA_QK), lambda i: (i, 0)),
                   pl.BlockSpec((None, vt_rows, tm), lambda i: (i, 0, 0))],
        out_shape=[jax.ShapeDtypeStruct((t // tm, MLA_QK, tm), BF16), jax.ShapeDtypeStruct((t, MLA_QK), BF16),
                   jax.ShapeDtypeStruct((t // tm, vt_rows, tm), BF16)],
        compiler_params=_params("parallel"),
        name="mla_prep",
    )(mla_in, mla_in, mla_in, cos_t, sin_t, q_norm_g, kv_norm_g, w_uq_ext, w_ukv_ext)


def _attn_kernel(qt_ref, k_ref, vt_ref, bz_ref, o_ref, m_sc, acc_sc, s_sc):
    i = pl.program_id(1)
    blk = ATTN_BLK
    krow = lax.broadcasted_iota(jnp.int32, (blk, blk), 0)
    qcol = lax.broadcasted_iota(jnp.int32, (blk, blk), 1)
    causal = krow <= qcol
    m_sc[...] = jnp.full(m_sc.shape, NEG_BIG, F32)
    acc_sc[...] = jnp.zeros(acc_sc.shape, F32)

    all_heads = tuple(range(MLA_HEADS))

    def score(j, slot, heads=all_heads):
        start = j * blk if isinstance(j, int) else pl.multiple_of(j * blk, blk)
        for h in heads:
            kb = k_ref[pl.ds(start, blk), h * MLA_SLOT:(h + 1) * MLA_SLOT]
            s_sc[slot, h] = _dot(kb, qt_ref[h * MLA_SLOT:(h + 1) * MLA_SLOT, :])

    def absorb(j, slot, masked, heads=all_heads):
        for h in heads:
            s = s_sc[slot, h]
            if masked:
                s = jnp.where(causal, s, NEG_BIG)
            m_prev = m_sc[h]
            m_new = jnp.maximum(m_prev, jnp.max(s, axis=0, keepdims=True))
            alpha = jnp.exp2(m_prev - m_new)
            p = jnp.exp2(s - m_new).astype(BF16)
            vb = vt_ref[j, h * MLA_VT_ROWS:(h + 1) * MLA_VT_ROWS, :]
            acc_sc[h] = alpha * acc_sc[h] + _dot(vb, p)
            m_sc[h] = m_new

    score(0, 0)

    def body(t, c):
        j = 2 * t
        for h in all_heads:
            score(j + 1, 1, (h,))
            absorb(j, 0, False, (h,))
        for h in all_heads:
            score(j + 2, 0, (h,))
            absorb(j + 1, 1, False, (h,))
        return c

    lax.fori_loop(0, i // 2, body, 0)

    @pl.when(i % 2 == 0)
    def _():
        absorb(i, 0, True)

    @pl.when(i % 2 == 1)
    def _():
        for h in all_heads:
            score(i, 1, (h,))
            absorb(i - 1, 0, False, (h,))
        absorb(i, 1, True)
    outs = [acc_sc[h, 0:MLA_V, :] * (1.0 / acc_sc[h, MLA_V:MLA_V + 1, :]) for h in range(MLA_HEADS)]
    y = jnp.concatenate(outs, axis=0).T
    o_ref[...] = (y * _silu(bz_ref[...])).astype(o_ref.dtype)


def _mla_attention(qt, k, vt, bz, batch, seq):
    blk = ATTN_BLK
    nq = seq // blk
    vt_rows = MLA_HEADS * MLA_VT_ROWS
    return pl.pallas_call(
        _attn_kernel,
        grid=(batch, nq),
        in_specs=[pl.BlockSpec((None, MLA_QK, blk), lambda b, i: (b * nq + i, 0, 0)),
                  pl.BlockSpec((None, seq, MLA_QK), lambda b, i: (b, 0, 0)),
                  pl.BlockSpec((None, nq, vt_rows, blk), lambda b, i: (b, 0, 0, 0)),
                  _cols(BRANCH_W, 0, blk, nq)],
        out_specs=pl.BlockSpec((None, blk, BRANCH_W), lambda b, i: (b, i, 0)),
        out_shape=jax.ShapeDtypeStruct((batch, seq, BRANCH_W), BF16),
        scratch_shapes=[pltpu.VMEM((MLA_HEADS, 1, blk), F32), pltpu.VMEM((MLA_HEADS, MLA_VT_ROWS, blk), F32),
                        pltpu.VMEM((2, MLA_HEADS, blk, blk), F32)],
        compiler_params=_params("parallel", "arbitrary"),
        name="mla_attention",
    )(qt, k.reshape(batch, seq, MLA_QK), vt.reshape(batch, nq, vt_rows, blk), bz)


def _merge_kernel(x_ref, ya_ref, yb_ref, yc_ref, yd_ref, p_ref, wm_ref, bm_ref, wb_ref, wo_ref,
                  lng_ref, lnb_ref, wp_ref, wpg_ref, pg_ref, o_ref):
    d = D_MODEL
    y_refs = (ya_ref, yb_ref, yc_ref, yd_ref)
    parts = MERGE_TM // MERGE_ROWS

    def gated_merge(r):
        rows = slice(r * MERGE_ROWS, (r + 1) * MERGE_ROWS)
        xb = x_ref[rows, :].astype(BF16)
        merged = None
        for n, y_ref in enumerate(y_refs):
            gate = _sigmoid(_dot(xb, wm_ref[:, n * d:(n + 1) * d]) + bm_ref[:, n * d:(n + 1) * d])
            term = gate * _dot(y_ref[rows, :], wb_ref[n])
            merged = term if merged is None else merged + term
        return merged

    def post_norm(r, merged):
        rows = slice(r * MERGE_ROWS, (r + 1) * MERGE_ROWS)
        z = DEEPNORM_ALPHA * x_ref[rows, :] + _dot(merged.astype(BF16), wo_ref[...])
        mu = jnp.mean(z, axis=-1, keepdims=True)
        zc = z - mu
        var = jnp.mean(zc * zc, axis=-1, keepdims=True)
        return zc * lax.rsqrt(var + LN_EPS) * lng_ref[...] + lnb_ref[...]

    def embed(r, xn):
        rows = slice(r * MERGE_ROWS, (r + 1) * MERGE_ROWS)
        e = _dot(p_ref[rows, :].astype(BF16), wp_ref[...]) * _sigmoid(_dot(xn.astype(BF16), wpg_ref[...]))
        o_ref[rows, :] = xn + _rms(e, pg_ref[...])

    merged, xn = {}, {}
    for t in range(parts + 2):
        if t < parts:
            merged[t] = gated_merge(t)
        if 0 <= t - 1 < parts:
            xn[t - 1] = post_norm(t - 1, merged.pop(t - 1))
        if 0 <= t - 2 < parts:
            embed(t - 2, xn.pop(t - 2))


def _merge(l, x2d, ys, p3d, w_merge, b_merge, w_branch, w_out, ln_g, ln_b, w_ple, w_ple_gate, ple_norm_g):
    t = x2d.shape[0]
    tm = MERGE_TM
    d = D_MODEL
    row_spec = lambda width: pl.BlockSpec((tm, width), lambda i: (i, 0))
    return pl.pallas_call(
        _merge_kernel,
        grid=(t // tm,),
        in_specs=[row_spec(d)] + [row_spec(BRANCH_W)] * N_BRANCH + [
                  pl.BlockSpec((None, tm, PLE_DIM), lambda i: (l, i, 0)),
                  _resident(l, (d, N_BRANCH * d)), _resident(l, (1, N_BRANCH * d)),
                  _resident(l, (N_BRANCH, BRANCH_W, d)), _resident(l, (d, d)),
                  _resident(l, (1, d)), _resident(l, (1, d)), _resident(l, (PLE_DIM, d)), _resident(l, (d, d)),
                  _resident(l, (1, d))],
        out_specs=row_spec(d),
        out_shape=jax.ShapeDtypeStruct((t, d), F32),
        compiler_params=_params("parallel"),
        name="merge",
    )(x2d, *ys, p3d, w_merge, b_merge, w_branch, w_out, ln_g, ln_b, w_ple, w_ple_gate, ple_norm_g)


def _rot_cols(w):
    half = w.shape[-1] // 2
    return jnp.concatenate([-w[..., half:], w[..., :half]], axis=-1)


def _zero_cols(w, n):
    return jnp.zeros(w.shape[:-1] + (n,), w.dtype)


def _rope_slot(w):
    return jnp.concatenate([_zero_cols(w, MLA_NOPE), w, _zero_cols(w, MLA_SLOT - MLA_NOPE - MLA_ROPE)], axis=-1)


def _in_weights(w_in):
    o = [0]
    for s in (256, 256, 256, 256, 128, 32, 256, 256, 256, 256, 128, 128, 256):
        o.append(o[-1] + s)
    col = lambda k: w_in[..., o[k]:o[k + 1]]
    a_val, a_gate, a_z, c_q, c_kv, k_r, b_z, u, c_z, q, k, v, d_z = (col(k) for k in range(13))
    groups = [a_val, a_gate, a_z, c_q, c_kv, _rope_slot(k_r), _rope_slot(_rot_cols(k_r)), b_z, u, c_z, q, k, v, d_z]
    return jnp.concatenate(groups, axis=-1).astype(BF16)


def _uq_weights(w_uq):
    hd = MLA_NOPE + MLA_ROPE
    main, rot = [], []
    for h in range(MLA_HEADS):
        head = w_uq[..., h * hd:(h + 1) * hd]
        main.append(jnp.concatenate([head, _zero_cols(w_uq, MLA_SLOT - hd)], axis=-1))
        rot.append(_rope_slot(_rot_cols(head[..., MLA_NOPE:])))
    return jnp.concatenate(main + rot, axis=-1).astype(BF16)


def _ukv_weights(w_ukv):
    hd = MLA_NOPE + MLA_V
    ks, vs = [], []
    for h in range(MLA_HEADS):
        ks.append(jnp.concatenate([w_ukv[..., h * hd:h * hd + MLA_NOPE], _zero_cols(w_ukv, MLA_SLOT - MLA_NOPE)],
                                  axis=-1))
        vs.append(w_ukv[..., h * hd + MLA_NOPE:(h + 1) * hd])
    return jnp.concatenate(ks + vs, axis=-1).astype(BF16)


def _rope_tables(seq):
    pos = jnp.arange(seq, dtype=F32)
    inv_freq = ROPE_THETA ** (-jnp.arange(0, MLA_ROPE, 2, dtype=F32) / MLA_ROPE)
    ang = pos[:, None] * inv_freq[None, :]
    cos, sin = jnp.cos(ang), jnp.sin(ang)
    ones = jnp.ones((seq, MLA_NOPE), F32)
    zeros = jnp.zeros((seq, MLA_NOPE), F32)
    pad = jnp.zeros((seq, MLA_SLOT - MLA_NOPE - MLA_ROPE), F32)
    cos_t = jnp.concatenate([ones, cos, cos, pad], axis=1)
    sin_t = jnp.concatenate([zeros, sin, sin, pad], axis=1)
    return cos_t, sin_t


def _ssm_weights(a_re, a_im, log_dt, b_re, b_im, c_re, c_im):
    g, p, h = SSM_GROUPS, SSM_STATE, SSM_GROUP
    nl = a_re.shape[0]
    dt = jnp.exp(log_dt.astype(F32))[..., None]
    lr, li = a_re.astype(F32), a_im.astype(F32)
    mag = jnp.exp(lr * dt)
    lb_re, lb_im = mag * jnp.cos(li * dt), mag * jnp.sin(li * dt)
    den = lr * lr + li * li
    nr, ni = lb_re - 1.0, lb_im
    f_re = ((nr * lr + ni * li) / den)[..., None]
    f_im = ((ni * lr - nr * li) / den)[..., None]
    bb_re = (f_re * b_re - f_im * b_im)[:, None]
    bb_im = (f_re * b_im + f_im * b_re)[:, None]
    c_re, c_im = c_re.astype(F32)[:, None], c_im.astype(F32)[:, None]
    prs, pis = [jnp.ones_like(lb_re)], [jnp.zeros_like(lb_im)]
    for _ in range(SSM_SUB):
        qr, qi = prs[-1], pis[-1]
        prs.append(qr * lb_re - qi * lb_im)
        pis.append(qr * lb_im + qi * lb_re)
    pw_r, pw_i = jnp.stack(prs, axis=1), jnp.stack(pis, axis=1)
    rows_gh = lambda m: m.reshape(nl, SSM_SUB, g * h, m.shape[-1])
    lanes = lambda m: jnp.tile(m, LANES // m.shape[-1])
    qr, qi = pw_r[:, :SSM_SUB, :, :, None], pw_i[:, :SSM_SUB, :, :, None]
    vr = jnp.swapaxes(qr * bb_re - qi * bb_im, -1, -2)
    vi = jnp.swapaxes(qr * bb_im + qi * bb_re, -1, -2)
    bc = lanes(jnp.stack([rows_gh(vr), rows_gh(vi)], axis=2))[:, ::-1]
    qr, qi = pw_r[:, 1:, :, None, :], pw_i[:, 1:, :, None, :]
    cc = lanes(jnp.stack([rows_gh(c_re * qr - c_im * qi), -rows_gh(c_re * qi + c_im * qr)], axis=2))
    lag = jnp.sum(vr[..., None, :] * c_re[:, :, :, None] - vi[..., None, :] * c_im[:, :, :, None], axis=-1)
    gc = lanes(rows_gh(lag))
    mr, mi = pw_r[:, SSM_SUB].reshape(nl, 1, g * p), pw_i[:, SSM_SUB].reshape(nl, 1, g * p)
    mu = [(mr, mi)]
    for _ in range(7):
        qr, qi = mu[-1]
        mu.append((qr * mr - qi * mi, qr * mi + qi * mr))
    cat = lambda q: jnp.concatenate(q, axis=-1)
    rowid = jnp.arange(8)[:, None]
    tables = [jnp.where(rowid >= (1 << lvl), cat(mu[(1 << lvl) - 1]), 0.0) for lvl in range(3)]
    tables.append(jnp.concatenate([cat(q) for q in mu], axis=1))
    return bc, cc, gc, jnp.stack(tables, axis=1)


def kernel(x, p, w_in, w_merge, b_merge, conv_w, conv_b, conv_norm_g, conv_norm_b, w_pw2, mla_q_norm_g,
           mla_kv_norm_g, w_uq, w_ukv, ssm_a_re, ssm_a_im, ssm_log_dt, ssm_b_re, ssm_b_im, ssm_c_re,
           ssm_c_im, ssm_d, w_glu, attn_sinks, w_branch, w_out, ln_g, ln_b, w_ple, w_ple_gate, ple_norm_g):
    batch, seq, d = x.shape
    t = batch * seq
    rows = lambda a: a.reshape(DEPTH, 1, -1).astype(F32)
    bf = lambda a: a.astype(BF16)
    cos_t, sin_t = _rope_tables(seq)
    w_in_ext, w_uq_ext, w_ukv_ext = _in_weights(w_in), _uq_weights(w_uq), _ukv_weights(w_ukv)
    bp, cp, gd, sc = _ssm_weights(ssm_a_re, ssm_a_im, ssm_log_dt, ssm_b_re, ssm_b_im, ssm_c_re, ssm_c_im)
    conv_p = (conv_w.astype(F32), rows(conv_b), rows(conv_norm_g), rows(conv_norm_b), bf(w_pw2))
    mla_p = (rows(mla_q_norm_g), rows(mla_kv_norm_g), w_uq_ext, w_ukv_ext)
    ssm_p = (bp, cp, gd, sc, rows(ssm_d), bf(w_glu))
    merge_p = (bf(w_merge), rows(b_merge), bf(w_branch), bf(w_out), rows(ln_g), rows(ln_b), bf(w_ple),
               bf(w_ple_gate), rows(ple_norm_g))
    sinks = attn_sinks.astype(F32)
    p3d = p.reshape(DEPTH, t, PLE_DIM)
    x2d = x.reshape(t, d)
    for l in range(DEPTH):
        conv_in, mla_in, bz, ssm_in, swa_in, dz = _in_proj(l, x2d, w_in_ext)
        y_a = _conv_branch(l, conv_in, *conv_p, batch, seq)
        qt, k, vt = _mla_prep(l, mla_in, cos_t, sin_t, *mla_p, seq)
        y_b = _mla_attention(qt, k, vt, bz, batch, seq)
        y_c = _ssm_branch(l, ssm_in, *ssm_p, batch, seq)
        y_d = _swa_branch(l, swa_in, dz, sinks, batch, seq)
        ys = [y.reshape(t, BRANCH_W) for y in (y_a, y_b, y_c, y_d)]
        x2d = _merge(l, x2d, ys, p3d, *merge_p)
    return x2d.reshape(batch, seq, d)
```
